```python
import jax, jax.numpy as jnp
from jax import lax
import numpy as np

D_MODEL = 4096
BATCH = 1
SEQ = 16384
DEPTH = 1
DEC_BATCH = 32
DEC_SEQ = 32
PAST_LEN = 1024

CHUNK = 64
N_PAST_CHUNKS = 8
BAND_PAST = N_PAST_CHUNKS * CHUNK
BAND = BAND_PAST + CHUNK
D_MIX = D_MODEL
D_CONV = D_MIX // 2
D_ATTN = D_MIX - D_CONV
HEAD_DIM = 128
N_HEADS = D_ATTN // HEAD_DIM
CONV_WIDTH = 3
MAX_REL = 256
NORM_EPS = 1e-6
ATTN_SCALE = HEAD_DIM ** -0.5
ADA_INIT = 0.2
IN_SPLITS = (D_CONV, 2 * D_CONV, 3 * D_CONV, 4 * D_CONV,
             4 * D_CONV + D_ATTN, 4 * D_CONV + 2 * D_ATTN, 4 * D_CONV + 3 * D_ATTN)
W_IN_COLS = 4 * D_CONV + 4 * D_ATTN

kernel_name = "hybrid_conv_chunkattn_stream_step"


def rmsnorm(x, g):
    xf = x.astype(jnp.float32)
    xf = xf * lax.rsqrt(jnp.mean(xf * xf, axis=-1, keepdims=True) + NORM_EPS)
    return xf.astype(x.dtype) * g


def branch_inputs(x, c, g_norm, w_ada, b_ada, w_in):
    mod = (c @ w_ada + b_ada)[:, None, :]
    shift, scale, gate = jnp.split(mod, 3, axis=-1)
    h = rmsnorm(x, g_norm) * (1 + scale) + shift
    parts = jnp.split(h @ w_in, IN_SPLITS, axis=-1)
    return parts, gate


def depthwise_conv(u_ext, conv_w, conv_b):
    t = u_ext.shape[1] - (CONV_WIDTH - 1)
    out = conv_b
    for i in range(CONV_WIDTH):
        out = out + conv_w[i] * u_ext[:, i:i + t]
    return out


def rel_bias_lookup(rel_bias, rel):
    return rel_bias[:, jnp.clip(rel, -MAX_REL, MAX_REL) + MAX_REL]


def band_attention(q, k, v, bias, mask):
    s = jnp.einsum('bqhd,bkhd->bhqk', q, k).astype(jnp.float32) * ATTN_SCALE + bias.astype(jnp.float32)
    if mask is not None:
        s = jnp.where(mask, s, -jnp.inf)
    p = jax.nn.softmax(s, axis=-1).astype(v.dtype)
    return jnp.einsum('bhqk,bkhd->bqhd', p, v)


def merge(x, a, o, z_conv, z_attn, gate, w_out):
    mixed = jnp.concatenate([a * jax.nn.silu(z_conv), o * jax.nn.silu(z_attn)], axis=-1)
    return x + gate * (mixed @ w_out)


def prompt_layer(x, c, g_norm, w_ada, b_ada, w_in, conv_w, conv_b, rel_bias, w_out):
    bsz, seq, _ = x.shape
    (xin, bg, cg, z_conv, q, k, v, z_attn), gate = branch_inputs(x, c, g_norm, w_ada, b_ada, w_in)
    u = cg * xin
    u_ext = jnp.pad(u, ((0, 0), (CONV_WIDTH - 1, 0), (0, 0)))
    a = bg * depthwise_conv(u_ext, conv_w, conv_b)
    n_chunks = seq // CHUNK
    q, k, v = (t.reshape(bsz, seq, N_HEADS, HEAD_DIM) for t in (q, k, v))
    pad = ((0, 0), (BAND_PAST, 0), (0, 0), (0, 0))
    kp, vp = jnp.pad(k, pad), jnp.pad(v, pad)
    offs = jnp.arange(BAND)
    bias = rel_bias_lookup(rel_bias, jnp.arange(CHUNK)[:, None] + BAND_PAST - offs[None, :])
    q_chunks = jnp.moveaxis(q.reshape(bsz, n_chunks, CHUNK, N_HEADS, HEAD_DIM), 1, 0)

    def one_chunk(args):
        ci, qc = args
        start = ci * CHUNK
        kb = lax.dynamic_slice_in_dim(kp, start, BAND, axis=1)
        vb = lax.dynamic_slice_in_dim(vp, start, BAND, axis=1)
        mask = (start - BAND_PAST + offs >= 0)[None, :]
        return band_attention(qc, kb, vb, bias, mask)

    o = lax.map(one_chunk, (jnp.arange(n_chunks), q_chunks))
    o = jnp.moveaxis(o, 0, 1).reshape(bsz, seq, D_ATTN)
    x_new = merge(x, a, o, z_conv, z_attn, gate, w_out)
    rows = min(BAND_PAST, seq)
    return x_new, k[:, seq - rows:], v[:, seq - rows:], u[:, seq - (CONV_WIDTH - 1):]


def sample_layer(x, c, cache_k, cache_v, cache_conv, g_norm, w_ada, b_ada, w_in, conv_w, conv_b, rel_bias, w_out):
    bsz, t, _ = x.shape
    (xin, bg, cg, z_conv, q, k, v, z_attn), gate = branch_inputs(x, c, g_norm, w_ada, b_ada, w_in)
    u = cg * xin
    u_ext = jnp.concatenate([cache_conv, u], axis=1)
    a = bg * depthwise_conv(u_ext, conv_w, conv_b)
    q, k, v = (z.reshape(bsz, t, N_HEADS, HEAD_DIM) for z in (q, k, v))
    r = cache_k.shape[1]
    kc = jnp.concatenate([cache_k, k], axis=1)
    vc = jnp.concatenate([cache_v, v], axis=1)
    bias = rel_bias_lookup(rel_bias, r + jnp.arange(t)[:, None] - jnp.arange(r + t)[None, :])
    o = band_attention(q, kc, vc, bias, None).reshape(bsz, t, D_ATTN)
    x_new = merge(x, a, o, z_conv, z_attn, gate, w_out)
    return x_new, k, v, u_ext[:, -(CONV_WIDTH - 1):]


def setup_inputs(seed: int = 0) -> dict:
    key = jax.random.key(seed)
    ks = jax.random.split(key, 16)
    f32 = jnp.float32
    r = min(BAND_PAST, PAST_LEN)
    nrm = lambda k, s: jax.random.normal(k, s, f32)
    return {
        'x_prompt': nrm(ks[0], (BATCH, SEQ, D_MODEL)),
        'x_sample': nrm(ks[1], (DEC_BATCH, DEC_SEQ, D_MODEL)),
        'cache_k': nrm(ks[2], (DEPTH, DEC_BATCH, r, N_HEADS, HEAD_DIM)),
        'cache_v': nrm(ks[3], (DEPTH, DEC_BATCH, r, N_HEADS, HEAD_DIM)),
        'cache_conv': nrm(ks[4], (DEPTH, DEC_BATCH, CONV_WIDTH - 1, D_CONV)),
        'c_prompt': nrm(ks[5], (BATCH, D_MODEL)),
        'c_sample': nrm(ks[6], (DEC_BATCH, D_MODEL)),
        'g_norm': 1.0 + 0.01 * nrm(ks[7], (DEPTH, D_MODEL)),
        'w_ada': ADA_INIT * D_MODEL ** -0.5 * nrm(ks[8], (DEPTH, D_MODEL, 3 * D_MODEL)),
        'b_ada': 0.02 * nrm(ks[9], (DEPTH, 3 * D_MODEL)),
        'w_in': D_MODEL ** -0.5 * nrm(ks[10], (DEPTH, D_MODEL, W_IN_COLS)),
        'conv_w': CONV_WIDTH ** -0.5 * nrm(ks[11], (DEPTH, CONV_WIDTH, D_CONV)),
        'conv_b': 0.01 * nrm(ks[12], (DEPTH, D_CONV)),
        'rel_bias': 0.5 * nrm(ks[13], (DEPTH, N_HEADS, 2 * MAX_REL + 1)),
        'w_out': D_MIX ** -0.5 * nrm(ks[14], (DEPTH, D_MIX, D_MODEL)),
        'g_final': 1.0 + 0.01 * nrm(ks[15], (D_MODEL,)),
    }


def reference(x_prompt, x_sample, cache_k, cache_v, cache_conv, c_prompt, c_sample,
              g_norm, w_ada, b_ada, w_in, conv_w, conv_b, rel_bias, w_out, g_final):
    xp, xs = x_prompt, x_sample
    kp_l, vp_l, up_l, ks_l, vs_l, us_l = [], [], [], [], [], []
    for l in range(DEPTH):
        xp, kp, vp, up = prompt_layer(xp, c_prompt, g_norm[l], w_ada[l], b_ada[l], w_in[l],
                                      conv_w[l], conv_b[l], rel_bias[l], w_out[l])
        xs, kn, vn, un = sample_layer(xs, c_sample, cache_k[l], cache_v[l], cache_conv[l],
                                      g_norm[l], w_ada[l], b_ada[l], w_in[l],
                                      conv_w[l], conv_b[l], rel_bias[l], w_out[l])
        kp_l.append(kp); vp_l.append(vp); up_l.append(up)
        ks_l.append(kn); vs_l.append(vn); us_l.append(un)
    y_prompt = rmsnorm(xp, g_final)
    y_sample = rmsnorm(xs, g_final)
    return (y_prompt, y_sample, jnp.stack(kp_l), jnp.stack(vp_l), jnp.stack(up_l),
            jnp.stack(ks_l), jnp.stack(vs_l), jnp.stack(us_l))
```

```python
import functools

import jax
import jax.numpy as jnp
from jax import lax
from jax.experimental import pallas as pl
from jax.experimental.pallas import tpu as pltpu

CHUNK = 64
N_PAST_CHUNKS = 8
BAND_PAST = N_PAST_CHUNKS * CHUNK
HEAD_DIM = 128
CONV_WIDTH = 3
MAX_REL = 256
NORM_EPS = 1e-6
ATTN_SCALE = HEAD_DIM ** -0.5

V7X_LANES = 128
V7X_SUBLANES = 8
V7X_VMEM_BYTES = 64 * 1024 * 1024

F32 = jnp.float32
BF16 = jnp.bfloat16


def _vmem_limit(block_bytes):
    need = 2 * block_bytes + 16 * 1024 * 1024
    return int(min(need, V7X_VMEM_BYTES - 4 * 1024 * 1024))


def _nbytes(shape, dtype):
    n = 1
    for s in shape:
        n *= s
    return n * jnp.dtype(dtype).itemsize


def _params(semantics, blocks):
    total = sum(_nbytes(s, d) for s, d in blocks)
    return pltpu.CompilerParams(dimension_semantics=semantics, vmem_limit_bytes=_vmem_limit(total))


def _silu(z):
    return z * jax.nn.sigmoid(z)


def _ada_kernel(c_ref, w_ref, b_ref, o_ref):
    c = c_ref[...].astype(BF16)
    w = w_ref[...].astype(BF16)
    o_ref[...] = jnp.dot(c, w, preferred_element_type=F32) + b_ref[...]


def _ada(c, w_ada, b_ada, *, tn=512):
    r, d = c.shape
    n = w_ada.shape[1]
    assert n % tn == 0 and r % V7X_SUBLANES == 0
    blocks = [((r, d), F32), ((d, tn), F32), ((1, tn), F32), ((r, tn), F32), ((d, tn), BF16)]
    return pl.pallas_call(
        _ada_kernel,
        grid=(n // tn,),
        in_specs=[pl.BlockSpec((r, d), lambda j: (0, 0)),
                  pl.BlockSpec((d, tn), lambda j: (0, j)),
                  pl.BlockSpec((1, tn), lambda j: (0, j))],
        out_specs=pl.BlockSpec((r, tn), lambda j: (0, j)),
        out_shape=jax.ShapeDtypeStruct((r, n), F32),
        compiler_params=_params(("arbitrary",), blocks),
        name="ada",
    )(c, w_ada, b_ada)


def _prep_kernel(x_ref, g_ref, scale_ref, shift_ref, h_ref):
    x = x_ref[...]
    ms = jnp.mean(x * x, axis=-1, keepdims=True)
    xn = x * lax.rsqrt(ms + NORM_EPS)
    h = xn * g_ref[...] * (1.0 + scale_ref[...]) + shift_ref[...]
    h_ref[...] = h.astype(BF16)


def _prep(x, g, scale, shift, *, nb, t):
    b, s, d = x.shape
    assert b % nb == 0 and s % t == 0
    blocks = [((nb, t, d), F32), ((nb, t, d), BF16), ((nb, t, d), F32)]
    return pl.pallas_call(
        _prep_kernel,
        grid=(b // nb, s // t),
        in_specs=[pl.BlockSpec((nb, t, d), lambda i, j: (i, j, 0)),
                  pl.BlockSpec((1, 1, d), lambda i, j: (0, 0, 0)),
                  pl.BlockSpec((nb, 1, d), lambda i, j: (i, 0, 0)),
                  pl.BlockSpec((nb, 1, d), lambda i, j: (i, 0, 0))],
        out_specs=pl.BlockSpec((nb, t, d), lambda i, j: (i, j, 0)),
        out_shape=jax.ShapeDtypeStruct((b, s, d), BF16),
        compiler_params=_params(("arbitrary", "arbitrary"), blocks),
        name="prep",
    )(x, g, scale, shift)


def _conv_epilogue(u, prev1, prev2, bg, z, cw_ref, cb_ref):
    conv = cb_ref[...] + cw_ref[0:1, :] * prev2
    conv = conv + cw_ref[1:2, :] * prev1
    conv = conv + cw_ref[2:3, :] * u
    return (bg * conv) * _silu(z)


def _conv_prompt_kernel(h_ref, wx_ref, wb_ref, wc_ref, wz_ref, cw_ref, cb_ref,
                        mix_ref, tail_ref, halo_ref):
    i = pl.program_id(0)
    j = pl.program_id(1)
    h = h_ref[...]
    xin = jnp.dot(h, wx_ref[...], preferred_element_type=F32)
    bg = jnp.dot(h, wb_ref[...], preferred_element_type=F32)
    cg = jnp.dot(h, wc_ref[...], preferred_element_type=F32)
    z = jnp.dot(h, wz_ref[...], preferred_element_type=F32)
    u = cg * xin
    tm = u.shape[0]

    @pl.when(i == 0)
    def _():
        halo_ref[j] = jnp.zeros(halo_ref.shape[1:], F32)

    halo = halo_ref[j]
    last1 = halo[V7X_SUBLANES - 1:V7X_SUBLANES, :]
    last2 = halo[V7X_SUBLANES - 2:V7X_SUBLANES - 1, :]
    row = lax.broadcasted_iota(jnp.int32, u.shape, 0)
    prev1 = jnp.where(row == 0, last1, pltpu.roll(u, 1, 0))
    prev2 = jnp.where(row == 0, last2, jnp.where(row == 1, last1, pltpu.roll(u, 2, 0)))
    mix_ref[...] = _conv_epilogue(u, prev1, prev2, bg, z, cw_ref, cb_ref).astype(BF16)
    u_last = u[tm - V7X_SUBLANES:, :]
    halo_ref[j] = u_last
    tail_ref[0] = u_last


def _conv_prompt(h, w_in, conv_w, conv_b, *, d_conv, tm, tn):
    s, d = h.shape
    assert s % tm == 0 and d_conv % tn == 0
    nj = d_conv // tn
    wspec = lambda part: pl.BlockSpec((d, tn), lambda i, j, part=part: (0, part * nj + j))
    blocks = [((tm, d), BF16)] + [((d, tn), BF16)] * 4 + [((tm, tn), BF16)] + [((tm, tn), F32)] * 8
    return pl.pallas_call(
        _conv_prompt_kernel,
        grid=(s // tm, nj),
        in_specs=[pl.BlockSpec((tm, d), lambda i, j: (i, 0)),
                  wspec(0), wspec(1), wspec(2), wspec(3),
                  pl.BlockSpec((CONV_WIDTH, tn), lambda i, j: (0, j)),
                  pl.BlockSpec((1, tn), lambda i, j: (0, j))],
        out_specs=[pl.BlockSpec((tm, tn), lambda i, j: (i, j)),
                   pl.BlockSpec((1, V7X_SUBLANES, tn), lambda i, j: (i, 0, j))],
        out_shape=[jax.ShapeDtypeStruct((s, d_conv), BF16),
                   jax.ShapeDtypeStruct((s // tm, V7X_SUBLANES, d_conv), F32)],
        scratch_shapes=[pltpu.VMEM((nj, V7X_SUBLANES, tn), F32)],
        compiler_params=_params(("arbitrary", "arbitrary"), blocks),
        name="conv_prompt",
    )(h, w_in, w_in, w_in, w_in, conv_w, conv_b)


def _conv_sample_kernel(h_ref, wx_ref, wb_ref, wc_ref, wz_ref, cw_ref, cb_ref, cache_ref,
                        mix_ref, tail_ref, *, nb, t):
    h = h_ref[...]
    xin = jnp.dot(h, wx_ref[...], preferred_element_type=F32)
    bg = jnp.dot(h, wb_ref[...], preferred_element_type=F32)
    cg = jnp.dot(h, wc_ref[...], preferred_element_type=F32)
    z = jnp.dot(h, wz_ref[...], preferred_element_type=F32)
    u = cg * xin
    tn = u.shape[1]
    cache = cache_ref[...]
    c2 = jnp.broadcast_to(cache[:, 0:1, :], (nb, t, tn)).reshape(nb * t, tn)
    c1 = jnp.broadcast_to(cache[:, 1:2, :], (nb, t, tn)).reshape(nb * t, tn)
    pos = lax.broadcasted_iota(jnp.int32, (nb, t, tn), 1).reshape(nb * t, tn)
    prev1 = jnp.where(pos == 0, c1, pltpu.roll(u, 1, 0))
    prev2 = jnp.where(pos == 0, c2, jnp.where(pos == 1, c1, pltpu.roll(u, 2, 0)))
    mix_ref[...] = _conv_epilogue(u, prev1, prev2, bg, z, cw_ref, cb_ref).astype(BF16)
    tail_ref[...] = u.reshape(nb, t, tn)[:, t - V7X_SUBLANES:, :]


def _conv_sample(h, w_in, conv_w, conv_b, cache_conv, *, d_conv, nb, t, tn):
    m, d = h.shape
    assert m == nb * t and d_conv % tn == 0 and t % V7X_SUBLANES == 0 and t >= CONV_WIDTH - 1
    nj = d_conv // tn
    wspec = lambda part: pl.BlockSpec((d, tn), lambda j, part=part: (0, part * nj + j))
    blocks = [((m, d), BF16)] + [((d, tn), BF16)] * 4 + [((m, tn), BF16)] + [((m, tn), F32)] * 10
    return pl.pallas_call(
        functools.partial(_conv_sample_kernel, nb=nb, t=t),
        grid=(nj,),
        in_specs=[pl.BlockSpec((m, d), lambda j: (0, 0)),
                  wspec(0), wspec(1), wspec(2), wspec(3),
                  pl.BlockSpec((CONV_WIDTH, tn), lambda j: (0, j)),
                  pl.BlockSpec((1, tn), lambda j: (0, j)),
                  pl.BlockSpec((nb, CONV_WIDTH - 1, tn), lambda j: (0, 0, j))],
        out_specs=[pl.BlockSpec((m, tn), lambda j: (0, j)),
                   pl.BlockSpec((nb, V7X_SUBLANES, tn), lambda j: (0, 0, j))],
        out_shape=[jax.ShapeDtypeStruct((m, d_conv), BF16),
                   jax.ShapeDtypeStruct((nb, V7X_SUBLANES, d_conv), F32)],
        compiler_params=_params(("arbitrary",), blocks),
        name="conv_sample",
    )(h, w_in, w_in, w_in, w_in, conv_w, conv_b, cache_conv)


def _proj_kernel(*refs, n_parts):
    h = refs[0][...]
    for p in range(n_parts):
        w_ref, o_ref = refs[1 + p], refs[1 + n_parts + p]
        o_ref[...] = jnp.dot(h, w_ref[...], preferred_element_type=F32).astype(o_ref.dtype)


def _proj(h, w_in, *, col_starts, width, out_dtypes, tm, tn, name):
    m, d = h.shape
    n_parts = len(col_starts)
    assert m % tm == 0 and width % tn == 0 and all(c % tn == 0 for c in col_starts)
    wspec = lambda c0: pl.BlockSpec((d, tn), lambda i, j, c0=c0: (0, c0 // tn + j))
    blocks = ([((tm, d), BF16)] + [((d, tn), BF16)] * n_parts
              + [((tm, tn), dt) for dt in out_dtypes] + [((tm, tn), F32)] * n_parts)
    return pl.pallas_call(
        functools.partial(_proj_kernel, n_parts=n_parts),
        grid=(m // tm, width // tn),
        in_specs=[pl.BlockSpec((tm, d), lambda i, j: (i, 0))] + [wspec(c0) for c0 in col_starts],
        out_specs=[pl.BlockSpec((tm, tn), lambda i, j: (i, j)) for _ in out_dtypes],
        out_shape=[jax.ShapeDtypeStruct((m, width), dt) for dt in out_dtypes],
        compiler_params=_params(("arbitrary", "arbitrary"), blocks),
        name=name,
    )(h, *([w_in] * n_parts))


def _bias_seq(rel_bias, offset, n):
    rev = rel_bias[:, ::-1]
    left = offset - MAX_REL
    assert left >= 0
    right = max(n - left - rev.shape[1], 0)
    return jnp.pad(rev, ((0, 0), (left, right)), mode="edge")[:, :n]


def _toeplitz_kernel(seq_ref, o_ref, *, rows, width):
    seq = jnp.broadcast_to(seq_ref[0], (rows, width))
    o_ref[0] = pltpu.roll(seq, 0, 1, stride=1, stride_axis=0)


def _toeplitz(seq, rows):
    nh, width = seq.shape
    blocks = [((1, width), F32), ((rows, width), F32), ((rows, width), F32)]
    return pl.pallas_call(
        functools.partial(_toeplitz_kernel, rows=rows, width=width),
        grid=(nh,),
        in_specs=[pl.BlockSpec((1, 1, width), lambda h: (h, 0, 0))],
        out_specs=pl.BlockSpec((1, rows, width), lambda h: (h, 0, 0)),
        out_shape=jax.ShapeDtypeStruct((nh, rows, width), F32),
        compiler_params=_params(("arbitrary",), blocks),
        name="bias_toeplitz",
    )(seq.reshape(nh, 1, width))


def _attn_prompt_kernel(q_ref, k0_ref, k1_ref, k2_ref, v0_ref, v1_ref, v2_ref, z_ref, bias_ref, o_ref,
                        *, heads, tq, lane0):
    b = pl.program_id(1)
    nk = 3 * tq
    qc = lax.broadcasted_iota(jnp.int32, (tq, nk), 0) // CHUNK
    kc = lax.broadcasted_iota(jnp.int32, (tq, nk), 1) // CHUNK - (2 * tq) // CHUNK
    band = (kc <= qc) & (kc >= qc - N_PAST_CHUNKS)
    visible = band & (kc >= -b * (tq // CHUNK))
    for hh in range(heads):
        sl = slice(hh * HEAD_DIM, (hh + 1) * HEAD_DIM)
        k = jnp.concatenate([k0_ref[:, sl], k1_ref[:, sl], k2_ref[:, sl]], axis=0)
        v = jnp.concatenate([v0_ref[:, sl], v1_ref[:, sl], v2_ref[:, sl]], axis=0)
        s = lax.dot_general(q_ref[:, sl], k, (((1,), (1,)), ((), ())), preferred_element_type=F32)
        s = s * ATTN_SCALE + bias_ref[hh, :, lane0:lane0 + nk]
        s = jnp.where(visible, s, -jnp.inf)
        m = jnp.max(s, axis=-1, keepdims=True)
        p = jnp.exp(s - m)
        l = jnp.sum(p, axis=-1, keepdims=True)
        o = jnp.dot(p.astype(BF16), v, preferred_element_type=F32) / l
        o_ref[:, sl] = (o * _silu(z_ref[:, sl])).astype(BF16)


def _attn_prompt(q, k, v, z, bias, *, tq, heads, lane0):
    s, da = q.shape
    nh = da // HEAD_DIM
    wcols = heads * HEAD_DIM
    assert s % tq == 0 and nh % heads == 0 and tq % CHUNK == 0 and 2 * tq >= BAND_PAST
    kspec = lambda back: pl.BlockSpec((tq, wcols), lambda g, b, back=back: (jnp.maximum(b - back, 0), g))
    blocks = ([((tq, wcols), BF16)] * 8 + [((tq, wcols), F32)] + [((heads,) + bias.shape[1:], F32)]
              + [((tq, 3 * tq), F32)] * 6)
    return pl.pallas_call(
        functools.partial(_attn_prompt_kernel, heads=heads, tq=tq, lane0=lane0),
        grid=(nh // heads, s // tq),
        in_specs=[pl.BlockSpec((tq, wcols), lambda g, b: (b, g)),
                  kspec(2), kspec(1), kspec(0), kspec(2), kspec(1), kspec(0),
                  pl.BlockSpec((tq, wcols), lambda g, b: (b, g)),
                  pl.BlockSpec((heads,) + bias.shape[1:], lambda g, b: (g, 0, 0))],
        out_specs=pl.BlockSpec((tq, wcols), lambda g, b: (b, g)),
        out_shape=jax.ShapeDtypeStruct((s, da), BF16),
        compiler_params=_params(("arbitrary", "arbitrary"), blocks),
        name="attn_prompt",
    )(q, k, k, k, v, v, v, z, bias)


def _attn_sample_kernel(q_ref, kn_ref, vn_ref, ck_ref, cv_ref, z_ref, bias_ref, o_ref, *, nh, r, t, lane0):
    for hh in range(nh):
        sl = slice(hh * HEAD_DIM, (hh + 1) * HEAD_DIM)
        q = q_ref[0, :, sl].astype(BF16)
        dims = (((1,), (1,)), ((), ()))
        sc = lax.dot_general(q, ck_ref[0, :, sl].astype(BF16), dims, preferred_element_type=F32)
        sn = lax.dot_general(q, kn_ref[0, :, sl].astype(BF16), dims, preferred_element_type=F32)
        sc = sc * ATTN_SCALE + bias_ref[hh, :, lane0:lane0 + r]
        sn = sn * ATTN_SCALE + bias_ref[hh, :, lane0 + r:lane0 + r + t]
        m = jnp.maximum(jnp.max(sc, axis=-1, keepdims=True), jnp.max(sn, axis=-1, keepdims=True))
        pc = jnp.exp(sc - m)
        pn = jnp.exp(sn - m)
        l = jnp.sum(pc, axis=-1, keepdims=True) + jnp.sum(pn, axis=-1, keepdims=True)
        o = jnp.dot(pc.astype(BF16), cv_ref[0, :, sl].astype(BF16), preferred_element_type=F32)
        o = o + jnp.dot(pn.astype(BF16), vn_ref[0, :, sl].astype(BF16), preferred_element_type=F32)
        o_ref[0, :, sl] = ((o / l) * _silu(z_ref[0, :, sl])).astype(BF16)


def _attn_sample(q, kn, vn, cache_k, cache_v, z, bias, *, lane0):
    nb, t, da = q.shape
    r = cache_k.shape[1]
    nh = da // HEAD_DIM
    assert r % V7X_LANES == 0 and (lane0 + r) % V7X_LANES == 0
    new = pl.BlockSpec((1, t, da), lambda b: (b, 0, 0))
    old = pl.BlockSpec((1, r, da), lambda b: (b, 0, 0))
    blocks = [((t, da), F32)] * 5 + [((r, da), F32)] * 2 + [(bias.shape, F32)] + [((r, da), BF16)] * 2
    return pl.pallas_call(
        functools.partial(_attn_sample_kernel, nh=nh, r=r, t=t, lane0=lane0),
        grid=(nb,),
        in_specs=[new, new, new, old, old, new, pl.BlockSpec(bias.shape, lambda b: (0, 0, 0))],
        out_specs=new,
        out_shape=jax.ShapeDtypeStruct((nb, t, da), BF16),
        compiler_params=_params(("arbitrary",), blocks),
        name="attn_sample",
    )(q, kn, vn, cache_k, cache_v, z, bias)


def _out_kernel(x_ref, ma_ref, mb_ref, wa_ref, wb_ref, gate_ref, gf_ref, y_ref, res_ref, ss_ref, *, nj, tn):
    j = pl.program_id(1)
    acc = jnp.dot(ma_ref[...], wa_ref[...], preferred_element_type=F32)
    acc = acc + jnp.dot(mb_ref[...], wb_ref[...], preferred_element_type=F32)
    nb, t, _ = x_ref.shape
    res = x_ref[...] + gate_ref[...] * acc.reshape(nb, t, tn)
    res_ref[j] = res
    part = jnp.sum(res * res, axis=-1, keepdims=True)

    @pl.when(j == 0)
    def _():
        ss_ref[...] = part

    @pl.when(j > 0)
    def _():
        ss_ref[...] += part

    @pl.when(j == nj - 1)
    def _():
        inv = lax.rsqrt(ss_ref[...] * (1.0 / (nj * tn)) + NORM_EPS)
        for n in range(nj):
            y_ref[:, :, n * tn:(n + 1) * tn] = (res_ref[n] * inv) * gf_ref[:, :, n * tn:(n + 1) * tn]


def _out(x, mix_a, mix_b, w_out, gate, g_final, *, nb, t, tn):
    b, s, d = x.shape
    half = mix_a.shape[1]
    tm = nb * t
    assert b % nb == 0 and s % t == 0 and d % tn == 0 and w_out.shape[0] == 2 * half
    nj = d // tn
    nt = s // t
    assert nb == 1 or nt == 1, "a row tile must be contiguous in the flattened (B*S) mixed rows"
    blocks =([((nb, t, tn), F32)] + [((tm, half), BF16)] * 2 + [((half, tn), BF16)] * 2
              + [((nb, t, d), F32)] + [((tm, tn), F32)] * 4)
    scratch_bytes = _nbytes((nb, t, d), F32)
    params = pltpu.CompilerParams(
        dimension_semantics=("arbitrary", "arbitrary"),
        vmem_limit_bytes=_vmem_limit(sum(_nbytes(sh, dt) for sh, dt in blocks) + scratch_bytes // 2))
    return pl.pallas_call(
        functools.partial(_out_kernel, nj=nj, tn=tn),
        grid=((b // nb) * nt, nj),
        in_specs=[pl.BlockSpec((nb, t, tn), lambda i, j: (i // nt, i % nt, j)),
                  pl.BlockSpec((tm, half), lambda i, j: (i, 0)),
                  pl.BlockSpec((tm, half), lambda i, j: (i, 0)),
                  pl.BlockSpec((half, tn), lambda i, j: (0, j)),
                  pl.BlockSpec((half, tn), lambda i, j: (1, j)),
                  pl.BlockSpec((nb, 1, tn), lambda i, j: (i // nt, 0, j)),
                  pl.BlockSpec((1, 1, d), lambda i, j: (0, 0, 0))],
        out_specs=pl.BlockSpec((nb, t, d), lambda i, j: (i // nt, i % nt, 0)),
        out_shape=jax.ShapeDtypeStruct((b, s, d), F32),
        scratch_shapes=[pltpu.VMEM((nj, nb, t, tn), F32), pltpu.VMEM((nb, t, 1), F32)],
        compiler_params=params,
        name="out_proj",
    )(x, mix_a, mix_b, w_out, w_out, gate, g_final)


def kernel(x_prompt, x_sample, cache_k, cache_v, cache_conv, c_prompt, c_sample,
           g_norm, w_ada, b_ada, w_in, conv_w, conv_b, rel_bias, w_out, g_final):
    depth = g_norm.shape[0]
    assert depth == 1, "single-layer trunk"
    bp, sp, d = x_prompt.shape
    bs, ts, _ = x_sample.shape
    assert bp == 1
    d_conv = conv_w.shape[-1]
    d_attn = w_out.shape[1] - d_conv
    nh = d_attn // HEAD_DIM
    r = cache_k.shape[2]
    rows_kept = min(BAND_PAST, sp)

    n_c = bp + bs
    pad = (-n_c) % V7X_SUBLANES
    c_all = jnp.concatenate([c_prompt, c_sample, jnp.zeros((pad, d), F32)], axis=0)
    mod = _ada(c_all, w_ada[0], b_ada)
    shift, scale, gate = (mod[:n_c, i * d:(i + 1) * d].reshape(n_c, 1, d) for i in range(3))

    w_in_b = w_in[0].astype(BF16)
    w_out_b = w_out[0].astype(BF16)
    g3 = g_norm.reshape(1, 1, d)
    gf3 = g_final.reshape(1, 1, d)
    attn0 = 4 * d_conv

    tq = 256
    hp = _prep(x_prompt, g3, scale[:bp], shift[:bp], nb=1, t=512).reshape(sp, d)
    mix_conv_p, u_tail_p = _conv_prompt(hp, w_in_b, conv_w[0], conv_b, d_conv=d_conv, tm=512, tn=256)
    qp, kp, vp, zp = _proj(hp, w_in_b, col_starts=[attn0 + i * d_attn for i in range(4)], width=d_attn,
                           out_dtypes=[BF16, BF16, BF16, F32], tm=512, tn=256, name="qkvz_prompt")
    k_keep, v_keep = _proj(hp[sp - rows_kept:], w_in_b, col_starts=[attn0 + d_attn, attn0 + 2 * d_attn],
                           width=d_attn, out_dtypes=[F32, F32], tm=rows_kept, tn=256, name="kv_keep_prompt")
    width_p = 4 * tq
    bias_p = _toeplitz(_bias_seq(rel_bias[0], 3 * tq, width_p), tq)
    mix_attn_p = _attn_prompt(qp, kp, vp, zp, bias_p, tq=tq, heads=4, lane0=tq)
    y_prompt = _out(x_prompt, mix_conv_p, mix_attn_p, w_out_b, gate[:bp], gf3, nb=1, t=512, tn=512)

    hs = _prep(x_sample, g3, scale[bp:], shift[bp:], nb=8, t=ts).reshape(bs * ts, d)
    mix_conv_s, u_tail_s = _conv_sample(hs, w_in_b, conv_w[0], conv_b, cache_conv[0],
                                        d_conv=d_conv, nb=bs, t=ts, tn=256)
    qs, ks, vs, zs = _proj(hs, w_in_b, col_starts=[attn0 + i * d_attn for i in range(4)], width=d_attn,
                           out_dtypes=[F32, F32, F32, F32], tm=bs * ts, tn=256, name="qkvz_sample")
    lane0_s = V7X_LANES
    width_s = lane0_s + r + V7X_LANES * (-(-(2 * ts) // V7X_LANES))
    bias_s = _toeplitz(_bias_seq(rel_bias[0], lane0_s + r, width_s), ts)
    to3 = lambda a: a.reshape(bs, ts, d_attn)
    mix_attn_s = _attn_sample(to3(qs), to3(ks), to3(vs), cache_k[0].reshape(bs, r, d_attn),
                              cache_v[0].reshape(bs, r, d_attn), to3(zs), bias_s, lane0=lane0_s)
    y_sample = _out(x_sample, mix_conv_s, mix_attn_s.reshape(bs * ts, d_attn), w_out_b, gate[bp:], gf3,
                    nb=16, t=ts, tn=512)

    keep = CONV_WIDTH - 1
    new_k_prompt = k_keep.reshape(1, bp, rows_kept, nh, HEAD_DIM)
    new_v_prompt = v_keep.reshape(1, bp, rows_kept, nh, HEAD_DIM)
    new_conv_prompt = u_tail_p[-1, V7X_SUBLANES - keep:, :].reshape(1, bp, keep, d_conv)
    new_k_sample = ks.reshape(1, bs, ts, nh, HEAD_DIM)
    new_v_sample = vs.reshape(1, bs, ts, nh, HEAD_DIM)
    new_conv_sample = u_tail_s[:, V7X_SUBLANES - keep:, :].reshape(1, bs, keep, d_conv)
    return (y_prompt, y_sample, new_k_prompt, new_v_prompt, new_conv_prompt,
            new_k_sample, new_v_sample, new_conv_sample)
```

```python
import functools
import math

import jax
import jax.numpy as jnp
from jax import lax
from jax.experimental import pallas as pl
from jax.experimental.pallas import tpu as pltpu

CHUNK = 64
N_PAST_CHUNKS = 8
BAND_PAST = N_PAST_CHUNKS * CHUNK
HEAD_DIM = 128
CONV_WIDTH = 3
MAX_REL = 256
NORM_EPS = 1e-6
ATTN_SCALE = HEAD_DIM ** -0.5
LOG2E = math.log2(math.e)

V7X_LANES = 128
V7X_SUBLANES = 8
V7X_VMEM_BYTES = 64 * 1024 * 1024

F32 = jnp.float32
BF16 = jnp.bfloat16


def _nbytes(shape, dtype):
    n = 1
    for s in shape:
        n *= s
    return n * jnp.dtype(dtype).itemsize


def _params(semantics, blocks, extra_bytes=0):
    need = 2 * sum(_nbytes(s, d) for s, d in blocks) + extra_bytes + 16 * 1024 * 1024
    limit = int(min(need, V7X_VMEM_BYTES - 4 * 1024 * 1024))
    return pltpu.CompilerParams(dimension_semantics=semantics, vmem_limit_bytes=limit)


def _silu(z):
    return z * jax.nn.sigmoid(z)


def _mm(a, b):
    return jnp.dot(a, b, preferred_element_type=F32)


def _ada_kernel(c_ref, w_ref, b_ref, o_ref):
    o_ref[...] = _mm(c_ref[...].astype(BF16), w_ref[...].astype(BF16)) + b_ref[...]


def _ada(c, w_ada, b_ada, *, tn=512):
    r, d = c.shape
    n = w_ada.shape[1]
    assert n % tn == 0 and r % V7X_SUBLANES == 0
    blocks = [((r, d), F32), ((d, tn), F32), ((1, tn), F32), ((r, tn), F32), ((d, tn), BF16)]
    return pl.pallas_call(
        _ada_kernel,
        grid=(n // tn,),
        in_specs=[pl.BlockSpec((r, d), lambda j: (0, 0)),
                  pl.BlockSpec((d, tn), lambda j: (0, j)),
                  pl.BlockSpec((1, tn), lambda j: (0, j))],
        out_specs=pl.BlockSpec((r, tn), lambda j: (0, j)),
        out_shape=jax.ShapeDtypeStruct((r, n), F32),
        compiler_params=_params(("arbitrary",), blocks),
        name="ada",
    )(c, w_ada, b_ada)


def _prep_kernel(x_ref, g_ref, scale_ref, shift_ref, h_ref):
    x = x_ref[...]
    ms = jnp.mean(x * x, axis=-1, keepdims=True)
    xn = x * lax.rsqrt(ms + NORM_EPS)
    h = xn * g_ref[...] * (1.0 + scale_ref[...]) + shift_ref[...]
    h_ref[...] = h.astype(BF16)


def _prep(x, g, scale, shift, *, nb, t):
    b, s, d = x.shape
    assert b % nb == 0 and s % t == 0
    blocks = [((nb, t, d), F32), ((nb, t, d), BF16), ((nb, t, d), F32)]
    return pl.pallas_call(
        _prep_kernel,
        grid=(b // nb, s // t),
        in_specs=[pl.BlockSpec((nb, t, d), lambda i, j: (i, j, 0)),
                  pl.BlockSpec((1, 1, d), lambda i, j: (0, 0, 0)),
                  pl.BlockSpec((nb, 1, d), lambda i, j: (i, 0, 0)),
                  pl.BlockSpec((nb, 1, d), lambda i, j: (i, 0, 0))],
        out_specs=pl.BlockSpec((nb, t, d), lambda i, j: (i, j, 0)),
        out_shape=jax.ShapeDtypeStruct((b, s, d), BF16),
        compiler_params=_params(("arbitrary", "arbitrary"), blocks),
        name="prep",
    )(x, g, scale, shift)


def _conv_epilogue(u, prev1, prev2, bg, z, cw_ref, cb_ref):
    conv = cb_ref[...] + cw_ref[0:1, :] * prev2
    conv = conv + cw_ref[1:2, :] * prev1
    conv = conv + cw_ref[2:3, :] * u
    return (bg * conv) * _silu(z)


def _inproj_prompt_kernel(h_ref, w0_ref, w1_ref, w2_ref, w3_ref, cw_ref, cb_ref,
                          mix_ref, tail_ref, q_ref, k_ref, v_ref, z_ref, halo_ref, *, nbc, halves):
    i = pl.program_id(0)
    j = pl.program_id(1)
    hm = h_ref.shape[0] // halves

    @pl.when(j < nbc)
    def _conv_group():
        @pl.when(i == 0)
        def _():
            halo_ref[j] = jnp.zeros(halo_ref.shape[1:], F32)

        last8 = halo_ref[j]
        for s in range(halves):
            rows = slice(s * hm, (s + 1) * hm)
            h = h_ref[rows, :]
            xin = _mm(h, w0_ref[...])
            bg = _mm(h, w1_ref[...])
            cg = _mm(h, w2_ref[...])
            z = _mm(h, w3_ref[...])
            u = cg * xin
            last1 = last8[V7X_SUBLANES - 1:V7X_SUBLANES, :]
            last2 = last8[V7X_SUBLANES - 2:V7X_SUBLANES - 1, :]
            row = lax.broadcasted_iota(jnp.int32, u.shape, 0)
            prev1 = jnp.where(row == 0, last1, pltpu.roll(u, 1, 0))
            prev2 = jnp.where(row == 0, last2, jnp.where(row == 1, last1, pltpu.roll(u, 2, 0)))
            mix_ref[rows, :] = _conv_epilogue(u, prev1, prev2, bg, z, cw_ref, cb_ref).astype(BF16)
            last8 = u[hm - V7X_SUBLANES:, :]
        halo_ref[j] = last8
        tail_ref[0] = last8

    @pl.when(j >= nbc)
    def _attn_group():
        for s in range(halves):
            rows = slice(s * hm, (s + 1) * hm)
            h = h_ref[rows, :]
            q_ref[rows, :] = _mm(h, w0_ref[...]).astype(BF16)
            k_ref[rows, :] = _mm(h, w1_ref[...]).astype(BF16)
            v_ref[rows, :] = _mm(h, w2_ref[...]).astype(BF16)
            z_ref[rows, :] = _mm(h, w3_ref[...])


def _inproj_prompt(h, w_in, conv_w, conv_b, *, d_conv, d_attn, tm, tn, halves):
    s, d = h.shape
    assert s % tm == 0 and d_conv % tn == 0 and d_attn % tn == 0 and (tm // halves) % V7X_SUBLANES == 0
    nbc, nba = d_conv // tn, d_attn // tn

    def wspec(p):
        return pl.BlockSpec((d, tn), lambda i, j: (0, jnp.where(j < nbc, p * nbc + j, 4 * nbc + p * nba + j - nbc)))

    cj = lambda j: jnp.minimum(j, nbc - 1)
    aj = lambda j: jnp.maximum(j - nbc, 0)
    attn_spec = pl.BlockSpec((tm, tn), lambda i, j: (i, aj(j)))
    blocks = ([((tm, d), BF16)] + [((d, tn), BF16)] * 4 + [((tm, tn), BF16)] * 4 + [((tm, tn), F32)]
              + [((tm // halves, tn), F32)] * 8)
    return pl.pallas_call(
        functools.partial(_inproj_prompt_kernel, nbc=nbc, halves=halves),
        grid=(s // tm, nbc + nba),
        in_specs=[pl.BlockSpec((tm, d), lambda i, j: (i, 0)),
                  wspec(0), wspec(1), wspec(2), wspec(3),
                  pl.BlockSpec((CONV_WIDTH, tn), lambda i, j: (0, cj(j))),
                  pl.BlockSpec((1, tn), lambda i, j: (0, cj(j)))],
        out_specs=[pl.BlockSpec((tm, tn), lambda i, j: (i, cj(j))),
                   pl.BlockSpec((1, V7X_SUBLANES, tn), lambda i, j: (i, 0, cj(j))),
                   attn_spec, attn_spec, attn_spec, attn_spec],
        out_shape=[jax.ShapeDtypeStruct((s, d_conv), BF16),
                   jax.ShapeDtypeStruct((s // tm, V7X_SUBLANES, d_conv), F32),
                   jax.ShapeDtypeStruct((s, d_attn), BF16),
                   jax.ShapeDtypeStruct((s, d_attn), BF16),
                   jax.ShapeDtypeStruct((s, d_attn), BF16),
                   jax.ShapeDtypeStruct((s, d_attn), F32)],
        scratch_shapes=[pltpu.VMEM((nbc, V7X_SUBLANES, tn), F32)],
        compiler_params=_params(("arbitrary", "arbitrary"), blocks),
        name="inproj_prompt",
    )(h, w_in, w_in, w_in, w_in, conv_w, conv_b)


def _conv_sample_kernel(h_ref, wx_ref, wb_ref, wc_ref, wz_ref, cw_ref, cb_ref, cache_ref,
                        mix_ref, tail_ref, *, nb, t):
    h = h_ref[...]
    xin = _mm(h, wx_ref[...])
    bg = _mm(h, wb_ref[...])
    cg = _mm(h, wc_ref[...])
    z = _mm(h, wz_ref[...])
    u = cg * xin
    tn = u.shape[1]
    cache = cache_ref[...]
    c2 = jnp.broadcast_to(cache[:, 0:1, :], (nb, t, tn)).reshape(nb * t, tn)
    c1 = jnp.broadcast_to(cache[:, 1:2, :], (nb, t, tn)).reshape(nb * t, tn)
    pos = lax.broadcasted_iota(jnp.int32, (nb, t, tn), 1).reshape(nb * t, tn)
    prev1 = jnp.where(pos == 0, c1, pltpu.roll(u, 1, 0))
    prev2 = jnp.where(pos == 0, c2, jnp.where(pos == 1, c1, pltpu.roll(u, 2, 0)))
    mix_ref[...] = _conv_epilogue(u, prev1, prev2, bg, z, cw_ref, cb_ref).astype(BF16)
    tail_ref[...] = u.reshape(nb, t, tn)[:, t - V7X_SUBLANES:, :]


def _conv_sample(h, w_in, conv_w, conv_b, cache_conv, *, d_conv, nb, t, tn):
    m, d = h.shape
    assert m == nb * t and d_conv % tn == 0 and t % V7X_SUBLANES == 0 and t >= CONV_WIDTH - 1
    nj = d_conv // tn
    wspec = lambda part: pl.BlockSpec((d, tn), lambda j, part=part: (0, part * nj + j))
    blocks = [((m, d), BF16)] + [((d, tn), BF16)] * 4 + [((m, tn), BF16)] + [((m, tn), F32)] * 10
    return pl.pallas_call(
        functools.partial(_conv_sample_kernel, nb=nb, t=t),
        grid=(nj,),
        in_specs=[pl.BlockSpec((m, d), lambda j: (0, 0)),
                  wspec(0), wspec(1), wspec(2), wspec(3),
                  pl.BlockSpec((CONV_WIDTH, tn), lambda j: (0, j)),
                  pl.BlockSpec((1, tn), lambda j: (0, j)),
                  pl.BlockSpec((nb, CONV_WIDTH - 1, tn), lambda j: (0, 0, j))],
        out_specs=[pl.BlockSpec((m, tn), lambda j: (0, j)),
                   pl.BlockSpec((nb, V7X_SUBLANES, tn), lambda j: (0, 0, j))],
        out_shape=[jax.ShapeDtypeStruct((m, d_conv), BF16),
                   jax.ShapeDtypeStruct((nb, V7X_SUBLANES, d_conv), F32)],
        compiler_params=_params(("arbitrary",), blocks),
        name="conv_sample",
    )(h, w_in, w_in, w_in, w_in, conv_w, conv_b, cache_conv)


def _proj_kernel(*refs, n_parts):
    h = refs[0][...]
    for p in range(n_parts):
        w_ref, o_ref = refs[1 + p], refs[1 + n_parts + p]
        o_ref[...] = _mm(h, w_ref[...]).astype(o_ref.dtype)


def _proj(h, w_in, *, col_starts, width, out_dtypes, tm, tn, name):
    m, d = h.shape
    n_parts = len(col_starts)
    assert m % tm == 0 and width % tn == 0 and all(c % tn == 0 for c in col_starts)
    wspec = lambda c0: pl.BlockSpec((d, tn), lambda i, j, c0=c0: (0, c0 // tn + j))
    blocks = ([((tm, d), BF16)] + [((d, tn), BF16)] * n_parts
              + [((tm, tn), dt) for dt in out_dtypes] + [((tm, tn), F32)] * n_parts)
    return pl.pallas_call(
        functools.partial(_proj_kernel, n_parts=n_parts),
        grid=(m // tm, width // tn),
        in_specs=[pl.BlockSpec((tm, d), lambda i, j: (i, 0))] + [wspec(c0) for c0 in col_starts],
        out_specs=[pl.BlockSpec((tm, tn), lambda i, j: (i, j)) for _ in out_dtypes],
        out_shape=[jax.ShapeDtypeStruct((m, width), dt) for dt in out_dtypes],
        compiler_params=_params(("arbitrary", "arbitrary"), blocks),
        name=name,
    )(h, *([w_in] * n_parts))


def _bias_seq(rel_bias, offset, n):
    rev = rel_bias[:, ::-1]
    left = offset - MAX_REL
    assert left >= 0
    right = max(n - left - rev.shape[1], 0)
    return jnp.pad(rev, ((0, 0), (left, right)), mode="edge")[:, :n]


def _bias_table_kernel(seq_ref, o_ref, *, rows, cols, lane0, band_cols):
    width = seq_ref.shape[-1]
    seq = jnp.broadcast_to(seq_ref[0], (rows, width))
    table = pltpu.roll(seq, width - lane0, 1, stride=1, stride_axis=0)[:, :cols] * LOG2E
    if band_cols is not None:
        qc = lax.broadcasted_iota(jnp.int32, (rows, cols), 0) // CHUNK
        kc = lax.broadcasted_iota(jnp.int32, (rows, cols), 1) // CHUNK - band_cols // CHUNK
        table = jnp.where((kc <= qc) & (kc >= qc - N_PAST_CHUNKS), table, -jnp.inf)
    o_ref[0] = table


def _bias_table(rel_bias, *, rows, cols, rel0, band_cols=None):
    nh = rel_bias.shape[0]
    lane0 = V7X_LANES * (-(-rows // V7X_LANES))
    width = 1 << (lane0 + cols - 1).bit_length()
    seq = _bias_seq(rel_bias, rel0 + lane0, width)
    blocks = [((1, width), F32), ((rows, cols), F32), ((rows, width), F32), ((rows, width), F32)]
    return pl.pallas_call(
        functools.partial(_bias_table_kernel, rows=rows, cols=cols, lane0=lane0, band_cols=band_cols),
        grid=(nh,),
        in_specs=[pl.BlockSpec((1, 1, width), lambda h: (h, 0, 0))],
        out_specs=pl.BlockSpec((1, rows, cols), lambda h: (h, 0, 0)),
        out_shape=jax.ShapeDtypeStruct((nh, rows, cols), F32),
        compiler_params=_params(("arbitrary",), blocks),
        name="bias_table",
    )(seq.reshape(nh, 1, width))


def _attn_prompt_kernel(q_ref, k0_ref, k1_ref, k2_ref, v0_ref, v1_ref, v2_ref, z_ref, bias_ref, o_ref,
                        *, heads, tq):
    b = pl.program_id(1)
    nk = 3 * tq
    dims = (((1,), (1,)), ((), ()))

    def run(mask_keys):
        if mask_keys:
            kc = lax.broadcasted_iota(jnp.int32, (1, nk), 1) // CHUNK
            valid = kc >= (2 - b) * (tq // CHUNK)
        for hh in range(heads):
            sl = slice(hh * HEAD_DIM, (hh + 1) * HEAD_DIM)
            k = jnp.concatenate([k0_ref[:, sl], k1_ref[:, sl], k2_ref[:, sl]], axis=0)
            v = jnp.concatenate([v0_ref[:, sl], v1_ref[:, sl], v2_ref[:, sl]], axis=0)
            s = lax.dot_general(q_ref[:, sl], k, dims, preferred_element_type=F32)
            s = s * (ATTN_SCALE * LOG2E) + bias_ref[hh]
            if mask_keys:
                s = jnp.where(valid, s, -jnp.inf)
            m = jnp.max(s, axis=-1, keepdims=True)
            p = jnp.exp2(s - m)
            l = jnp.sum(p, axis=-1, keepdims=True)
            o = _mm(p.astype(BF16), v) / l
            o_ref[:, sl] = (o * _silu(z_ref[:, sl])).astype(BF16)

    pl.when(b < 2)(lambda: run(True))
    pl.when(b >= 2)(lambda: run(False))


def _attn_prompt(q, k, v, z, bias, *, tq, heads):
    s, da = q.shape
    nh = da // HEAD_DIM
    wcols = heads * HEAD_DIM
    assert s % tq == 0 and nh % heads == 0 and tq % CHUNK == 0 and 2 * tq >= BAND_PAST
    kspec = lambda back: pl.BlockSpec((tq, wcols), lambda g, b, back=back: (jnp.maximum(b - back, 0), g))
    blocks = ([((tq, wcols), BF16)] * 8 + [((tq, wcols), F32)] + [((heads, tq, 3 * tq), F32)]
              + [((tq, 3 * tq), F32)] * 8)
    return pl.pallas_call(
        functools.partial(_attn_prompt_kernel, heads=heads, tq=tq),
        grid=(nh // heads, s // tq),
        in_specs=[pl.BlockSpec((tq, wcols), lambda g, b: (b, g)),
                  kspec(2), kspec(1), kspec(0), kspec(2), kspec(1), kspec(0),
                  pl.BlockSpec((tq, wcols), lambda g, b: (b, g)),
                  pl.BlockSpec((heads, tq, 3 * tq), lambda g, b: (g, 0, 0))],
        out_specs=pl.BlockSpec((tq, wcols), lambda g, b: (b, g)),
        out_shape=jax.ShapeDtypeStruct((s, da), BF16),
        compiler_params=_params(("arbitrary", "arbitrary"), blocks),
        name="attn_prompt",
    )(q, k, k, k, v, v, v, z, bias)


def _attn_sample_kernel(q_ref, kn_ref, vn_ref, ck_ref, cv_ref, z_ref, bias_ref, o_ref, *, nh, r, t):
    dims = (((1,), (1,)), ((), ()))
    for hh in range(nh):
        sl = slice(hh * HEAD_DIM, (hh + 1) * HEAD_DIM)
        head_rows = pl.ds(hh, r, stride=nh)
        q = q_ref[0, :, sl].astype(BF16)
        sc = lax.dot_general(q, ck_ref[0, head_rows, :].astype(BF16), dims, preferred_element_type=F32)
        sn = lax.dot_general(q, kn_ref[0, :, sl].astype(BF16), dims, preferred_element_type=F32)
        sc = sc * (ATTN_SCALE * LOG2E) + bias_ref[hh, :, 0:r]
        sn = sn * (ATTN_SCALE * LOG2E) + bias_ref[hh, :, r:r + t]
        m = jnp.maximum(jnp.max(sc, axis=-1, keepdims=True), jnp.max(sn, axis=-1, keepdims=True))
        pc = jnp.exp2(sc - m)
        pn = jnp.exp2(sn - m)
        l = jnp.sum(pc, axis=-1, keepdims=True) + jnp.sum(pn, axis=-1, keepdims=True)
        o = _mm(pc.astype(BF16), cv_ref[0, head_rows, :].astype(BF16))
        o = o + _mm(pn.astype(BF16), vn_ref[0, :, sl].astype(BF16))
        o_ref[0, :, sl] = ((o / l) * _silu(z_ref[0, :, sl])).astype(BF16)


def _attn_sample(q, kn, vn, cache_k, cache_v, z, bias):
    nb, t, da = q.shape
    nh = da // HEAD_DIM
    r = cache_k.shape[1] // nh
    assert r % V7X_LANES == 0 and bias.shape[2] >= r + t
    new = pl.BlockSpec((1, t, da), lambda b: (b, 0, 0))
    old = pl.BlockSpec((1, r * nh, HEAD_DIM), lambda b: (b, 0, 0))
    blocks = [((t, da), F32)] * 5 + [((r, da), F32)] * 2 + [(bias.shape, F32)] + [((r, da), BF16)] * 2
    return pl.pallas_call(
        functools.partial(_attn_sample_kernel, nh=nh, r=r, t=t),
        grid=(nb,),
        in_specs=[new, new, new, old, old, new, pl.BlockSpec(bias.shape, lambda b: (0, 0, 0))],
        out_specs=new,
        out_shape=jax.ShapeDtypeStruct((nb, t, da), BF16),
        compiler_params=_params(("arbitrary",), blocks),
        name="attn_sample",
    )(q, kn, vn, cache_k, cache_v, z, bias)


def _out_kernel(x_ref, ma_ref, mb_ref, wa_ref, wb_ref, gate_ref, gf_ref, y_ref, ss_ref, *, nj, tn):
    j = pl.program_id(1)
    acc = _mm(ma_ref[...], wa_ref[...]) + _mm(mb_ref[...], wb_ref[...])
    nb, t, _ = x_ref.shape
    res = x_ref[...] + gate_ref[...] * acc.reshape(nb, t, tn)
    y_ref[:, :, pl.ds(pl.multiple_of(j * tn, tn), tn)] = res
    part = jnp.sum(res * res, axis=-1, keepdims=True)

    @pl.when(j == 0)
    def _():
        ss_ref[...] = part

    @pl.when(j > 0)
    def _():
        ss_ref[...] += part

    @pl.when(j == nj - 1)
    def _():
        inv = lax.rsqrt(ss_ref[...] * (1.0 / (nj * tn)) + NORM_EPS)
        for n in range(nj):
            cols = slice(n * tn, (n + 1) * tn)
            y_ref[:, :, cols] = (y_ref[:, :, cols] * inv) * gf_ref[:, :, cols]


def _out(x, mix_a, mix_b, w_out, gate, g_final, *, nb, t, tn):
    b, s, d = x.shape
    half = mix_a.shape[1]
    tm = nb * t
    assert b % nb == 0 and s % t == 0 and d % tn == 0 and w_out.shape[0] == 2 * half
    nj = d // tn
    nt = s // t
    assert nb == 1 or nt == 1, "a row tile must be contiguous in the flattened (B*S) mixed rows"
    blocks = ([((nb, t, tn), F32)] + [((tm, half), BF16)] * 2 + [((half, tn), BF16)] * 2
              + [((nb, t, d), F32)] + [((tm, tn), F32)] * 4)
    return pl.pallas_call(
        functools.partial(_out_kernel, nj=nj, tn=tn),
        grid=((b // nb) * nt, nj),
        in_specs=[pl.BlockSpec((nb, t, tn), lambda i, j: (i // nt, i % nt, j)),
                  pl.BlockSpec((tm, half), lambda i, j: (i, 0)),
                  pl.BlockSpec((tm, half), lambda i, j: (i, 0)),
                  pl.BlockSpec((half, tn), lambda i, j: (0, j)),
                  pl.BlockSpec((half, tn), lambda i, j: (1, j)),
                  pl.BlockSpec((nb, 1, tn), lambda i, j: (i // nt, 0, j)),
                  pl.BlockSpec((1, 1, d), lambda i, j: (0, 0, 0))],
        out_specs=pl.BlockSpec((nb, t, d), lambda i, j: (i // nt, i % nt, 0)),
        out_shape=jax.ShapeDtypeStruct((b, s, d), F32),
        scratch_shapes=[pltpu.VMEM((nb, t, 1), F32)],
        compiler_params=_params(("arbitrary", "arbitrary"), blocks),
        name="out_proj",
    )(x, mix_a, mix_b, w_out, w_out, gate, g_final)


def kernel(x_prompt, x_sample, cache_k, cache_v, cache_conv, c_prompt, c_sample,
           g_norm, w_ada, b_ada, w_in, conv_w, conv_b, rel_bias, w_out, g_final):
    depth = g_norm.shape[0]
    assert depth == 1, "single-layer trunk"
    bp, sp, d = x_prompt.shape
    bs, ts, _ = x_sample.shape
    assert bp == 1
    d_conv = conv_w.shape[-1]
    d_attn = w_out.shape[1] - d_conv
    nh = d_attn // HEAD_DIM
    r = cache_k.shape[2]
    rows_kept = min(BAND_PAST, sp)

    n_c = bp + bs
    pad = (-n_c) % V7X_SUBLANES
    c_all = jnp.concatenate([c_prompt, c_sample, jnp.zeros((pad, d), F32)], axis=0)
    mod = _ada(c_all, w_ada[0], b_ada)
    shift, scale, gate = (mod[:n_c, i * d:(i + 1) * d].reshape(n_c, 1, d) for i in range(3))

    w_in_b = w_in[0].astype(BF16)
    w_out_b = w_out[0].astype(BF16)
    g3 = g_norm.reshape(1, 1, d)
    gf3 = g_final.reshape(1, 1, d)
    attn0 = 4 * d_conv

    tq = 256
    hp = _prep(x_prompt, g3, scale[:bp], shift[:bp], nb=1, t=512).reshape(sp, d)
    mix_conv_p, u_tail_p, qp, kp, vp, zp = _inproj_prompt(
        hp, w_in_b, conv_w[0], conv_b, d_conv=d_conv, d_attn=d_attn, tm=1024, tn=256, halves=2)
    k_keep, v_keep = _proj(hp[sp - rows_kept:], w_in_b, col_starts=[attn0 + d_attn, attn0 + 2 * d_attn],
                           width=d_attn, out_dtypes=[F32, F32], tm=rows_kept, tn=256, name="kv_keep_prompt")
    bias_p = _bias_table(rel_bias[0], rows=tq, cols=3 * tq, rel0=2 * tq, band_cols=2 * tq)
    mix_attn_p = _attn_prompt(qp, kp, vp, zp, bias_p, tq=tq, heads=8)
    y_prompt = _out(x_prompt, mix_conv_p, mix_attn_p, w_out_b, gate[:bp], gf3, nb=1, t=512, tn=1024)

    hs = _prep(x_sample, g3, scale[bp:], shift[bp:], nb=8, t=ts).reshape(bs * ts, d)
    mix_conv_s, u_tail_s = _conv_sample(hs, w_in_b, conv_w[0], conv_b, cache_conv[0],
                                        d_conv=d_conv, nb=bs, t=ts, tn=256)
    qs, ks, vs, zs = _proj(hs, w_in_b, col_starts=[attn0 + i * d_attn for i in range(4)], width=d_attn,
                           out_dtypes=[F32, F32, F32, F32], tm=bs * ts, tn=256, name="qkvz_sample")
    bias_s = _bias_table(rel_bias[0], rows=ts, cols=r + V7X_LANES * (-(-ts // V7X_LANES)), rel0=r)
    to3 = lambda a: a.reshape(bs, ts, d_attn)
    mix_attn_s = _attn_sample(to3(qs), to3(ks), to3(vs), cache_k[0].reshape(bs, r * nh, HEAD_DIM),
                              cache_v[0].reshape(bs, r * nh, HEAD_DIM), to3(zs), bias_s)
    y_sample = _out(x_sample, mix_conv_s, mix_attn_s.reshape(bs * ts, d_attn), w_out_b, gate[bp:], gf3,
                    nb=16, t=ts, tn=1024)

    keep = CONV_WIDTH - 1
    new_k_prompt = k_keep.reshape(1, bp, rows_kept, nh, HEAD_DIM)
    new_v_prompt = v_keep.reshape(1, bp, rows_kept, nh, HEAD_DIM)
    new_conv_prompt = u_tail_p[-1, V7X_SUBLANES - keep:, :].reshape(1, bp, keep, d_conv)
    new_k_sample = ks.reshape(1, bs, ts, nh, HEAD_DIM)
    new_v_sample = vs.reshape(1, bs, ts, nh, HEAD_DIM)
    new_conv_sample = u_tail_s[:, V7X_SUBLANES - keep:, :].reshape(1, bs, keep, d_conv)
    return (y_prompt, y_sample, new_k_prompt, new_v_prompt, new_conv_prompt,
            new_k_sample, new_v_sample, new_conv_sample)
```

```python
import functools
import math

import jax
import jax.numpy as jnp
from jax import lax
from jax.experimental import pallas as pl
from jax.experimental.pallas import tpu as pltpu

CHUNK = 64
N_PAST_CHUNKS = 8
BAND_PAST = N_PAST_CHUNKS * CHUNK
HEAD_DIM = 128
CONV_WIDTH = 3
MAX_REL = 256
NORM_EPS = 1e-6
ATTN_SCALE = HEAD_DIM ** -0.5
LOG2E = math.log2(math.e)
N_PARTS = 4

V7X_LANES = 128
V7X_SUBLANES = 8
V7X_VMEM_BYTES = 64 * 1024 * 1024

F32 = jnp.float32
BF16 = jnp.bfloat16


def _nbytes(shape, dtype):
    n = 1
    for s in shape:
        n *= s
    return n * jnp.dtype(dtype).itemsize


def _params(semantics, blocks, extra_bytes=0):
    need = 2 * sum(_nbytes(s, d) for s, d in blocks) + extra_bytes + 16 * 1024 * 1024
    limit = int(min(need, V7X_VMEM_BYTES - 4 * 1024 * 1024))
    return pltpu.CompilerParams(dimension_semantics=semantics, vmem_limit_bytes=limit)


def _silu(z):
    return z * jax.nn.sigmoid(z)


def _mm(a, b):
    return jnp.dot(a, b, preferred_element_type=F32)


def _ada_kernel(c_ref, w_ref, b_ref, o_ref):
    o_ref[...] = _mm(c_ref[...].astype(BF16), w_ref[...].astype(BF16)) + b_ref[...]


def _ada(c, w_ada, b_ada, *, tn=512):
    r, d = c.shape
    n = w_ada.shape[1]
    assert n % tn == 0 and r % V7X_SUBLANES == 0
    blocks = [((r, d), F32), ((d, tn), F32), ((1, tn), F32), ((r, tn), F32), ((d, tn), BF16)]
    return pl.pallas_call(
        _ada_kernel,
        grid=(n // tn,),
        in_specs=[pl.BlockSpec((r, d), lambda j: (0, 0)),
                  pl.BlockSpec((d, tn), lambda j: (0, j)),
                  pl.BlockSpec((1, tn), lambda j: (0, j))],
        out_specs=pl.BlockSpec((r, tn), lambda j: (0, j)),
        out_shape=jax.ShapeDtypeStruct((r, n), F32),
        compiler_params=_params(("arbitrary",), blocks),
        name="ada",
    )(c, w_ada, b_ada)


def _modulated_norm(x, g, scale, shift):
    ms = jnp.mean(x * x, axis=-1, keepdims=True)
    xn = x * lax.rsqrt(ms + NORM_EPS)
    return (xn * g * (1.0 + scale) + shift).astype(BF16)


def _prep_kernel(x_ref, g_ref, scale_ref, shift_ref, h_ref):
    h_ref[...] = _modulated_norm(x_ref[...], g_ref[...], scale_ref[...], shift_ref[...])


def _prep(x, g, scale, shift, *, nb, t, s0=0, s_len=None):
    b, s, d = x.shape
    s_len = s if s_len is None else s_len
    assert b % nb == 0 and s_len % t == 0 and s0 % t == 0 and s0 + s_len <= s
    j0 = s0 // t
    blocks = [((nb, t, d), F32), ((nb, t, d), BF16), ((nb, t, d), F32)]
    return pl.pallas_call(
        _prep_kernel,
        grid=(b // nb, s_len // t),
        in_specs=[pl.BlockSpec((nb, t, d), lambda i, j: (i, j + j0, 0)),
                  pl.BlockSpec((1, 1, d), lambda i, j: (0, 0, 0)),
                  pl.BlockSpec((nb, 1, d), lambda i, j: (i, 0, 0)),
                  pl.BlockSpec((nb, 1, d), lambda i, j: (i, 0, 0))],
        out_specs=pl.BlockSpec((nb, t, d), lambda i, j: (i, j, 0)),
        out_shape=jax.ShapeDtypeStruct((b, s_len, d), BF16),
        compiler_params=_params(("arbitrary", "arbitrary"), blocks),
        name="prep",
    )(x, g, scale, shift)


def _blocked_weight_kernel(*refs, tn):
    o_ref = refs[-1]
    for p, w_ref in enumerate(refs[:-1]):
        o_ref[:, p * tn:(p + 1) * tn] = w_ref[...].astype(BF16)


def _blocked_weight(w_in, d_conv, d_attn, tn, *, tr=2048):
    d, n = w_in.shape
    nbc, nba = d_conv // tn, d_attn // tn
    assert d % tr == 0 and n == N_PARTS * (d_conv + d_attn) and d_conv % tn == 0 and d_attn % tn == 0

    def wspec(p):
        src = lambda j: jnp.where(j < nbc, p * nbc + j, N_PARTS * nbc + p * nba + j - nbc)
        return pl.BlockSpec((tr, tn), lambda i, j: (i, src(j)))

    blocks = [((tr, tn), F32)] * N_PARTS + [((tr, N_PARTS * tn), BF16)]
    return pl.pallas_call(
        functools.partial(_blocked_weight_kernel, tn=tn),
        grid=(d // tr, nbc + nba),
        in_specs=[wspec(p) for p in range(N_PARTS)],
        out_specs=pl.BlockSpec((tr, N_PARTS * tn), lambda i, j: (i, j)),
        out_shape=jax.ShapeDtypeStruct((d, n), BF16),
        compiler_params=_params(("arbitrary", "arbitrary"), blocks),
        name="blocked_weight",
    )(*([w_in] * N_PARTS))


def _conv_epilogue(u, prev1, prev2, bg, z, cw_ref, cb_ref):
    conv = cb_ref[...] + cw_ref[0:1, :] * prev2
    conv = conv + cw_ref[1:2, :] * prev1
    conv = conv + cw_ref[2:3, :] * u
    return (bg * conv) * _silu(z)


def _inproj_prompt_kernel(x_ref, g_ref, scale_ref, shift_ref, h0_ref, w_ref, cw_ref, cb_ref,
                          mix_ref, tail_ref, q_ref, k_ref, v_ref, z_ref, h_buf, halo_ref, *, nbc, halves, tn):
    i = pl.program_id(0)
    j = pl.program_id(1)
    slot = lax.rem(i, 2)
    sm = x_ref.shape[0]
    hm = h_buf.shape[1] // halves

    @pl.when((i == 0) & (j == 0))
    def _():
        pltpu.sync_copy(h0_ref, h_buf.at[0])

    def prep_next_slab():
        h = _modulated_norm(x_ref[...], g_ref[...], scale_ref[...], shift_ref[...])
        h_buf[1 - slot, pl.ds(pl.multiple_of(j * sm, sm), sm), :] = h

    def part(h, p):
        return _mm(h, w_ref[:, p * tn:(p + 1) * tn])

    def sub_tiles():
        for s in range(halves):
            if s == halves - 1:
                prep_next_slab()
            yield slice(s * hm, (s + 1) * hm)

    @pl.when(j < nbc)
    def _conv_group():
        @pl.when(i == 0)
        def _():
            halo_ref[j] = jnp.zeros(halo_ref.shape[1:], F32)

        last8 = halo_ref[j]
        for rows in sub_tiles():
            h = h_buf[slot, rows, :]
            xin, bg, cg, z = part(h, 0), part(h, 1), part(h, 2), part(h, 3)
            u = cg * xin
            last1 = last8[V7X_SUBLANES - 1:V7X_SUBLANES, :]
            last2 = last8[V7X_SUBLANES - 2:V7X_SUBLANES - 1, :]
            row = lax.broadcasted_iota(jnp.int32, u.shape, 0)
            prev1 = jnp.where(row == 0, last1, pltpu.roll(u, 1, 0))
            prev2 = jnp.where(row == 0, last2, jnp.where(row == 1, last1, pltpu.roll(u, 2, 0)))
            mix_ref[rows, :] = _conv_epilogue(u, prev1, prev2, bg, z, cw_ref, cb_ref).astype(BF16)
            last8 = u[hm - V7X_SUBLANES:, :]
        halo_ref[j] = last8
        tail_ref[0] = last8

    @pl.when(j >= nbc)
    def _attn_group():
        for rows in sub_tiles():
            h = h_buf[slot, rows, :]
            q_ref[rows, :] = part(h, 0).astype(BF16)
            k_ref[rows, :] = part(h, 1).astype(BF16)
            v_ref[rows, :] = part(h, 2).astype(BF16)
            z_ref[rows, :] = part(h, 3)


def _inproj_prompt(x, g, scale, shift, h0, w_blk, conv_w, conv_b, *, d_conv, d_attn, tm, tn, halves):
    s, d = x.shape
    assert s % tm == 0 and d_conv % tn == 0 and d_attn % tn == 0 and (tm // halves) % V7X_SUBLANES == 0
    nbc, nba = d_conv // tn, d_attn // tn
    nsteps = nbc + nba
    nrt = s // tm
    sm = tm // nsteps
    assert tm % nsteps == 0 and sm % (2 * V7X_SUBLANES) == 0 and h0.shape == (tm, d)
    cj = lambda j: jnp.minimum(j, nbc - 1)
    aj = lambda j: jnp.maximum(j - nbc, 0)
    attn_spec = pl.BlockSpec((tm, tn), lambda i, j: (i, aj(j)))
    row_vec = pl.BlockSpec((1, d), lambda i, j: (0, 0))
    blocks = ([((sm, d), F32), ((d, N_PARTS * tn), BF16)] + [((tm, tn), BF16)] * 4 + [((tm, tn), F32)]
              + [((tm // halves, tn), F32)] * 8)
    return pl.pallas_call(
        functools.partial(_inproj_prompt_kernel, nbc=nbc, halves=halves, tn=tn),
        grid=(nrt, nsteps),
        in_specs=[pl.BlockSpec((sm, d), lambda i, j: (jnp.minimum(i + 1, nrt - 1) * nsteps + j, 0)),
                  row_vec, row_vec, row_vec,
                  pl.BlockSpec(memory_space=pl.ANY),
                  pl.BlockSpec((d, N_PARTS * tn), lambda i, j: (0, j)),
                  pl.BlockSpec((CONV_WIDTH, tn), lambda i, j: (0, cj(j))),
                  pl.BlockSpec((1, tn), lambda i, j: (0, cj(j)))],
        out_specs=[pl.BlockSpec((tm, tn), lambda i, j: (i, cj(j))),
                   pl.BlockSpec((1, V7X_SUBLANES, tn), lambda i, j: (i, 0, cj(j))),
                   attn_spec, attn_spec, attn_spec, attn_spec],
        out_shape=[jax.ShapeDtypeStruct((s, d_conv), BF16),
                   jax.ShapeDtypeStruct((nrt, V7X_SUBLANES, d_conv), F32),
                   jax.ShapeDtypeStruct((s, d_attn), BF16),
                   jax.ShapeDtypeStruct((s, d_attn), BF16),
                   jax.ShapeDtypeStruct((s, d_attn), BF16),
                   jax.ShapeDtypeStruct((s, d_attn), F32)],
        scratch_shapes=[pltpu.VMEM((2, tm, d), BF16), pltpu.VMEM((nbc, V7X_SUBLANES, tn), F32)],
        compiler_params=_params(("arbitrary", "arbitrary"), blocks, extra_bytes=_nbytes((2, tm, d), BF16)),
        name="inproj_prompt",
    )(x, g, scale, shift, h0, w_blk, conv_w, conv_b)


def _conv_sample_kernel(h_ref, w_ref, cw_ref, cb_ref, cache_ref, mix_ref, tail_ref, *, nb, t, tn):
    h = h_ref[...]
    xin, bg, cg, z = (_mm(h, w_ref[:, p * tn:(p + 1) * tn]) for p in range(N_PARTS))
    u = cg * xin
    cache = cache_ref[...]
    c2 = jnp.broadcast_to(cache[:, 0:1, :], (nb, t, tn)).reshape(nb * t, tn)
    c1 = jnp.broadcast_to(cache[:, 1:2, :], (nb, t, tn)).reshape(nb * t, tn)
    pos = lax.broadcasted_iota(jnp.int32, (nb, t, tn), 1).reshape(nb * t, tn)
    prev1 = jnp.where(pos == 0, c1, pltpu.roll(u, 1, 0))
    prev2 = jnp.where(pos == 0, c2, jnp.where(pos == 1, c1, pltpu.roll(u, 2, 0)))
    mix_ref[...] = _conv_epilogue(u, prev1, prev2, bg, z, cw_ref, cb_ref).astype(BF16)
    tail_ref[...] = u.reshape(nb, t, tn)[:, t - V7X_SUBLANES:, :]


def _conv_sample(h, w_blk, conv_w, conv_b, cache_conv, *, d_conv, nb, t, tn):
    m, d = h.shape
    assert m == nb * t and d_conv % tn == 0 and t % V7X_SUBLANES == 0 and t >= CONV_WIDTH - 1
    blocks = [((m, d), BF16), ((d, N_PARTS * tn), BF16), ((m, tn), BF16)] + [((m, tn), F32)] * 10
    return pl.pallas_call(
        functools.partial(_conv_sample_kernel, nb=nb, t=t, tn=tn),
        grid=(d_conv // tn,),
        in_specs=[pl.BlockSpec((m, d), lambda j: (0, 0)),
                  pl.BlockSpec((d, N_PARTS * tn), lambda j: (0, j)),
                  pl.BlockSpec((CONV_WIDTH, tn), lambda j: (0, j)),
                  pl.BlockSpec((1, tn), lambda j: (0, j)),
                  pl.BlockSpec((nb, CONV_WIDTH - 1, tn), lambda j: (0, 0, j))],
        out_specs=[pl.BlockSpec((m, tn), lambda j: (0, j)),
                   pl.BlockSpec((nb, V7X_SUBLANES, tn), lambda j: (0, 0, j))],
        out_shape=[jax.ShapeDtypeStruct((m, d_conv), BF16),
                   jax.ShapeDtypeStruct((nb, V7X_SUBLANES, d_conv), F32)],
        compiler_params=_params(("arbitrary",), blocks),
        name="conv_sample",
    )(h, w_blk, conv_w, conv_b, cache_conv)


def _proj_kernel(h_ref, w_ref, *o_refs, parts, tn):
    h = h_ref[...]
    for p, o_ref in zip(parts, o_refs):
        o_ref[...] = _mm(h, w_ref[:, p * tn:(p + 1) * tn]).astype(o_ref.dtype)


def _proj(h, w_blk, *, block0, n_blocks, parts, out_dtypes, tm, tn, name):
    m, d = h.shape
    assert m % tm == 0 and len(parts) == len(out_dtypes)
    blocks = ([((tm, d), BF16), ((d, N_PARTS * tn), BF16)]
              + [((tm, tn), dt) for dt in out_dtypes] + [((tm, tn), F32)] * len(parts))
    return pl.pallas_call(
        functools.partial(_proj_kernel, parts=parts, tn=tn),
        grid=(m // tm, n_blocks),
        in_specs=[pl.BlockSpec((tm, d), lambda i, j: (i, 0)),
                  pl.BlockSpec((d, N_PARTS * tn), lambda i, j: (0, block0 + j))],
        out_specs=[pl.BlockSpec((tm, tn), lambda i, j: (i, j)) for _ in out_dtypes],
        out_shape=[jax.ShapeDtypeStruct((m, n_blocks * tn), dt) for dt in out_dtypes],
        compiler_params=_params(("arbitrary", "arbitrary"), blocks),
        name=name,
    )(h, w_blk)


def _bias_seq(rel_bias, offset, n):
    rev = rel_bias[:, ::-1]
    left = offset - MAX_REL
    assert left >= 0
    right = max(n - left - rev.shape[1], 0)
    return jnp.pad(rev, ((0, 0), (left, right)), mode="edge")[:, :n]


def _bias_table_kernel(seq_ref, o_ref, *, rows, cols, lane0, band_cols):
    width = seq_ref.shape[-1]
    seq = jnp.broadcast_to(seq_ref[0], (rows, width))
    table = pltpu.roll(seq, width - lane0, 1, stride=1, stride_axis=0)[:, :cols] * LOG2E
    if band_cols is not None:
        qc = lax.broadcasted_iota(jnp.int32, (rows, cols), 0) // CHUNK
        kc = lax.broadcasted_iota(jnp.int32, (rows, cols), 1) // CHUNK - band_cols // CHUNK
        table = jnp.where((kc <= qc) & (kc >= qc - N_PAST_CHUNKS), table, -jnp.inf)
    o_ref[0] = table


def _bias_table(rel_bias, *, rows, cols, rel0, band_cols=None):
    nh = rel_bias.shape[0]
    lane0 = V7X_LANES * (-(-rows // V7X_LANES))
    width = 1 << (lane0 + cols - 1).bit_length()
    seq = _bias_seq(rel_bias, rel0 + lane0, width)
    blocks = [((1, width), F32), ((rows, cols), F32), ((rows, width), F32), ((rows, width), F32)]
    return pl.pallas_call(
        functools.partial(_bias_table_kernel, rows=rows, cols=cols, lane0=lane0, band_cols=band_cols),
        grid=(nh,),
        in_specs=[pl.BlockSpec((1, 1, width), lambda h: (h, 0, 0))],
        out_specs=pl.BlockSpec((1, rows, cols), lambda h: (h, 0, 0)),
        out_shape=jax.ShapeDtypeStruct((nh, rows, cols), F32),
        compiler_params=_params(("arbitrary",), blocks),
        name="bias_table",
    )(seq.reshape(nh, 1, width))


def _attn_prompt_kernel(q_ref, k0_ref, k1_ref, k2_ref, v0_ref, v1_ref, v2_ref, z_ref, bias_ref, o_ref,
                        *, heads, tq):
    b = pl.program_id(1)
    nk = 3 * tq
    dims = (((1,), (1,)), ((), ()))

    def run(mask_keys):
        if mask_keys:
            kc = lax.broadcasted_iota(jnp.int32, (1, nk), 1) // CHUNK
            valid = kc >= (2 - b) * (tq // CHUNK)
        for hh in range(heads):
            sl = slice(hh * HEAD_DIM, (hh + 1) * HEAD_DIM)
            k = jnp.concatenate([k0_ref[:, sl], k1_ref[:, sl], k2_ref[:, sl]], axis=0)
            v = jnp.concatenate([v0_ref[:, sl], v1_ref[:, sl], v2_ref[:, sl]], axis=0)
            s = lax.dot_general(q_ref[:, sl], k, dims, preferred_element_type=F32)
            s = s * (ATTN_SCALE * LOG2E) + bias_ref[hh]
            if mask_keys:
                s = jnp.where(valid, s, -jnp.inf)
            m = jnp.max(s, axis=-1, keepdims=True)
            p = jnp.exp2(s - m)
            l = jnp.sum(p, axis=-1, keepdims=True)
            o = _mm(p.astype(BF16), v) / l
            o_ref[:, sl] = (o * _silu(z_ref[:, sl])).astype(BF16)

    pl.when(b < 2)(lambda: run(True))
    pl.when(b >= 2)(lambda: run(False))


def _attn_prompt(q, k, v, z, bias, *, tq, heads):
    s, da = q.shape
    nh = da // HEAD_DIM
    wcols = heads * HEAD_DIM
    assert s % tq == 0 and nh % heads == 0 and tq % CHUNK == 0 and 2 * tq >= BAND_PAST
    kspec = lambda back: pl.BlockSpec((tq, wcols), lambda g, b, back=back: (jnp.maximum(b - back, 0), g))
    blocks = ([((tq, wcols), BF16)] * 8 + [((tq, wcols), F32)] + [((heads, tq, 3 * tq), F32)]
              + [((tq, 3 * tq), F32)] * 8)
    return pl.pallas_call(
        functools.partial(_attn_prompt_kernel, heads=heads, tq=tq),
        grid=(nh // heads, s // tq),
        in_specs=[pl.BlockSpec((tq, wcols), lambda g, b: (b, g)),
                  kspec(2), kspec(1), kspec(0), kspec(2), kspec(1), kspec(0),
                  pl.BlockSpec((tq, wcols), lambda g, b: (b, g)),
                  pl.BlockSpec((heads, tq, 3 * tq), lambda g, b: (g, 0, 0))],
        out_specs=pl.BlockSpec((tq, wcols), lambda g, b: (b, g)),
        out_shape=jax.ShapeDtypeStruct((s, da), BF16),
        compiler_params=_params(("arbitrary", "arbitrary"), blocks),
        name="attn_prompt",
    )(q, k, k, k, v, v, v, z, bias)


def _attn_sample_kernel(q_ref, kn_ref, vn_ref, ck_ref, cv_ref, z_ref, bias_ref, o_ref, *, nh, r, t):
    dims = (((1,), (1,)), ((), ()))
    for hh in range(nh):
        sl = slice(hh * HEAD_DIM, (hh + 1) * HEAD_DIM)
        head_rows = pl.ds(hh, r, stride=nh)
        q = q_ref[0, :, sl].astype(BF16)
        sc = lax.dot_general(q, ck_ref[0, head_rows, :].astype(BF16), dims, preferred_element_type=F32)
        sn = lax.dot_general(q, kn_ref[0, :, sl].astype(BF16), dims, preferred_element_type=F32)
        sc = sc * (ATTN_SCALE * LOG2E) + bias_ref[hh, :, 0:r]
        sn = sn * (ATTN_SCALE * LOG2E) + bias_ref[hh, :, r:r + t]
        m = jnp.maximum(jnp.max(sc, axis=-1, keepdims=True), jnp.max(sn, axis=-1, keepdims=True))
        pc = jnp.exp2(sc - m)
        pn = jnp.exp2(sn - m)
        l = jnp.sum(pc, axis=-1, keepdims=True) + jnp.sum(pn, axis=-1, keepdims=True)
        o = _mm(pc.astype(BF16), cv_ref[0, head_rows, :].astype(BF16))
        o = o + _mm(pn.astype(BF16), vn_ref[0, :, sl].astype(BF16))
        o_ref[0, :, sl] = ((o / l) * _silu(z_ref[0, :, sl])).astype(BF16)


def _attn_sample(q, kn, vn, cache_k, cache_v, z, bias):
    nb, t, da = q.shape
    nh = da // HEAD_DIM
    r = cache_k.shape[1] // nh
    assert r % V7X_LANES == 0 and bias.shape[2] >= r + t
    new = pl.BlockSpec((1, t, da), lambda b: (b, 0, 0))
    old = pl.BlockSpec((1, r * nh, HEAD_DIM), lambda b: (b, 0, 0))
    blocks = [((t, da), F32)] * 5 + [((r, da), F32)] * 2 + [(bias.shape, F32)] + [((r, da), BF16)] * 2
    return pl.pallas_call(
        functools.partial(_attn_sample_kernel, nh=nh, r=r, t=t),
        grid=(nb,),
        in_specs=[new, new, new, old, old, new, pl.BlockSpec(bias.shape, lambda b: (0, 0, 0))],
        out_specs=new,
        out_shape=jax.ShapeDtypeStruct((nb, t, da), BF16),
        compiler_params=_params(("arbitrary",), blocks),
        name="attn_sample",
    )(q, kn, vn, cache_k, cache_v, z, bias)


def _out_kernel(x_ref, ma_ref, mb_ref, wa_ref, wb_ref, gate_ref, gf_ref, y_ref, ss_ref, *, nj, tn):
    j = pl.program_id(1)
    acc = _mm(ma_ref[...], wa_ref[...]) + _mm(mb_ref[...], wb_ref[...])
    nb, t, _ = x_ref.shape
    res = x_ref[...] + gate_ref[...] * acc.reshape(nb, t, tn)
    y_ref[:, :, pl.ds(pl.multiple_of(j * tn, tn), tn)] = res
    part = jnp.sum(res * res, axis=-1, keepdims=True)

    @pl.when(j == 0)
    def _():
        ss_ref[...] = part

    @pl.when(j > 0)
    def _():
        ss_ref[...] += part

    @pl.when(j == nj - 1)
    def _():
        inv = lax.rsqrt(ss_ref[...] * (1.0 / (nj * tn)) + NORM_EPS)
        for n in range(nj):
            cols = slice(n * tn, (n + 1) * tn)
            y_ref[:, :, cols] = (y_ref[:, :, cols] * inv) * gf_ref[:, :, cols]


def _out(x, mix_a, mix_b, w_out, gate, g_final, *, nb, t, tn):
    b, s, d = x.shape
    half = mix_a.shape[1]
    tm = nb * t
    assert b % nb == 0 and s % t == 0 and d % tn == 0 and w_out.shape[0] == 2 * half
    nj = d // tn
    nt = s // t
    assert nb == 1 or nt == 1, "a row tile must be contiguous in the flattened (B*S) mixed rows"
    blocks = ([((nb, t, tn), F32)] + [((tm, half), BF16)] * 2 + [((half, tn), BF16)] * 2
              + [((nb, t, d), F32)] + [((tm, tn), F32)] * 4)
    return pl.pallas_call(
        functools.partial(_out_kernel, nj=nj, tn=tn),
        grid=((b // nb) * nt, nj),
        in_specs=[pl.BlockSpec((nb, t, tn), lambda i, j: (i // nt, i % nt, j)),
                  pl.BlockSpec((tm, half), lambda i, j: (i, 0)),
                  pl.BlockSpec((tm, half), lambda i, j: (i, 0)),
                  pl.BlockSpec((half, tn), lambda i, j: (0, j)),
                  pl.BlockSpec((half, tn), lambda i, j: (1, j)),
                  pl.BlockSpec((nb, 1, tn), lambda i, j: (i // nt, 0, j)),
                  pl.BlockSpec((1, 1, d), lambda i, j: (0, 0, 0))],
        out_specs=pl.BlockSpec((nb, t, d), lambda i, j: (i // nt, i % nt, 0)),
        out_shape=jax.ShapeDtypeStruct((b, s, d), F32),
        scratch_shapes=[pltpu.VMEM((nb, t, 1), F32)],
        compiler_params=_params(("arbitrary", "arbitrary"), blocks),
        name="out_proj",
    )(x, mix_a, mix_b, w_out, w_out, gate, g_final)


def kernel(x_prompt, x_sample, cache_k, cache_v, cache_conv, c_prompt, c_sample,
           g_norm, w_ada, b_ada, w_in, conv_w, conv_b, rel_bias, w_out, g_final):
    depth = g_norm.shape[0]
    assert depth == 1, "single-layer trunk"
    bp, sp, d = x_prompt.shape
    bs, ts, _ = x_sample.shape
    assert bp == 1
    d_conv = conv_w.shape[-1]
    d_attn = w_out.shape[1] - d_conv
    nh = d_attn // HEAD_DIM
    r = cache_k.shape[2]
    rows_kept = min(BAND_PAST, sp)

    n_c = bp + bs
    pad = (-n_c) % V7X_SUBLANES
    c_all = jnp.concatenate([c_prompt, c_sample, jnp.zeros((pad, d), F32)], axis=0)
    mod = _ada(c_all, w_ada[0], b_ada)
    shift, scale, gate = (mod[:n_c, i * d:(i + 1) * d].reshape(n_c, 1, d) for i in range(3))

    tn = 256
    nbc, nba = d_conv // tn, d_attn // tn
    w_blk = _blocked_weight(w_in[0], d_conv, d_attn, tn)
    w_out_b = w_out[0].astype(BF16)
    g3 = g_norm.reshape(1, 1, d)
    gf3 = g_final.reshape(1, 1, d)

    tm, tq = 1024, 256
    norm_p = dict(g=g3, scale=scale[:bp], shift=shift[:bp], nb=1, t=512)
    h_first = _prep(x_prompt, s0=0, s_len=tm, **norm_p).reshape(tm, d)
    h_keep = _prep(x_prompt, s0=sp - rows_kept, s_len=rows_kept, **norm_p).reshape(rows_kept, d)
    mix_conv_p, u_tail_p, qp, kp, vp, zp = _inproj_prompt(
        x_prompt.reshape(sp, d), g_norm, scale[0], shift[0], h_first, w_blk, conv_w[0], conv_b,
        d_conv=d_conv, d_attn=d_attn, tm=tm, tn=tn, halves=2)
    k_keep, v_keep = _proj(h_keep, w_blk, block0=nbc, n_blocks=nba, parts=(1, 2), out_dtypes=[F32, F32],
                           tm=rows_kept, tn=tn, name="kv_keep_prompt")
    bias_p = _bias_table(rel_bias[0], rows=tq, cols=3 * tq, rel0=2 * tq, band_cols=2 * tq)
    mix_attn_p = _attn_prompt(qp, kp, vp, zp, bias_p, tq=tq, heads=16)
    y_prompt = _out(x_prompt, mix_conv_p, mix_attn_p, w_out_b, gate[:bp], gf3, nb=1, t=512, tn=1024)

    hs = _prep(x_sample, g3, scale[bp:], shift[bp:], nb=8, t=ts).reshape(bs * ts, d)
    mix_conv_s, u_tail_s = _conv_sample(hs, w_blk, conv_w[0], conv_b, cache_conv[0],
                                        d_conv=d_conv, nb=bs, t=ts, tn=tn)
    qs, ks, vs, zs = _proj(hs, w_blk, block0=nbc, n_blocks=nba, parts=(0, 1, 2, 3), out_dtypes=[F32] * 4,
                           tm=bs * ts, tn=tn, name="qkvz_sample")
    bias_s = _bias_table(rel_bias[0], rows=ts, cols=r + V7X_LANES * (-(-ts // V7X_LANES)), rel0=r)
    to3 = lambda a: a.reshape(bs, ts, d_attn)
    mix_attn_s = _attn_sample(to3(qs), to3(ks), to3(vs), cache_k[0].reshape(bs, r * nh, HEAD_DIM),
                              cache_v[0].reshape(bs, r * nh, HEAD_DIM), to3(zs), bias_s)
    y_sample = _out(x_sample, mix_conv_s, mix_attn_s.reshape(bs * ts, d_attn), w_out_b, gate[bp:], gf3,
                    nb=16, t=ts, tn=1024)

    keep = CONV_WIDTH - 1
    new_k_prompt = k_keep.reshape(1, bp, rows_kept, nh, HEAD_DIM)
    new_v_prompt = v_keep.reshape(1, bp, rows_kept, nh, HEAD_DIM)
    new_conv_prompt = u_tail_p[-1, V7X_SUBLANES - keep:, :].reshape(1, bp, keep, d_conv)
    new_k_sample = ks.reshape(1, bs, ts, nh, HEAD_DIM)
    new_v_sample = vs.reshape(1, bs, ts, nh, HEAD_DIM)
    new_conv_sample = u_tail_s[:, V7X_SUBLANES - keep:, :].reshape(1, bs, keep, d_conv)
    return (y_prompt, y_sample, new_k_prompt, new_v_prompt, new_conv_prompt,
            new_k_sample, new_v_sample, new_conv_sample)
```

```python
import functools
import math

import jax
import jax.numpy as jnp
from jax import lax
from jax.experimental import pallas as pl
from jax.experimental.pallas import tpu as pltpu

CHUNK = 64
N_PAST_CHUNKS = 8
BAND_PAST = N_PAST_CHUNKS * CHUNK
HEAD_DIM = 128
CONV_WIDTH = 3
MAX_REL = 256
NORM_EPS = 1e-6
ATTN_SCALE = HEAD_DIM ** -0.5
LOG2E = math.log2(math.e)
N_PARTS = 4

V7X_LANES = 128
V7X_SUBLANES = 8
V7X_VMEM_BYTES = 64 * 1024 * 1024

F32 = jnp.float32
BF16 = jnp.bfloat16


def _nbytes(shape, dtype):
    n = 1
    for s in shape:
        n *= s
    return n * jnp.dtype(dtype).itemsize


def _params(semantics, blocks, extra_bytes=0):
    need = 2 * sum(_nbytes(s, d) for s, d in blocks) + extra_bytes + 16 * 1024 * 1024
    limit = int(min(need, V7X_VMEM_BYTES - 4 * 1024 * 1024))
    return pltpu.CompilerParams(dimension_semantics=semantics, vmem_limit_bytes=limit)


def _silu(z):
    return z * jax.nn.sigmoid(z)


def _mm(a, b):
    return jnp.dot(a, b, preferred_element_type=F32)


def _ada_kernel(c_ref, w_ref, b_ref, o_ref):
    o_ref[...] = _mm(c_ref[...].astype(BF16), w_ref[...].astype(BF16)) + b_ref[...]


def _ada(c, w_ada, b_ada, *, tn=512):
    r, d = c.shape
    n = w_ada.shape[1]
    assert n % tn == 0 and r % V7X_SUBLANES == 0
    blocks = [((r, d), F32), ((d, tn), F32), ((1, tn), F32), ((r, tn), F32), ((d, tn), BF16)]
    return pl.pallas_call(
        _ada_kernel,
        grid=(n // tn,),
        in_specs=[pl.BlockSpec((r, d), lambda j: (0, 0)),
                  pl.BlockSpec((d, tn), lambda j: (0, j)),
                  pl.BlockSpec((1, tn), lambda j: (0, j))],
        out_specs=pl.BlockSpec((r, tn), lambda j: (0, j)),
        out_shape=jax.ShapeDtypeStruct((r, n), F32),
        compiler_params=_params(("arbitrary",), blocks),
        name="ada",
    )(c, w_ada, b_ada)


def _modulated_norm(x, g, scale, shift):
    ms = jnp.mean(x * x, axis=-1, keepdims=True)
    xn = x * lax.rsqrt(ms + NORM_EPS)
    return (xn * g * (1.0 + scale) + shift).astype(BF16)


def _prep_kernel(x_ref, g_ref, scale_ref, shift_ref, h_ref):
    h_ref[...] = _modulated_norm(x_ref[...], g_ref[...], scale_ref[...], shift_ref[...])


def _prep(x, g, scale, shift, *, nb, t, s0=0, s_len=None):
    b, s, d = x.shape
    s_len = s if s_len is None else s_len
    assert b % nb == 0 and s_len % t == 0 and s0 % t == 0 and s0 + s_len <= s
    j0 = s0 // t
    blocks = [((nb, t, d), F32), ((nb, t, d), BF16), ((nb, t, d), F32)]
    return pl.pallas_call(
        _prep_kernel,
        grid=(b // nb, s_len // t),
        in_specs=[pl.BlockSpec((nb, t, d), lambda i, j: (i, j + j0, 0)),
                  pl.BlockSpec((1, 1, d), lambda i, j: (0, 0, 0)),
                  pl.BlockSpec((nb, 1, d), lambda i, j: (i, 0, 0)),
                  pl.BlockSpec((nb, 1, d), lambda i, j: (i, 0, 0))],
        out_specs=pl.BlockSpec((nb, t, d), lambda i, j: (i, j, 0)),
        out_shape=jax.ShapeDtypeStruct((b, s_len, d), BF16),
        compiler_params=_params(("arbitrary", "arbitrary"), blocks),
        name="prep",
    )(x, g, scale, shift)


def _blocked_weight_kernel(*refs, tn):
    o_ref = refs[-1]
    for p, w_ref in enumerate(refs[:-1]):
        o_ref[:, p * tn:(p + 1) * tn] = w_ref[...].astype(BF16)


def _blocked_weight(w_in, d_conv, d_attn, tn, *, tr=2048):
    d, n = w_in.shape
    nbc, nba = d_conv // tn, d_attn // tn
    assert d % tr == 0 and n == N_PARTS * (d_conv + d_attn) and d_conv % tn == 0 and d_attn % tn == 0

    def wspec(p):
        src = lambda j: jnp.where(j < nbc, p * nbc + j, N_PARTS * nbc + p * nba + j - nbc)
        return pl.BlockSpec((tr, tn), lambda i, j: (i, src(j)))

    blocks = [((tr, tn), F32)] * N_PARTS + [((tr, N_PARTS * tn), BF16)]
    return pl.pallas_call(
        functools.partial(_blocked_weight_kernel, tn=tn),
        grid=(d // tr, nbc + nba),
        in_specs=[wspec(p) for p in range(N_PARTS)],
        out_specs=pl.BlockSpec((tr, N_PARTS * tn), lambda i, j: (i, j)),
        out_shape=jax.ShapeDtypeStruct((d, n), BF16),
        compiler_params=_params(("arbitrary", "arbitrary"), blocks),
        name="blocked_weight",
    )(*([w_in] * N_PARTS))


def _conv_epilogue(u, prev1, prev2, bg, z, cw_ref, cb_ref):
    conv = cb_ref[...] + cw_ref[0:1, :] * prev2
    conv = conv + cw_ref[1:2, :] * prev1
    conv = conv + cw_ref[2:3, :] * u
    return (bg * conv) * _silu(z)


def _inproj_prompt_kernel(x_ref, g_ref, scale_ref, shift_ref, h0_ref, w_ref, cw_ref, cb_ref,
                          mix_ref, tail_ref, q_ref, k_ref, v_ref, z_ref, k_new_ref, v_new_ref, h_buf, halo_ref,
                          *, nbc, halves, tn):
    i = pl.program_id(0)
    j = pl.program_id(1)
    slot = lax.rem(i, 2)
    sm = x_ref.shape[0]
    hm = h_buf.shape[1] // halves

    @pl.when((i == 0) & (j == 0))
    def _():
        pltpu.sync_copy(h0_ref, h_buf.at[0])

    def prep_next_slab():
        h = _modulated_norm(x_ref[...], g_ref[...], scale_ref[...], shift_ref[...])
        h_buf[1 - slot, pl.ds(pl.multiple_of(j * sm, sm), sm), :] = h

    def part(h, p):
        return _mm(h, w_ref[:, p * tn:(p + 1) * tn])

    def sub_tiles():
        for s in range(halves):
            if s == halves - 1:
                prep_next_slab()
            yield slice(s * hm, (s + 1) * hm)

    @pl.when(j < nbc)
    def _conv_group():
        @pl.when(i == 0)
        def _():
            halo_ref[j] = jnp.zeros(halo_ref.shape[1:], F32)

        last8 = halo_ref[j]
        for rows in sub_tiles():
            h = h_buf[slot, rows, :]
            xin, bg, cg, z = part(h, 0), part(h, 1), part(h, 2), part(h, 3)
            u = cg * xin
            last1 = last8[V7X_SUBLANES - 1:V7X_SUBLANES, :]
            last2 = last8[V7X_SUBLANES - 2:V7X_SUBLANES - 1, :]
            row = lax.broadcasted_iota(jnp.int32, u.shape, 0)
            prev1 = jnp.where(row == 0, last1, pltpu.roll(u, 1, 0))
            prev2 = jnp.where(row == 0, last2, jnp.where(row == 1, last1, pltpu.roll(u, 2, 0)))
            mix_ref[rows, :] = _conv_epilogue(u, prev1, prev2, bg, z, cw_ref, cb_ref).astype(BF16)
            last8 = u[hm - V7X_SUBLANES:, :]
        halo_ref[j] = last8
        tail_ref[0] = last8

    @pl.when(j >= nbc)
    def _attn_group():
        for rows in sub_tiles():
            h = h_buf[slot, rows, :]
            k, v = part(h, 1), part(h, 2)
            q_ref[rows, :] = part(h, 0).astype(BF16)
            k_ref[rows, :] = k.astype(BF16)
            v_ref[rows, :] = v.astype(BF16)
            z_ref[rows, :] = part(h, 3)
        k_new_ref[0] = k
        v_new_ref[0] = v


def _inproj_prompt(x, g, scale, shift, h0, w_blk, conv_w, conv_b, *, d_conv, d_attn, tm, tn, halves):
    s, d = x.shape
    assert s % tm == 0 and d_conv % tn == 0 and d_attn % tn == 0 and (tm // halves) % V7X_SUBLANES == 0
    nbc, nba = d_conv // tn, d_attn // tn
    nsteps = nbc + nba
    nrt = s // tm
    hm = tm // halves
    sm = tm // nsteps
    assert tm % nsteps == 0 and sm % (2 * V7X_SUBLANES) == 0 and h0.shape == (tm, d)
    cj = lambda j: jnp.minimum(j, nbc - 1)
    aj = lambda j: jnp.maximum(j - nbc, 0)
    attn_spec = pl.BlockSpec((tm, tn), lambda i, j: (i, aj(j)))
    newest_spec = pl.BlockSpec((1, hm, tn), lambda i, j: (i, 0, aj(j)))
    row_vec = pl.BlockSpec((1, d), lambda i, j: (0, 0))
    blocks = ([((sm, d), F32), ((d, N_PARTS * tn), BF16)] + [((tm, tn), BF16)] * 4 + [((tm, tn), F32)]
              + [((hm, tn), F32)] * 10)
    return pl.pallas_call(
        functools.partial(_inproj_prompt_kernel, nbc=nbc, halves=halves, tn=tn),
        grid=(nrt, nsteps),
        in_specs=[pl.BlockSpec((sm, d), lambda i, j: (jnp.minimum(i + 1, nrt - 1) * nsteps + j, 0)),
                  row_vec, row_vec, row_vec,
                  pl.BlockSpec(memory_space=pl.ANY),
                  pl.BlockSpec((d, N_PARTS * tn), lambda i, j: (0, j)),
                  pl.BlockSpec((CONV_WIDTH, tn), lambda i, j: (0, cj(j))),
                  pl.BlockSpec((1, tn), lambda i, j: (0, cj(j)))],
        out_specs=[pl.BlockSpec((tm, tn), lambda i, j: (i, cj(j))),
                   pl.BlockSpec((1, V7X_SUBLANES, tn), lambda i, j: (i, 0, cj(j))),
                   attn_spec, attn_spec, attn_spec, attn_spec, newest_spec, newest_spec],
        out_shape=[jax.ShapeDtypeStruct((s, d_conv), BF16),
                   jax.ShapeDtypeStruct((nrt, V7X_SUBLANES, d_conv), F32),
                   jax.ShapeDtypeStruct((s, d_attn), BF16),
                   jax.ShapeDtypeStruct((s, d_attn), BF16),
                   jax.ShapeDtypeStruct((s, d_attn), BF16),
                   jax.ShapeDtypeStruct((s, d_attn), F32),
                   jax.ShapeDtypeStruct((nrt, hm, d_attn), F32),
                   jax.ShapeDtypeStruct((nrt, hm, d_attn), F32)],
        scratch_shapes=[pltpu.VMEM((2, tm, d), BF16), pltpu.VMEM((nbc, V7X_SUBLANES, tn), F32)],
        compiler_params=_params(("arbitrary", "arbitrary"), blocks, extra_bytes=_nbytes((2, tm, d), BF16)),
        name="inproj_prompt",
    )(x, g, scale, shift, h0, w_blk, conv_w, conv_b)


def _conv_sample_kernel(h_ref, w_ref, cw_ref, cb_ref, cache_ref, mix_ref, tail_ref, *, nb, t, tn):
    h = h_ref[...]
    xin, bg, cg, z = (_mm(h, w_ref[:, p * tn:(p + 1) * tn]) for p in range(N_PARTS))
    u = cg * xin
    cache = cache_ref[...]
    c2 = jnp.broadcast_to(cache[:, 0:1, :], (nb, t, tn)).reshape(nb * t, tn)
    c1 = jnp.broadcast_to(cache[:, 1:2, :], (nb, t, tn)).reshape(nb * t, tn)
    pos = lax.broadcasted_iota(jnp.int32, (nb, t, tn), 1).reshape(nb * t, tn)
    prev1 = jnp.where(pos == 0, c1, pltpu.roll(u, 1, 0))
    prev2 = jnp.where(pos == 0, c2, jnp.where(pos == 1, c1, pltpu.roll(u, 2, 0)))
    mix_ref[...] = _conv_epilogue(u, prev1, prev2, bg, z, cw_ref, cb_ref).astype(BF16)
    tail_ref[...] = u.reshape(nb, t, tn)[:, t - V7X_SUBLANES:, :]


def _conv_sample(h, w_blk, conv_w, conv_b, cache_conv, *, d_conv, nb, t, tn):
    m, d = h.shape
    assert m == nb * t and d_conv % tn == 0 and t % V7X_SUBLANES == 0 and t >= CONV_WIDTH - 1
    blocks = [((m, d), BF16), ((d, N_PARTS * tn), BF16), ((m, tn), BF16)] + [((m, tn), F32)] * 10
    return pl.pallas_call(
        functools.partial(_conv_sample_kernel, nb=nb, t=t, tn=tn),
        grid=(d_conv // tn,),
        in_specs=[pl.BlockSpec((m, d), lambda j: (0, 0)),
                  pl.BlockSpec((d, N_PARTS * tn), lambda j: (0, j)),
                  pl.BlockSpec((CONV_WIDTH, tn), lambda j: (0, j)),
                  pl.BlockSpec((1, tn), lambda j: (0, j)),
                  pl.BlockSpec((nb, CONV_WIDTH - 1, tn), lambda j: (0, 0, j))],
        out_specs=[pl.BlockSpec((m, tn), lambda j: (0, j)),
                   pl.BlockSpec((nb, V7X_SUBLANES, tn), lambda j: (0, 0, j))],
        out_shape=[jax.ShapeDtypeStruct((m, d_conv), BF16),
                   jax.ShapeDtypeStruct((nb, V7X_SUBLANES, d_conv), F32)],
        compiler_params=_params(("arbitrary",), blocks),
        name="conv_sample",
    )(h, w_blk, conv_w, conv_b, cache_conv)


def _proj_kernel(h_ref, w_ref, *o_refs, parts, tn):
    h = h_ref[...]
    for p, o_ref in zip(parts, o_refs):
        o_ref[...] = _mm(h, w_ref[:, p * tn:(p + 1) * tn]).astype(o_ref.dtype)


def _proj(h, w_blk, *, block0, n_blocks, parts, out_dtypes, tm, tn, name):
    m, d = h.shape
    assert m % tm == 0 and len(parts) == len(out_dtypes)
    blocks = ([((tm, d), BF16), ((d, N_PARTS * tn), BF16)]
              + [((tm, tn), dt) for dt in out_dtypes] + [((tm, tn), F32)] * len(parts))
    return pl.pallas_call(
        functools.partial(_proj_kernel, parts=parts, tn=tn),
        grid=(m // tm, n_blocks),
        in_specs=[pl.BlockSpec((tm, d), lambda i, j: (i, 0)),
                  pl.BlockSpec((d, N_PARTS * tn), lambda i, j: (0, block0 + j))],
        out_specs=[pl.BlockSpec((tm, tn), lambda i, j: (i, j)) for _ in out_dtypes],
        out_shape=[jax.ShapeDtypeStruct((m, n_blocks * tn), dt) for dt in out_dtypes],
        compiler_params=_params(("arbitrary", "arbitrary"), blocks),
        name=name,
    )(h, w_blk)


def _bias_seq(rel_bias, offset, n):
    rev = rel_bias[:, ::-1]
    left = offset - MAX_REL
    assert left >= 0
    right = max(n - left - rev.shape[1], 0)
    return jnp.pad(rev, ((0, 0), (left, right)), mode="edge")[:, :n]


def _bias_table_kernel(seq_ref, o_ref, *, rows, cols, lane0, band_cols):
    width = seq_ref.shape[-1]
    seq = jnp.broadcast_to(seq_ref[0], (rows, width))
    table = pltpu.roll(seq, width - lane0, 1, stride=1, stride_axis=0)[:, :cols] * LOG2E
    if band_cols is not None:
        qc = lax.broadcasted_iota(jnp.int32, (rows, cols), 0) // CHUNK
        kc = lax.broadcasted_iota(jnp.int32, (rows, cols), 1) // CHUNK - band_cols // CHUNK
        table = jnp.where((kc <= qc) & (kc >= qc - N_PAST_CHUNKS), table, -jnp.inf)
    o_ref[0] = table


def _bias_table(rel_bias, *, rows, cols, rel0, band_cols=None):
    nh = rel_bias.shape[0]
    lane0 = V7X_LANES * (-(-rows // V7X_LANES))
    width = 1 << (lane0 + cols - 1).bit_length()
    seq = _bias_seq(rel_bias, rel0 + lane0, width)
    blocks = [((1, width), F32), ((rows, cols), F32), ((rows, width), F32), ((rows, width), F32)]
    return pl.pallas_call(
        functools.partial(_bias_table_kernel, rows=rows, cols=cols, lane0=lane0, band_cols=band_cols),
        grid=(nh,),
        in_specs=[pl.BlockSpec((1, 1, width), lambda h: (h, 0, 0))],
        out_specs=pl.BlockSpec((1, rows, cols), lambda h: (h, 0, 0)),
        out_shape=jax.ShapeDtypeStruct((nh, rows, cols), F32),
        compiler_params=_params(("arbitrary",), blocks),
        name="bias_table",
    )(seq.reshape(nh, 1, width))


def _attn_prompt_kernel(q_ref, k0_ref, k1_ref, k2_ref, v0_ref, v1_ref, v2_ref, z_ref, bias_ref, o_ref,
                        *, heads, tq):
    b = pl.program_id(1)
    nk = 3 * tq
    dims = (((1,), (1,)), ((), ()))

    def run(mask_keys):
        if mask_keys:
            kc = lax.broadcasted_iota(jnp.int32, (1, nk), 1) // CHUNK
            valid = kc >= (2 - b) * (tq // CHUNK)
        for hh in range(heads):
            sl = slice(hh * HEAD_DIM, (hh + 1) * HEAD_DIM)
            k = jnp.concatenate([k0_ref[:, sl], k1_ref[:, sl], k2_ref[:, sl]], axis=0)
            v = jnp.concatenate([v0_ref[:, sl], v1_ref[:, sl], v2_ref[:, sl]], axis=0)
            s = lax.dot_general(q_ref[:, sl], k, dims, preferred_element_type=F32)
            s = s * (ATTN_SCALE * LOG2E) + bias_ref[hh]
            if mask_keys:
                s = jnp.where(valid, s, -jnp.inf)
            m = jnp.max(s, axis=-1, keepdims=True)
            p = jnp.exp2(s - m)
            l = jnp.sum(p, axis=-1, keepdims=True)
            o = _mm(p.astype(BF16), v) / l
            o_ref[:, sl] = (o * _silu(z_ref[:, sl])).astype(BF16)

    pl.when(b < 2)(lambda: run(True))
    pl.when(b >= 2)(lambda: run(False))


def _attn_prompt(q, k, v, z, bias, *, tq, heads):
    s, da = q.shape
    nh = da // HEAD_DIM
    wcols = heads * HEAD_DIM
    assert s % tq == 0 and nh % heads == 0 and tq % CHUNK == 0 and 2 * tq >= BAND_PAST
    kspec = lambda back: pl.BlockSpec((tq, wcols), lambda g, b, back=back: (jnp.maximum(b - back, 0), g))
    blocks = ([((tq, wcols), BF16)] * 8 + [((tq, wcols), F32)] + [((heads, tq, 3 * tq), F32)]
              + [((tq, 3 * tq), F32)] * 8)
    return pl.pallas_call(
        functools.partial(_attn_prompt_kernel, heads=heads, tq=tq),
        grid=(nh // heads, s // tq),
        in_specs=[pl.BlockSpec((tq, wcols), lambda g, b: (b, g)),
                  kspec(2), kspec(1), kspec(0), kspec(2), kspec(1), kspec(0),
                  pl.BlockSpec((tq, wcols), lambda g, b: (b, g)),
                  pl.BlockSpec((heads, tq, 3 * tq), lambda g, b: (g, 0, 0))],
        out_specs=pl.BlockSpec((tq, wcols), lambda g, b: (b, g)),
        out_shape=jax.ShapeDtypeStruct((s, da), BF16),
        compiler_params=_params(("arbitrary", "arbitrary"), blocks),
        name="attn_prompt",
    )(q, k, k, k, v, v, v, z, bias)


def _attn_sample_kernel(q_ref, kn_ref, vn_ref, ck_ref, cv_ref, z_ref, bias_ref, o_ref, *, nh, r, t):
    dims = (((1,), (1,)), ((), ()))
    for bb in range(q_ref.shape[0]):
        for hh in range(nh):
            sl = slice(hh * HEAD_DIM, (hh + 1) * HEAD_DIM)
            head_rows = pl.ds(hh, r, stride=nh)
            q = q_ref[bb, :, sl].astype(BF16)
            sc = lax.dot_general(q, ck_ref[bb, head_rows, :].astype(BF16), dims, preferred_element_type=F32)
            sn = lax.dot_general(q, kn_ref[bb, :, sl].astype(BF16), dims, preferred_element_type=F32)
            sc = sc * (ATTN_SCALE * LOG2E) + bias_ref[hh, :, 0:r]
            sn = sn * (ATTN_SCALE * LOG2E) + bias_ref[hh, :, r:r + t]
            m = jnp.maximum(jnp.max(sc, axis=-1, keepdims=True), jnp.max(sn, axis=-1, keepdims=True))
            pc = jnp.exp2(sc - m)
            pn = jnp.exp2(sn - m)
            l = jnp.sum(pc, axis=-1, keepdims=True) + jnp.sum(pn, axis=-1, keepdims=True)
            o = _mm(pc.astype(BF16), cv_ref[bb, head_rows, :].astype(BF16))
            o = o + _mm(pn.astype(BF16), vn_ref[bb, :, sl].astype(BF16))
            o_ref[bb, :, sl] = ((o / l) * _silu(z_ref[bb, :, sl])).astype(BF16)


def _attn_sample(q, kn, vn, cache_k, cache_v, z, bias, *, nbs):
    nb, t, da = q.shape
    nh = da // HEAD_DIM
    r = cache_k.shape[1] // nh
    assert r % V7X_LANES == 0 and bias.shape[2] >= r + t and nb % nbs == 0
    new = pl.BlockSpec((nbs, t, da), lambda b: (b, 0, 0))
    old = pl.BlockSpec((nbs, r * nh, HEAD_DIM), lambda b: (b, 0, 0))
    blocks = ([((nbs, t, da), F32)] * 5 + [((nbs, r, da), F32)] * 2 + [(bias.shape, F32)]
              + [((nbs, r, da), BF16)] * 2)
    return pl.pallas_call(
        functools.partial(_attn_sample_kernel, nh=nh, r=r, t=t),
        grid=(nb // nbs,),
        in_specs=[new, new, new, old, old, new, pl.BlockSpec(bias.shape, lambda b: (0, 0, 0))],
        out_specs=new,
        out_shape=jax.ShapeDtypeStruct((nb, t, da), BF16),
        compiler_params=_params(("arbitrary",), blocks),
        name="attn_sample",
    )(q, kn, vn, cache_k, cache_v, z, bias)


def _out_kernel(x_ref, ma_ref, mb_ref, wa_ref, wb_ref, gate_ref, gf_ref, y_ref, ss_ref, *, nj, tn):
    j = pl.program_id(1)
    acc = _mm(ma_ref[...], wa_ref[...]) + _mm(mb_ref[...], wb_ref[...])
    nb, t, _ = x_ref.shape
    res = x_ref[...] + gate_ref[...] * acc.reshape(nb, t, tn)
    y_ref[:, :, pl.ds(pl.multiple_of(j * tn, tn), tn)] = res
    part = jnp.sum(res * res, axis=-1, keepdims=True)

    @pl.when(j == 0)
    def _():
        ss_ref[...] = part

    @pl.when(j > 0)
    def _():
        ss_ref[...] += part

    @pl.when(j == nj - 1)
    def _():
        inv = lax.rsqrt(ss_ref[...] * (1.0 / (nj * tn)) + NORM_EPS)
        for n in range(nj):
            cols = slice(n * tn, (n + 1) * tn)
            y_ref[:, :, cols] = (y_ref[:, :, cols] * inv) * gf_ref[:, :, cols]


def _out(x, mix_a, mix_b, w_out, gate, g_final, *, nb, t, tn):
    b, s, d = x.shape
    half = mix_a.shape[1]
    tm = nb * t
    assert b % nb == 0 and s % t == 0 and d % tn == 0 and w_out.shape[0] == 2 * half
    nj = d // tn
    nt = s // t
    assert nb == 1 or nt == 1, "a row tile must be contiguous in the flattened (B*S) mixed rows"
    blocks = ([((nb, t, tn), F32)] + [((tm, half), BF16)] * 2 + [((half, tn), BF16)] * 2
              + [((nb, t, d), F32)] + [((tm, tn), F32)] * 4)
    return pl.pallas_call(
        functools.partial(_out_kernel, nj=nj, tn=tn),
        grid=((b // nb) * nt, nj),
        in_specs=[pl.BlockSpec((nb, t, tn), lambda i, j: (i // nt, i % nt, j)),
                  pl.BlockSpec((tm, half), lambda i, j: (i, 0)),
                  pl.BlockSpec((tm, half), lambda i, j: (i, 0)),
                  pl.BlockSpec((half, tn), lambda i, j: (0, j)),
                  pl.BlockSpec((half, tn), lambda i, j: (1, j)),
                  pl.BlockSpec((nb, 1, tn), lambda i, j: (i // nt, 0, j)),
                  pl.BlockSpec((1, 1, d), lambda i, j: (0, 0, 0))],
        out_specs=pl.BlockSpec((nb, t, d), lambda i, j: (i // nt, i % nt, 0), pipeline_mode=pl.Buffered(1)),
        out_shape=jax.ShapeDtypeStruct((b, s, d), F32),
        scratch_shapes=[pltpu.VMEM((nb, t, 1), F32)],
        compiler_params=_params(("arbitrary", "arbitrary"), blocks),
        name="out_proj",
    )(x, mix_a, mix_b, w_out, w_out, gate, g_final)


def kernel(x_prompt, x_sample, cache_k, cache_v, cache_conv, c_prompt, c_sample,
           g_norm, w_ada, b_ada, w_in, conv_w, conv_b, rel_bias, w_out, g_final):
    depth = g_norm.shape[0]
    assert depth == 1, "single-layer trunk"
    bp, sp, d = x_prompt.shape
    bs, ts, _ = x_sample.shape
    assert bp == 1
    d_conv = conv_w.shape[-1]
    d_attn = w_out.shape[1] - d_conv
    nh = d_attn // HEAD_DIM
    r = cache_k.shape[2]
    rows_kept = min(BAND_PAST, sp)

    n_c = bp + bs
    pad = (-n_c) % V7X_SUBLANES
    c_all = jnp.concatenate([c_prompt, c_sample, jnp.zeros((pad, d), F32)], axis=0)
    mod = _ada(c_all, w_ada[0], b_ada)
    shift, scale, gate = (mod[:n_c, i * d:(i + 1) * d].reshape(n_c, 1, d) for i in range(3))

    tn = 256
    nbc, nba = d_conv // tn, d_attn // tn
    w_blk = _blocked_weight(w_in[0], d_conv, d_attn, tn)
    w_out_b = w_out[0].astype(BF16)
    g3 = g_norm.reshape(1, 1, d)
    gf3 = g_final.reshape(1, 1, d)

    tm, tq, halves = 1024, 256, 2
    assert rows_kept == tm // halves, "the kept K/V rows are the last sub-tile of the last row tile"
    h_first = _prep(x_prompt, g3, scale[:bp], shift[:bp], nb=1, t=512, s0=0, s_len=tm).reshape(tm, d)
    mix_conv_p, u_tail_p, qp, kp, vp, zp, k_keep, v_keep = _inproj_prompt(
        x_prompt.reshape(sp, d), g_norm, scale[0], shift[0], h_first, w_blk, conv_w[0], conv_b,
        d_conv=d_conv, d_attn=d_attn, tm=tm, tn=tn, halves=halves)
    bias_p = _bias_table(rel_bias[0], rows=tq, cols=3 * tq, rel0=2 * tq, band_cols=2 * tq)
    mix_attn_p = _attn_prompt(qp, kp, vp, zp, bias_p, tq=tq, heads=16)
    y_prompt = _out(x_prompt, mix_conv_p, mix_attn_p, w_out_b, gate[:bp], gf3, nb=1, t=1024, tn=512)

    hs = _prep(x_sample, g3, scale[bp:], shift[bp:], nb=8, t=ts).reshape(bs * ts, d)
    mix_conv_s, u_tail_s = _conv_sample(hs, w_blk, conv_w[0], conv_b, cache_conv[0],
                                        d_conv=d_conv, nb=bs, t=ts, tn=tn)
    qs, ks, vs, zs = _proj(hs, w_blk, block0=nbc, n_blocks=nba, parts=(0, 1, 2, 3), out_dtypes=[F32] * 4,
                           tm=bs * ts, tn=tn, name="qkvz_sample")
    bias_s = _bias_table(rel_bias[0], rows=ts, cols=r + V7X_LANES * (-(-ts // V7X_LANES)), rel0=r)
    to3 = lambda a: a.reshape(bs, ts, d_attn)
    mix_attn_s = _attn_sample(to3(qs), to3(ks), to3(vs), cache_k[0].reshape(bs, r * nh, HEAD_DIM),
                              cache_v[0].reshape(bs, r * nh, HEAD_DIM), to3(zs), bias_s, nbs=2)
    y_sample = _out(x_sample, mix_conv_s, mix_attn_s.reshape(bs * ts, d_attn), w_out_b, gate[bp:], gf3,
                    nb=bs, t=ts, tn=512)

    keep = CONV_WIDTH - 1
    new_k_prompt = k_keep[-1].reshape(1, bp, rows_kept, nh, HEAD_DIM)
    new_v_prompt = v_keep[-1].reshape(1, bp, rows_kept, nh, HEAD_DIM)
    new_conv_prompt = u_tail_p[-1, V7X_SUBLANES - keep:, :].reshape(1, bp, keep, d_conv)
    new_k_sample = ks.reshape(1, bs, ts, nh, HEAD_DIM)
    new_v_sample = vs.reshape(1, bs, ts, nh, HEAD_DIM)
    new_conv_sample = u_tail_s[:, V7X_SUBLANES - keep:, :].reshape(1, bs, keep, d_conv)
    return (y_prompt, y_sample, new_k_prompt, new_v_prompt, new_conv_prompt,
            new_k_sample, new_v_sample, new_conv_sample)
```

```python
import functools
import math

import jax
import jax.numpy as jnp
from jax import lax
from jax.experimental import pallas as pl
from jax.experimental.pallas import tpu as pltpu

CHUNK = 64
N_PAST_CHUNKS = 8
BAND_PAST = N_PAST_CHUNKS * CHUNK
HEAD_DIM = 128
CONV_WIDTH = 3
MAX_REL = 256
NORM_EPS = 1e-6
ATTN_SCALE = HEAD_DIM ** -0.5
LOG2E = math.log2(math.e)
N_PARTS = 4

V7X_LANES = 128
V7X_SUBLANES = 8
V7X_VMEM_BYTES = 64 * 1024 * 1024

F32 = jnp.float32
BF16 = jnp.bfloat16


def _nbytes(shape, dtype):
    n = 1
    for s in shape:
        n *= s
    return n * jnp.dtype(dtype).itemsize


def _params(semantics, blocks, extra_bytes=0):
    need = 2 * sum(_nbytes(s, d) for s, d in blocks) + extra_bytes + 16 * 1024 * 1024
    limit = int(min(need, V7X_VMEM_BYTES - 4 * 1024 * 1024))
    return pltpu.CompilerParams(dimension_semantics=semantics, vmem_limit_bytes=limit)


def _silu(z):
    return z * jax.nn.sigmoid(z)


def _mm(a, b):
    return jnp.dot(a, b, preferred_element_type=F32)


def _ada_kernel(c_ref, w_ref, b_ref, o_ref):
    o_ref[...] = _mm(c_ref[...].astype(BF16), w_ref[...].astype(BF16)) + b_ref[...]


def _ada(c, w_ada, b_ada, *, tn=512):
    r, d = c.shape
    n = w_ada.shape[1]
    assert n % tn == 0 and r % V7X_SUBLANES == 0
    blocks = [((r, d), F32), ((d, tn), F32), ((1, tn), F32), ((r, tn), F32), ((d, tn), BF16)]
    return pl.pallas_call(
        _ada_kernel,
        grid=(n // tn,),
        in_specs=[pl.BlockSpec((r, d), lambda j: (0, 0)),
                  pl.BlockSpec((d, tn), lambda j: (0, j)),
                  pl.BlockSpec((1, tn), lambda j: (0, j))],
        out_specs=pl.BlockSpec((r, tn), lambda j: (0, j)),
        out_shape=jax.ShapeDtypeStruct((r, n), F32),
        compiler_params=_params(("arbitrary",), blocks),
        name="ada",
    )(c, w_ada, b_ada)


def _modulated_norm(x, g, scale, shift):
    ms = jnp.mean(x * x, axis=-1, keepdims=True)
    xn = x * lax.rsqrt(ms + NORM_EPS)
    return (xn * g * (1.0 + scale) + shift).astype(BF16)


def _prep_kernel(x_ref, g_ref, scale_ref, shift_ref, h_ref):
    h_ref[...] = _modulated_norm(x_ref[...], g_ref[...], scale_ref[...], shift_ref[...])


def _prep(x, g, scale, shift, *, nb, t, s0=0, s_len=None):
    b, s, d = x.shape
    s_len = s if s_len is None else s_len
    assert b % nb == 0 and s_len % t == 0 and s0 % t == 0 and s0 + s_len <= s
    j0 = s0 // t
    blocks = [((nb, t, d), F32), ((nb, t, d), BF16), ((nb, t, d), F32)]
    return pl.pallas_call(
        _prep_kernel,
        grid=(b // nb, s_len // t),
        in_specs=[pl.BlockSpec((nb, t, d), lambda i, j: (i, j + j0, 0)),
                  pl.BlockSpec((1, 1, d), lambda i, j: (0, 0, 0)),
                  pl.BlockSpec((nb, 1, d), lambda i, j: (i, 0, 0)),
                  pl.BlockSpec((nb, 1, d), lambda i, j: (i, 0, 0))],
        out_specs=pl.BlockSpec((nb, t, d), lambda i, j: (i, j, 0)),
        out_shape=jax.ShapeDtypeStruct((b, s_len, d), BF16),
        compiler_params=_params(("arbitrary", "arbitrary"), blocks),
        name="prep",
    )(x, g, scale, shift)


def _blocked_weight_kernel(*refs, tn):
    o_ref = refs[-1]
    for p, w_ref in enumerate(refs[:-1]):
        o_ref[:, p * tn:(p + 1) * tn] = w_ref[...].astype(BF16)


def _blocked_weight(w_in, d_conv, d_attn, tn, *, tr=2048):
    d, n = w_in.shape
    nbc, nba = d_conv // tn, d_attn // tn
    assert d % tr == 0 and n == N_PARTS * (d_conv + d_attn) and d_conv % tn == 0 and d_attn % tn == 0

    def wspec(p):
        src = lambda j: jnp.where(j < nbc, p * nbc + j, N_PARTS * nbc + p * nba + j - nbc)
        return pl.BlockSpec((tr, tn), lambda i, j: (i, src(j)))

    blocks = [((tr, tn), F32)] * N_PARTS + [((tr, N_PARTS * tn), BF16)]
    return pl.pallas_call(
        functools.partial(_blocked_weight_kernel, tn=tn),
        grid=(d // tr, nbc + nba),
        in_specs=[wspec(p) for p in range(N_PARTS)],
        out_specs=pl.BlockSpec((tr, N_PARTS * tn), lambda i, j: (i, j)),
        out_shape=jax.ShapeDtypeStruct((d, n), BF16),
        compiler_params=_params(("arbitrary", "arbitrary"), blocks),
        name="blocked_weight",
    )(*([w_in] * N_PARTS))


def _conv_epilogue(u, prev1, prev2, bg, z, cw_ref, cb_ref):
    conv = cb_ref[...] + cw_ref[0:1, :] * prev2
    conv = conv + cw_ref[1:2, :] * prev1
    conv = conv + cw_ref[2:3, :] * u
    return (bg * conv) * _silu(z)


def _inproj_prompt_kernel(x_ref, g_ref, scale_ref, shift_ref, h0_ref, w_ref, cw_ref, cb_ref,
                          mix_ref, tail_ref, q_ref, k_ref, v_ref, z_ref, k_new_ref, v_new_ref, h_buf, halo_ref,
                          *, nbc, halves, tn):
    i = pl.program_id(0)
    j = pl.program_id(1)
    slot = lax.rem(i, 2)
    sm = x_ref.shape[0]
    hm = h_buf.shape[1] // halves

    @pl.when((i == 0) & (j == 0))
    def _():
        pltpu.sync_copy(h0_ref, h_buf.at[0])

    def prep_next_slab():
        h = _modulated_norm(x_ref[...], g_ref[...], scale_ref[...], shift_ref[...])
        h_buf[1 - slot, pl.ds(pl.multiple_of(j * sm, sm), sm), :] = h

    def part(h, p):
        return _mm(h, w_ref[:, p * tn:(p + 1) * tn])

    def sub_tiles():
        for s in range(halves):
            if s == halves - 1:
                prep_next_slab()
            yield slice(s * hm, (s + 1) * hm)

    @pl.when(j < nbc)
    def _conv_group():
        @pl.when(i == 0)
        def _():
            halo_ref[j] = jnp.zeros(halo_ref.shape[1:], F32)

        last8 = halo_ref[j]
        for rows in sub_tiles():
            h = h_buf[slot, rows, :]
            xin, bg, cg, z = part(h, 0), part(h, 1), part(h, 2), part(h, 3)
            u = cg * xin
            last1 = last8[V7X_SUBLANES - 1:V7X_SUBLANES, :]
            last2 = last8[V7X_SUBLANES - 2:V7X_SUBLANES - 1, :]
            row = lax.broadcasted_iota(jnp.int32, u.shape, 0)
            prev1 = jnp.where(row == 0, last1, pltpu.roll(u, 1, 0))
            prev2 = jnp.where(row == 0, last2, jnp.where(row == 1, last1, pltpu.roll(u, 2, 0)))
            mix_ref[rows, :] = _conv_epilogue(u, prev1, prev2, bg, z, cw_ref, cb_ref).astype(BF16)
            last8 = u[hm - V7X_SUBLANES:, :]
        halo_ref[j] = last8
        tail_ref[0] = last8

    @pl.when(j >= nbc)
    def _attn_group():
        first_new = h_buf.shape[1] - k_new_ref.shape[1]
        for rows in sub_tiles():
            h = h_buf[slot, rows, :]
            k, v = part(h, 1), part(h, 2)
            q_ref[rows, :] = part(h, 0).astype(BF16)
            k_ref[rows, :] = k.astype(BF16)
            v_ref[rows, :] = v.astype(BF16)
            z_ref[rows, :] = part(h, 3)
            if rows.start >= first_new:
                new_rows = slice(rows.start - first_new, rows.stop - first_new)
                k_new_ref[0, new_rows, :] = k
                v_new_ref[0, new_rows, :] = v


def _inproj_prompt(x, g, scale, shift, h0, w_blk, conv_w, conv_b, *, d_conv, d_attn, tm, tn, halves, n_new):
    s, d = x.shape
    assert s % tm == 0 and d_conv % tn == 0 and d_attn % tn == 0 and (tm // halves) % V7X_SUBLANES == 0
    nbc, nba = d_conv // tn, d_attn // tn
    nsteps = nbc + nba
    nrt = s // tm
    hm = tm // halves
    sm = tm // nsteps
    assert tm % nsteps == 0 and sm % (2 * V7X_SUBLANES) == 0 and h0.shape == (tm, d)
    assert n_new % hm == 0 and n_new <= tm, "the kept K/V rows are whole sub-tiles of the last row tile"
    cj = lambda j: jnp.minimum(j, nbc - 1)
    aj = lambda j: jnp.maximum(j - nbc, 0)
    attn_spec = pl.BlockSpec((tm, tn), lambda i, j: (i, aj(j)))
    newest_spec = pl.BlockSpec((1, n_new, tn), lambda i, j: (i, 0, aj(j)))
    row_vec = pl.BlockSpec((1, d), lambda i, j: (0, 0))
    blocks = ([((sm, d), F32), ((d, N_PARTS * tn), BF16)] + [((tm, tn), BF16)] * 4 + [((tm, tn), F32)]
              + [((n_new, tn), F32)] * 2 + [((hm, tn), F32)] * 8)
    return pl.pallas_call(
        functools.partial(_inproj_prompt_kernel, nbc=nbc, halves=halves, tn=tn),
        grid=(nrt, nsteps),
        in_specs=[pl.BlockSpec((sm, d), lambda i, j: (jnp.minimum(i + 1, nrt - 1) * nsteps + j, 0)),
                  row_vec, row_vec, row_vec,
                  pl.BlockSpec(memory_space=pl.ANY),
                  pl.BlockSpec((d, N_PARTS * tn), lambda i, j: (0, j)),
                  pl.BlockSpec((CONV_WIDTH, tn), lambda i, j: (0, cj(j))),
                  pl.BlockSpec((1, tn), lambda i, j: (0, cj(j)))],
        out_specs=[pl.BlockSpec((tm, tn), lambda i, j: (i, cj(j))),
                   pl.BlockSpec((1, V7X_SUBLANES, tn), lambda i, j: (i, 0, cj(j))),
                   attn_spec, attn_spec, attn_spec, attn_spec, newest_spec, newest_spec],
        out_shape=[jax.ShapeDtypeStruct((s, d_conv), BF16),
                   jax.ShapeDtypeStruct((nrt, V7X_SUBLANES, d_conv), F32),
                   jax.ShapeDtypeStruct((s, d_attn), BF16),
                   jax.ShapeDtypeStruct((s, d_attn), BF16),
                   jax.ShapeDtypeStruct((s, d_attn), BF16),
                   jax.ShapeDtypeStruct((s, d_attn), F32),
                   jax.ShapeDtypeStruct((nrt, n_new, d_attn), F32),
                   jax.ShapeDtypeStruct((nrt, n_new, d_attn), F32)],
        scratch_shapes=[pltpu.VMEM((2, tm, d), BF16), pltpu.VMEM((nbc, V7X_SUBLANES, tn), F32)],
        compiler_params=_params(("arbitrary", "arbitrary"), blocks, extra_bytes=_nbytes((2, tm, d), BF16)),
        name="inproj_prompt",
    )(x, g, scale, shift, h0, w_blk, conv_w, conv_b)


def _conv_sample_kernel(h_ref, w_ref, cw_ref, cb_ref, cache_ref, mix_ref, tail_ref, *, nb, t, tn):
    h = h_ref[...]
    xin, bg, cg, z = (_mm(h, w_ref[:, p * tn:(p + 1) * tn]) for p in range(N_PARTS))
    u = cg * xin
    cache = cache_ref[...]
    c2 = jnp.broadcast_to(cache[:, 0:1, :], (nb, t, tn)).reshape(nb * t, tn)
    c1 = jnp.broadcast_to(cache[:, 1:2, :], (nb, t, tn)).reshape(nb * t, tn)
    pos = lax.broadcasted_iota(jnp.int32, (nb, t, tn), 1).reshape(nb * t, tn)
    prev1 = jnp.where(pos == 0, c1, pltpu.roll(u, 1, 0))
    prev2 = jnp.where(pos == 0, c2, jnp.where(pos == 1, c1, pltpu.roll(u, 2, 0)))
    mix_ref[...] = _conv_epilogue(u, prev1, prev2, bg, z, cw_ref, cb_ref).astype(BF16)
    tail_ref[...] = u.reshape(nb, t, tn)[:, t - V7X_SUBLANES:, :]


def _conv_sample(h, w_blk, conv_w, conv_b, cache_conv, *, d_conv, nb, t, tn):
    m, d = h.shape
    assert m == nb * t and d_conv % tn == 0 and t % V7X_SUBLANES == 0 and t >= CONV_WIDTH - 1
    blocks = [((m, d), BF16), ((d, N_PARTS * tn), BF16), ((m, tn), BF16)] + [((m, tn), F32)] * 10
    return pl.pallas_call(
        functools.partial(_conv_sample_kernel, nb=nb, t=t, tn=tn),
        grid=(d_conv // tn,),
        in_specs=[pl.BlockSpec((m, d), lambda j: (0, 0)),
                  pl.BlockSpec((d, N_PARTS * tn), lambda j: (0, j)),
                  pl.BlockSpec((CONV_WIDTH, tn), lambda j: (0, j)),
                  pl.BlockSpec((1, tn), lambda j: (0, j)),
                  pl.BlockSpec((nb, CONV_WIDTH - 1, tn), lambda j: (0, 0, j))],
        out_specs=[pl.BlockSpec((m, tn), lambda j: (0, j)),
                   pl.BlockSpec((nb, V7X_SUBLANES, tn), lambda j: (0, 0, j))],
        out_shape=[jax.ShapeDtypeStruct((m, d_conv), BF16),
                   jax.ShapeDtypeStruct((nb, V7X_SUBLANES, d_conv), F32)],
        compiler_params=_params(("arbitrary",), blocks),
        name="conv_sample",
    )(h, w_blk, conv_w, conv_b, cache_conv)


def _proj_kernel(h_ref, w_ref, *o_refs, parts, tn):
    h = h_ref[...]
    for p, o_ref in zip(parts, o_refs):
        o_ref[...] = _mm(h, w_ref[:, p * tn:(p + 1) * tn]).astype(o_ref.dtype)


def _proj(h, w_blk, *, block0, n_blocks, parts, out_dtypes, tm, tn, name):
    m, d = h.shape
    assert m % tm == 0 and len(parts) == len(out_dtypes)
    blocks = ([((tm, d), BF16), ((d, N_PARTS * tn), BF16)]
              + [((tm, tn), dt) for dt in out_dtypes] + [((tm, tn), F32)] * len(parts))
    return pl.pallas_call(
        functools.partial(_proj_kernel, parts=parts, tn=tn),
        grid=(m // tm, n_blocks),
        in_specs=[pl.BlockSpec((tm, d), lambda i, j: (i, 0)),
                  pl.BlockSpec((d, N_PARTS * tn), lambda i, j: (0, block0 + j))],
        out_specs=[pl.BlockSpec((tm, tn), lambda i, j: (i, j)) for _ in out_dtypes],
        out_shape=[jax.ShapeDtypeStruct((m, n_blocks * tn), dt) for dt in out_dtypes],
        compiler_params=_params(("arbitrary", "arbitrary"), blocks),
        name=name,
    )(h, w_blk)


def _bias_seq(rel_bias, offset, n):
    rev = rel_bias[:, ::-1]
    left = offset - MAX_REL
    assert left >= 0
    right = max(n - left - rev.shape[1], 0)
    return jnp.pad(rev, ((0, 0), (left, right)), mode="edge")[:, :n]


def _bias_table_kernel(seq_ref, o_ref, *, rows, cols, lane0, band_cols):
    width = seq_ref.shape[-1]
    seq = jnp.broadcast_to(seq_ref[0], (rows, width))
    table = pltpu.roll(seq, width - lane0, 1, stride=1, stride_axis=0)[:, :cols] * LOG2E
    if band_cols is not None:
        qc = lax.broadcasted_iota(jnp.int32, (rows, cols), 0) // CHUNK
        kc = lax.broadcasted_iota(jnp.int32, (rows, cols), 1) // CHUNK - band_cols // CHUNK
        table = jnp.where((kc <= qc) & (kc >= qc - N_PAST_CHUNKS), table, -jnp.inf)
    o_ref[0] = table


def _bias_table(rel_bias, *, rows, cols, rel0, band_cols=None):
    nh = rel_bias.shape[0]
    lane0 = V7X_LANES * (-(-rows // V7X_LANES))
    width = 1 << (lane0 + cols - 1).bit_length()
    seq = _bias_seq(rel_bias, rel0 + lane0, width)
    blocks = [((1, width), F32), ((rows, cols), F32), ((rows, width), F32), ((rows, width), F32)]
    return pl.pallas_call(
        functools.partial(_bias_table_kernel, rows=rows, cols=cols, lane0=lane0, band_cols=band_cols),
        grid=(nh,),
        in_specs=[pl.BlockSpec((1, 1, width), lambda h: (h, 0, 0))],
        out_specs=pl.BlockSpec((1, rows, cols), lambda h: (h, 0, 0)),
        out_shape=jax.ShapeDtypeStruct((nh, rows, cols), F32),
        compiler_params=_params(("arbitrary",), blocks),
        name="bias_table",
    )(seq.reshape(nh, 1, width))


def _attn_prompt_kernel(q_ref, k0_ref, k1_ref, k2_ref, v0_ref, v1_ref, v2_ref, z_ref, bias_ref, o_ref,
                        *, heads, tq):
    b = pl.program_id(1)
    nk = 3 * tq
    dims = (((1,), (1,)), ((), ()))

    def run(mask_keys):
        if mask_keys:
            kc = lax.broadcasted_iota(jnp.int32, (1, nk), 1) // CHUNK
            valid = kc >= (2 - b) * (tq // CHUNK)
        for hh in range(heads):
            sl = slice(hh * HEAD_DIM, (hh + 1) * HEAD_DIM)
            k = jnp.concatenate([k0_ref[:, sl], k1_ref[:, sl], k2_ref[:, sl]], axis=0)
            v = jnp.concatenate([v0_ref[:, sl], v1_ref[:, sl], v2_ref[:, sl]], axis=0)
            s = lax.dot_general(q_ref[:, sl], k, dims, preferred_element_type=F32)
            s = s * (ATTN_SCALE * LOG2E) + bias_ref[hh]
            if mask_keys:
                s = jnp.where(valid, s, -jnp.inf)
            m = jnp.max(s, axis=-1, keepdims=True)
            p = jnp.exp2(s - m)
            l = jnp.sum(p, axis=-1, keepdims=True)
            o = _mm(p.astype(BF16), v) / l
            o_ref[:, sl] = (o * _silu(z_ref[:, sl])).astype(BF16)

    pl.when(b < 2)(lambda: run(True))
    pl.when(b >= 2)(lambda: run(False))


def _attn_prompt(q, k, v, z, bias, *, tq, heads):
    s, da = q.shape
    nh = da // HEAD_DIM
    wcols = heads * HEAD_DIM
    assert s % tq == 0 and nh % heads == 0 and tq % CHUNK == 0 and 2 * tq >= BAND_PAST
    kspec = lambda back: pl.BlockSpec((tq, wcols), lambda g, b, back=back: (jnp.maximum(b - back, 0), g))
    blocks = ([((tq, wcols), BF16)] * 8 + [((tq, wcols), F32)] + [((heads, tq, 3 * tq), F32)]
              + [((tq, 3 * tq), F32)] * 8)
    return pl.pallas_call(
        functools.partial(_attn_prompt_kernel, heads=heads, tq=tq),
        grid=(nh // heads, s // tq),
        in_specs=[pl.BlockSpec((tq, wcols), lambda g, b: (b, g)),
                  kspec(2), kspec(1), kspec(0), kspec(2), kspec(1), kspec(0),
                  pl.BlockSpec((tq, wcols), lambda g, b: (b, g)),
                  pl.BlockSpec((heads, tq, 3 * tq), lambda g, b: (g, 0, 0))],
        out_specs=pl.BlockSpec((tq, wcols), lambda g, b: (b, g)),
        out_shape=jax.ShapeDtypeStruct((s, da), BF16),
        compiler_params=_params(("arbitrary", "arbitrary"), blocks),
        name="attn_prompt",
    )(q, k, k, k, v, v, v, z, bias)


def _attn_sample_kernel(q_ref, kn_ref, vn_ref, ck_ref, cv_ref, z_ref, bias_ref, o_ref, *, nh, r, t):
    dims = (((1,), (1,)), ((), ()))
    for bb in range(q_ref.shape[0]):
        for hh in range(nh):
            sl = slice(hh * HEAD_DIM, (hh + 1) * HEAD_DIM)
            head_rows = pl.ds(hh, r, stride=nh)
            q = q_ref[bb, :, sl].astype(BF16)
            sc = lax.dot_general(q, ck_ref[bb, head_rows, :].astype(BF16), dims, preferred_element_type=F32)
            sn = lax.dot_general(q, kn_ref[bb, :, sl].astype(BF16), dims, preferred_element_type=F32)
            sc = sc * (ATTN_SCALE * LOG2E) + bias_ref[hh, :, 0:r]
            sn = sn * (ATTN_SCALE * LOG2E) + bias_ref[hh, :, r:r + t]
            m = jnp.maximum(jnp.max(sc, axis=-1, keepdims=True), jnp.max(sn, axis=-1, keepdims=True))
            pc = jnp.exp2(sc - m)
            pn = jnp.exp2(sn - m)
            l = jnp.sum(pc, axis=-1, keepdims=True) + jnp.sum(pn, axis=-1, keepdims=True)
            o = _mm(pc.astype(BF16), cv_ref[bb, head_rows, :].astype(BF16))
            o = o + _mm(pn.astype(BF16), vn_ref[bb, :, sl].astype(BF16))
            o_ref[bb, :, sl] = ((o / l) * _silu(z_ref[bb, :, sl])).astype(BF16)


def _attn_sample(q, kn, vn, cache_k, cache_v, z, bias, *, nbs):
    nb, t, da = q.shape
    nh = da // HEAD_DIM
    r = cache_k.shape[1] // nh
    assert r % V7X_LANES == 0 and bias.shape[2] >= r + t and nb % nbs == 0
    new = pl.BlockSpec((nbs, t, da), lambda b: (b, 0, 0))
    old = pl.BlockSpec((nbs, r * nh, HEAD_DIM), lambda b: (b, 0, 0))
    blocks = ([((nbs, t, da), F32)] * 5 + [((nbs, r, da), F32)] * 2 + [(bias.shape, F32)]
              + [((nbs, r, da), BF16)] * 2)
    return pl.pallas_call(
        functools.partial(_attn_sample_kernel, nh=nh, r=r, t=t),
        grid=(nb // nbs,),
        in_specs=[new, new, new, old, old, new, pl.BlockSpec(bias.shape, lambda b: (0, 0, 0))],
        out_specs=new,
        out_shape=jax.ShapeDtypeStruct((nb, t, da), BF16),
        compiler_params=_params(("arbitrary",), blocks),
        name="attn_sample",
    )(q, kn, vn, cache_k, cache_v, z, bias)


def _out_kernel(x_ref, ma_ref, mb_ref, wa_ref, wb_ref, gate_ref, gf_ref, y_ref, ss_ref, *, nj, tn):
    j = pl.program_id(1)
    acc = _mm(ma_ref[...], wa_ref[...]) + _mm(mb_ref[...], wb_ref[...])
    nb, t, _ = x_ref.shape
    res = x_ref[...] + gate_ref[...] * acc.reshape(nb, t, tn)
    y_ref[:, :, pl.ds(pl.multiple_of(j * tn, tn), tn)] = res
    part = jnp.sum(res * res, axis=-1, keepdims=True)

    @pl.when(j == 0)
    def _():
        ss_ref[...] = part

    @pl.when(j > 0)
    def _():
        ss_ref[...] += part

    @pl.when(j == nj - 1)
    def _():
        inv = lax.rsqrt(ss_ref[...] * (1.0 / (nj * tn)) + NORM_EPS)
        for n in range(nj):
            cols = slice(n * tn, (n + 1) * tn)
            y_ref[:, :, cols] = (y_ref[:, :, cols] * inv) * gf_ref[:, :, cols]


def _out(x, mix_a, mix_b, w_out, gate, g_final, *, nb, t, tn):
    b, s, d = x.shape
    half = mix_a.shape[1]
    tm = nb * t
    assert b % nb == 0 and s % t == 0 and d % tn == 0 and w_out.shape[0] == 2 * half
    nj = d // tn
    nt = s // t
    assert nb == 1 or nt == 1, "a row tile must be contiguous in the flattened (B*S) mixed rows"
    blocks = ([((nb, t, tn), F32)] + [((tm, half), BF16)] * 2 + [((half, tn), BF16)] * 2
              + [((nb, t, d), F32)] + [((tm, tn), F32)] * 4)
    return pl.pallas_call(
        functools.partial(_out_kernel, nj=nj, tn=tn),
        grid=((b // nb) * nt, nj),
        in_specs=[pl.BlockSpec((nb, t, tn), lambda i, j: (i // nt, i % nt, j)),
                  pl.BlockSpec((tm, half), lambda i, j: (i, 0)),
                  pl.BlockSpec((tm, half), lambda i, j: (i, 0)),
                  pl.BlockSpec((half, tn), lambda i, j: (0, j)),
                  pl.BlockSpec((half, tn), lambda i, j: (1, j)),
                  pl.BlockSpec((nb, 1, tn), lambda i, j: (i // nt, 0, j)),
                  pl.BlockSpec((1, 1, d), lambda i, j: (0, 0, 0))],
        out_specs=pl.BlockSpec((nb, t, d), lambda i, j: (i // nt, i % nt, 0)),
        out_shape=jax.ShapeDtypeStruct((b, s, d), F32),
        scratch_shapes=[pltpu.VMEM((nb, t, 1), F32)],
        compiler_params=_params(("arbitrary", "arbitrary"), blocks),
        name="out_proj",
    )(x, mix_a, mix_b, w_out, w_out, gate, g_final)


def kernel(x_prompt, x_sample, cache_k, cache_v, cache_conv, c_prompt, c_sample,
           g_norm, w_ada, b_ada, w_in, conv_w, conv_b, rel_bias, w_out, g_final):
    depth = g_norm.shape[0]
    assert depth == 1, "single-layer trunk"
    bp, sp, d = x_prompt.shape
    bs, ts, _ = x_sample.shape
    assert bp == 1
    d_conv = conv_w.shape[-1]
    d_attn = w_out.shape[1] - d_conv
    nh = d_attn // HEAD_DIM
    r = cache_k.shape[2]
    rows_kept = min(BAND_PAST, sp)

    n_c = bp + bs
    pad = (-n_c) % V7X_SUBLANES
    c_all = jnp.concatenate([c_prompt, c_sample, jnp.zeros((pad, d), F32)], axis=0)
    mod = _ada(c_all, w_ada[0], b_ada)
    shift, scale, gate = (mod[:n_c, i * d:(i + 1) * d].reshape(n_c, 1, d) for i in range(3))

    tn = 256
    nbc, nba = d_conv // tn, d_attn // tn
    w_blk = _blocked_weight(w_in[0], d_conv, d_attn, tn)
    w_out_b = w_out[0].astype(BF16)
    g3 = g_norm.reshape(1, 1, d)
    gf3 = g_final.reshape(1, 1, d)

    tm, tq, halves = 1024, 256, 2
    h_first = _prep(x_prompt, g3, scale[:bp], shift[:bp], nb=1, t=512, s0=0, s_len=tm).reshape(tm, d)
    mix_conv_p, u_tail_p, qp, kp, vp, zp, k_keep, v_keep = _inproj_prompt(
        x_prompt.reshape(sp, d), g_norm, scale[0], shift[0], h_first, w_blk, conv_w[0], conv_b,
        d_conv=d_conv, d_attn=d_attn, tm=tm, tn=tn, halves=halves, n_new=rows_kept)
    bias_p = _bias_table(rel_bias[0], rows=tq, cols=3 * tq, rel0=2 * tq, band_cols=2 * tq)
    mix_attn_p = _attn_prompt(qp, kp, vp, zp, bias_p, tq=tq, heads=16)
    y_prompt = _out(x_prompt, mix_conv_p, mix_attn_p, w_out_b, gate[:bp], gf3, nb=1, t=512, tn=1024)

    hs = _prep(x_sample, g3, scale[bp:], shift[bp:], nb=8, t=ts).reshape(bs * ts, d)
    mix_conv_s, u_tail_s = _conv_sample(hs, w_blk, conv_w[0], conv_b, cache_conv[0],
                                        d_conv=d_conv, nb=bs, t=ts, tn=tn)
    qs, ks, vs, zs = _proj(hs, w_blk, block0=nbc, n_blocks=nba, parts=(0, 1, 2, 3), out_dtypes=[F32] * 4,
                           tm=bs * ts, tn=tn, name="qkvz_sample")
    bias_s = _bias_table(rel_bias[0], rows=ts, cols=r + V7X_LANES * (-(-ts // V7X_LANES)), rel0=r)
    to3 = lambda a: a.reshape(bs, ts, d_attn)
    mix_attn_s = _attn_sample(to3(qs), to3(ks), to3(vs), cache_k[0].reshape(bs, r * nh, HEAD_DIM),
                              cache_v[0].reshape(bs, r * nh, HEAD_DIM), to3(zs), bias_s, nbs=1)
    y_sample = _out(x_sample, mix_conv_s, mix_attn_s.reshape(bs * ts, d_attn), w_out_b, gate[bp:], gf3,
                    nb=16, t=ts, tn=1024)

    keep = CONV_WIDTH - 1
    new_k_prompt = k_keep[-1].reshape(1, bp, rows_kept, nh, HEAD_DIM)
    new_v_prompt = v_keep[-1].reshape(1, bp, rows_kept, nh, HEAD_DIM)
    new_conv_prompt = u_tail_p[-1, V7X_SUBLANES - keep:, :].reshape(1, bp, keep, d_conv)
    new_k_sample = ks.reshape(1, bs, ts, nh, HEAD_DIM)
    new_v_sample = vs.reshape(1, bs, ts, nh, HEAD_DIM)
    new_conv_sample = u_tail_s[:, V7X_SUBLANES - keep:, :].reshape(1, bs, keep, d_conv)
    return (y_prompt, y_sample, new_k_prompt, new_v_prompt, new_conv_prompt,
            new_k_sample, new_v_sample, new_conv_sample)
```

```python
import functools
import math

import jax
import jax.numpy as jnp
from jax import lax
from jax.experimental import pallas as pl
from jax.experimental.pallas import tpu as pltpu

CHUNK = 64
N_PAST_CHUNKS = 8
BAND_PAST = N_PAST_CHUNKS * CHUNK
HEAD_DIM = 128
CONV_WIDTH = 3
MAX_REL = 256
NORM_EPS = 1e-6
ATTN_SCALE = HEAD_DIM ** -0.5
LOG2E = math.log2(math.e)
N_PARTS = 4

V7X_LANES = 128
V7X_SUBLANES = 8
V7X_VMEM_BYTES = 64 * 1024 * 1024

F32 = jnp.float32
BF16 = jnp.bfloat16


def _nbytes(shape, dtype):
    n = 1
    for s in shape:
        n *= s
    return n * jnp.dtype(dtype).itemsize


def _params(semantics, blocks, extra_bytes=0):
    need = 2 * sum(_nbytes(s, d) for s, d in blocks) + extra_bytes + 16 * 1024 * 1024
    limit = int(min(need, V7X_VMEM_BYTES - 4 * 1024 * 1024))
    return pltpu.CompilerParams(dimension_semantics=semantics, vmem_limit_bytes=limit)


def _silu(z):
    return z * jax.nn.sigmoid(z)


def _mm(a, b):
    return jnp.dot(a, b, preferred_element_type=F32)


def _ada_kernel(c_ref, w_ref, b_ref, o_ref):
    o_ref[...] = _mm(c_ref[...].astype(BF16), w_ref[...].astype(BF16)) + b_ref[...]


def _ada(c, w_ada, b_ada, *, tn=1024):
    r, d = c.shape
    n = w_ada.shape[1]
    assert n % tn == 0 and r % V7X_SUBLANES == 0
    blocks = [((r, d), F32), ((d, tn), F32), ((1, tn), F32), ((r, tn), F32), ((d, tn), BF16)]
    return pl.pallas_call(
        _ada_kernel,
        grid=(n // tn,),
        in_specs=[pl.BlockSpec((r, d), lambda j: (0, 0)),
                  pl.BlockSpec((d, tn), lambda j: (0, j)),
                  pl.BlockSpec((1, tn), lambda j: (0, j))],
        out_specs=pl.BlockSpec((r, tn), lambda j: (0, j)),
        out_shape=jax.ShapeDtypeStruct((r, n), F32),
        compiler_params=_params(("arbitrary",), blocks),
        name="ada",
    )(c, w_ada, b_ada)


def _modulated_norm(x, g, scale, shift):
    ms = jnp.mean(x * x, axis=-1, keepdims=True)
    xn = x * lax.rsqrt(ms + NORM_EPS)
    return (xn * g * (1.0 + scale) + shift).astype(BF16)


def _prep_kernel(x_ref, g_ref, scale_ref, shift_ref, h_ref):
    h_ref[...] = _modulated_norm(x_ref[...], g_ref[...], scale_ref[...], shift_ref[...])


def _prep(x, g, scale, shift, *, nb, t, s0=0, s_len=None):
    b, s, d = x.shape
    s_len = s if s_len is None else s_len
    assert b % nb == 0 and s_len % t == 0 and s0 % t == 0 and s0 + s_len <= s
    j0 = s0 // t
    blocks = [((nb, t, d), F32), ((nb, t, d), BF16), ((nb, t, d), F32)]
    return pl.pallas_call(
        _prep_kernel,
        grid=(b // nb, s_len // t),
        in_specs=[pl.BlockSpec((nb, t, d), lambda i, j: (i, j + j0, 0)),
                  pl.BlockSpec((1, 1, d), lambda i, j: (0, 0, 0)),
                  pl.BlockSpec((nb, 1, d), lambda i, j: (i, 0, 0)),
                  pl.BlockSpec((nb, 1, d), lambda i, j: (i, 0, 0))],
        out_specs=pl.BlockSpec((nb, t, d), lambda i, j: (i, j, 0)),
        out_shape=jax.ShapeDtypeStruct((b, s_len, d), BF16),
        compiler_params=_params(("arbitrary", "arbitrary"), blocks),
        name="prep",
    )(x, g, scale, shift)


def _blocked_weight_kernel(*refs, tn):
    o_ref = refs[-1]
    for p, w_ref in enumerate(refs[:-1]):
        o_ref[:, p * tn:(p + 1) * tn] = w_ref[...].astype(BF16)


def _blocked_weight(w_in, d_conv, d_attn, tn, *, tr=2048):
    d, n = w_in.shape
    nbc, nba = d_conv // tn, d_attn // tn
    assert d % tr == 0 and n == N_PARTS * (d_conv + d_attn) and d_conv % tn == 0 and d_attn % tn == 0

    def wspec(p):
        src = lambda j: jnp.where(j < nbc, p * nbc + j, N_PARTS * nbc + p * nba + j - nbc)
        return pl.BlockSpec((tr, tn), lambda i, j: (i, src(j)))

    blocks = [((tr, tn), F32)] * N_PARTS + [((tr, N_PARTS * tn), BF16)]
    return pl.pallas_call(
        functools.partial(_blocked_weight_kernel, tn=tn),
        grid=(d // tr, nbc + nba),
        in_specs=[wspec(p) for p in range(N_PARTS)],
        out_specs=pl.BlockSpec((None, tr, N_PARTS * tn), lambda i, j: (j, i, 0)),
        out_shape=jax.ShapeDtypeStruct((nbc + nba, d, N_PARTS * tn), BF16),
        compiler_params=_params(("arbitrary", "arbitrary"), blocks),
        name="blocked_weight",
    )(*([w_in] * N_PARTS))


def _weight_block_spec(d, tn, index_map):
    return pl.BlockSpec((None, d, N_PARTS * tn), lambda *ids: (index_map(*ids), 0, 0))


def _conv_epilogue(u, prev1, prev2, bg, z, cwb_ref):
    conv = cwb_ref[CONV_WIDTH:CONV_WIDTH + 1, :] + cwb_ref[0:1, :] * prev2
    conv = conv + cwb_ref[1:2, :] * prev1
    conv = conv + cwb_ref[2:3, :] * u
    return (bg * conv) * _silu(z)


def _inproj_prompt_kernel(x_ref, norm_ref, h0_ref, w_ref, cwb_ref,
                          mix_ref, tail_ref, q_ref, k_ref, v_ref, z_ref, k_new_ref, v_new_ref, h_buf, halo_ref,
                          *, nbc, halves, tn):
    i = pl.program_id(0)
    j = pl.program_id(1)
    slot = lax.rem(i, 2)
    sm = x_ref.shape[0]
    hm = h_buf.shape[1] // halves

    @pl.when((i == 0) & (j == 0))
    def _():
        pltpu.sync_copy(h0_ref, h_buf.at[0])

    def prep_next_slab():
        h = _modulated_norm(x_ref[...], norm_ref[0:1, :], norm_ref[1:2, :], norm_ref[2:3, :])
        h_buf[1 - slot, pl.ds(pl.multiple_of(j * sm, sm), sm), :] = h

    def part(h, p):
        return _mm(h, w_ref[:, p * tn:(p + 1) * tn])

    def sub_tiles():
        for s in range(halves):
            if s == halves - 1:
                prep_next_slab()
            yield slice(s * hm, (s + 1) * hm)

    @pl.when(j < nbc)
    def _conv_group():
        @pl.when(i == 0)
        def _():
            halo_ref[j] = jnp.zeros(halo_ref.shape[1:], F32)

        last8 = halo_ref[j]
        for rows in sub_tiles():
            h = h_buf[slot, rows, :]
            xin, bg, cg, z = part(h, 0), part(h, 1), part(h, 2), part(h, 3)
            u = cg * xin
            last1 = last8[V7X_SUBLANES - 1:V7X_SUBLANES, :]
            last2 = last8[V7X_SUBLANES - 2:V7X_SUBLANES - 1, :]
            row = lax.broadcasted_iota(jnp.int32, u.shape, 0)
            prev1 = jnp.where(row == 0, last1, pltpu.roll(u, 1, 0))
            prev2 = jnp.where(row == 0, last2, jnp.where(row == 1, last1, pltpu.roll(u, 2, 0)))
            mix_ref[rows, :] = _conv_epilogue(u, prev1, prev2, bg, z, cwb_ref).astype(BF16)
            last8 = u[hm - V7X_SUBLANES:, :]
        halo_ref[j] = last8
        tail_ref[0] = last8

    @pl.when(j >= nbc)
    def _attn_group():
        first_new = h_buf.shape[1] - k_new_ref.shape[1]
        for rows in sub_tiles():
            h = h_buf[slot, rows, :]
            k, v = part(h, 1), part(h, 2)
            q_ref[rows, :] = part(h, 0).astype(BF16)
            k_ref[rows, :] = k.astype(BF16)
            v_ref[rows, :] = v.astype(BF16)
            z_ref[rows, :] = part(h, 3)
            if rows.start >= first_new:
                new_rows = slice(rows.start - first_new, rows.stop - first_new)
                k_new_ref[0, new_rows, :] = k
                v_new_ref[0, new_rows, :] = v


def _inproj_prompt(x, norm, h0, w_blk, conv_wb, *, d_conv, d_attn, tm, tn, halves, n_new):
    s, d = x.shape
    assert s % tm == 0 and d_conv % tn == 0 and d_attn % tn == 0 and (tm // halves) % V7X_SUBLANES == 0
    nbc, nba = d_conv // tn, d_attn // tn
    nsteps = nbc + nba
    nrt = s // tm
    hm = tm // halves
    sm = tm // nsteps
    assert tm % nsteps == 0 and sm % (2 * V7X_SUBLANES) == 0 and h0.shape == (tm, d)
    assert n_new % hm == 0 and n_new <= tm, "the kept K/V rows are whole sub-tiles of the last row tile"
    cj = lambda j: jnp.minimum(j, nbc - 1)
    aj = lambda j: jnp.maximum(j - nbc, 0)
    attn_spec = pl.BlockSpec((tm, tn), lambda i, j: (i, aj(j)))
    newest_spec = pl.BlockSpec((1, n_new, tn), lambda i, j: (i, 0, aj(j)))
    blocks = ([((sm, d), F32), ((d, N_PARTS * tn), BF16)] + [((tm, tn), BF16)] * 4 + [((tm, tn), F32)]
              + [((n_new, tn), F32)] * 2 + [((hm, tn), F32)] * 8)
    return pl.pallas_call(
        functools.partial(_inproj_prompt_kernel, nbc=nbc, halves=halves, tn=tn),
        grid=(nrt, nsteps),
        in_specs=[pl.BlockSpec((sm, d), lambda i, j: (jnp.minimum(i + 1, nrt - 1) * nsteps + j, 0)),
                  pl.BlockSpec(norm.shape, lambda i, j: (0, 0)),
                  pl.BlockSpec(memory_space=pl.ANY),
                  _weight_block_spec(d, tn, lambda i, j: j),
                  pl.BlockSpec((CONV_WIDTH + 1, tn), lambda i, j: (0, cj(j)))],
        out_specs=[pl.BlockSpec((tm, tn), lambda i, j: (i, cj(j))),
                   pl.BlockSpec((1, V7X_SUBLANES, tn), lambda i, j: (i, 0, cj(j))),
                   attn_spec, attn_spec, attn_spec, attn_spec, newest_spec, newest_spec],
        out_shape=[jax.ShapeDtypeStruct((s, d_conv), BF16),
                   jax.ShapeDtypeStruct((nrt, V7X_SUBLANES, d_conv), F32),
                   jax.ShapeDtypeStruct((s, d_attn), BF16),
                   jax.ShapeDtypeStruct((s, d_attn), BF16),
                   jax.ShapeDtypeStruct((s, d_attn), BF16),
                   jax.ShapeDtypeStruct((s, d_attn), F32),
                   jax.ShapeDtypeStruct((nrt, n_new, d_attn), F32),
                   jax.ShapeDtypeStruct((nrt, n_new, d_attn), F32)],
        scratch_shapes=[pltpu.VMEM((2, tm, d), BF16), pltpu.VMEM((nbc, V7X_SUBLANES, tn), F32)],
        compiler_params=_params(("arbitrary", "arbitrary"), blocks, extra_bytes=_nbytes((2, tm, d), BF16)),
        name="inproj_prompt",
    )(x, norm, h0, w_blk, conv_wb)


def _conv_sample_kernel(h_ref, w_ref, cwb_ref, cache_ref, mix_ref, tail_ref, *, nb, t, tn):
    h = h_ref[...]
    xin, bg, cg, z = (_mm(h, w_ref[:, p * tn:(p + 1) * tn]) for p in range(N_PARTS))
    u = cg * xin
    cache = cache_ref[...]
    c2 = jnp.broadcast_to(cache[:, 0:1, :], (nb, t, tn)).reshape(nb * t, tn)
    c1 = jnp.broadcast_to(cache[:, 1:2, :], (nb, t, tn)).reshape(nb * t, tn)
    pos = lax.broadcasted_iota(jnp.int32, (nb, t, tn), 1).reshape(nb * t, tn)
    prev1 = jnp.where(pos == 0, c1, pltpu.roll(u, 1, 0))
    prev2 = jnp.where(pos == 0, c2, jnp.where(pos == 1, c1, pltpu.roll(u, 2, 0)))
    mix_ref[...] = _conv_epilogue(u, prev1, prev2, bg, z, cwb_ref).astype(BF16)
    tail_ref[...] = u.reshape(nb, t, tn)[:, t - V7X_SUBLANES:, :]


def _conv_sample(h, w_blk, conv_wb, cache_conv, *, d_conv, nb, t, tn):
    m, d = h.shape
    assert m == nb * t and d_conv % tn == 0 and t % V7X_SUBLANES == 0 and t >= CONV_WIDTH - 1
    blocks = [((m, d), BF16), ((d, N_PARTS * tn), BF16), ((m, tn), BF16)] + [((m, tn), F32)] * 10
    return pl.pallas_call(
        functools.partial(_conv_sample_kernel, nb=nb, t=t, tn=tn),
        grid=(d_conv // tn,),
        in_specs=[pl.BlockSpec((m, d), lambda j: (0, 0)),
                  _weight_block_spec(d, tn, lambda j: j),
                  pl.BlockSpec((CONV_WIDTH + 1, tn), lambda j: (0, j)),
                  pl.BlockSpec((nb, CONV_WIDTH - 1, tn), lambda j: (0, 0, j))],
        out_specs=[pl.BlockSpec((m, tn), lambda j: (0, j)),
                   pl.BlockSpec((nb, V7X_SUBLANES, tn), lambda j: (0, 0, j))],
        out_shape=[jax.ShapeDtypeStruct((m, d_conv), BF16),
                   jax.ShapeDtypeStruct((nb, V7X_SUBLANES, d_conv), F32)],
        compiler_params=_params(("arbitrary",), blocks),
        name="conv_sample",
    )(h, w_blk, conv_wb, cache_conv)


def _proj_kernel(h_ref, w_ref, *o_refs, parts, tn):
    h = h_ref[...]
    for p, o_ref in zip(parts, o_refs):
        o_ref[...] = _mm(h, w_ref[:, p * tn:(p + 1) * tn]).astype(o_ref.dtype)


def _proj(h, w_blk, *, block0, n_blocks, parts, out_dtypes, tm, tn, name):
    m, d = h.shape
    assert m % tm == 0 and len(parts) == len(out_dtypes)
    blocks = ([((tm, d), BF16), ((d, N_PARTS * tn), BF16)]
              + [((tm, tn), dt) for dt in out_dtypes] + [((tm, tn), F32)] * len(parts))
    return pl.pallas_call(
        functools.partial(_proj_kernel, parts=parts, tn=tn),
        grid=(m // tm, n_blocks),
        in_specs=[pl.BlockSpec((tm, d), lambda i, j: (i, 0)),
                  _weight_block_spec(d, tn, lambda i, j: block0 + j)],
        out_specs=[pl.BlockSpec((tm, tn), lambda i, j: (i, j)) for _ in out_dtypes],
        out_shape=[jax.ShapeDtypeStruct((m, n_blocks * tn), dt) for dt in out_dtypes],
        compiler_params=_params(("arbitrary", "arbitrary"), blocks),
        name=name,
    )(h, w_blk)


def _bias_seq(rel_bias, offset, n):
    rev = rel_bias[:, ::-1]
    left = offset - MAX_REL
    assert left >= 0
    right = max(n - left - rev.shape[1], 0)
    return jnp.pad(rev, ((0, 0), (left, right)), mode="edge")[:, :n]


def _bias_table_kernel(seq_ref, o_ref, *, rows, cols, lane0, band_cols):
    width = seq_ref.shape[-1]
    seq = jnp.broadcast_to(seq_ref[0], (rows, width))
    table = pltpu.roll(seq, width - lane0, 1, stride=1, stride_axis=0)[:, :cols] * LOG2E
    if band_cols is not None:
        qc = lax.broadcasted_iota(jnp.int32, (rows, cols), 0) // CHUNK
        kc = lax.broadcasted_iota(jnp.int32, (rows, cols), 1) // CHUNK - band_cols // CHUNK
        table = jnp.where((kc <= qc) & (kc >= qc - N_PAST_CHUNKS), table, -jnp.inf)
    o_ref[0] = table


def _bias_table(rel_bias, *, rows, cols, rel0, band_cols=None):
    nh = rel_bias.shape[0]
    lane0 = V7X_LANES * (-(-rows // V7X_LANES))
    width = 1 << (lane0 + cols - 1).bit_length()
    seq = _bias_seq(rel_bias, rel0 + lane0, width)
    blocks = [((1, width), F32), ((rows, cols), F32), ((rows, width), F32), ((rows, width), F32)]
    return pl.pallas_call(
        functools.partial(_bias_table_kernel, rows=rows, cols=cols, lane0=lane0, band_cols=band_cols),
        grid=(nh,),
        in_specs=[pl.BlockSpec((1, 1, width), lambda h: (h, 0, 0))],
        out_specs=pl.BlockSpec((1, rows, cols), lambda h: (h, 0, 0)),
        out_shape=jax.ShapeDtypeStruct((nh, rows, cols), F32),
        compiler_params=_params(("arbitrary",), blocks),
        name="bias_table",
    )(seq.reshape(nh, 1, width))


def _attn_prompt_kernel(q_ref, k0_ref, k1_ref, k2_ref, v0_ref, v1_ref, v2_ref, z_ref, bias_ref, o_ref,
                        *, heads, tq):
    b = pl.program_id(1)
    nk = 3 * tq
    dims = (((1,), (1,)), ((), ()))

    def run(mask_keys):
        if mask_keys:
            kc = lax.broadcasted_iota(jnp.int32, (1, nk), 1) // CHUNK
            valid = kc >= (2 - b) * (tq // CHUNK)
        for hh in range(heads):
            sl = slice(hh * HEAD_DIM, (hh + 1) * HEAD_DIM)
            k = jnp.concatenate([k0_ref[:, sl], k1_ref[:, sl], k2_ref[:, sl]], axis=0)
            v = jnp.concatenate([v0_ref[:, sl], v1_ref[:, sl], v2_ref[:, sl]], axis=0)
            s = lax.dot_general(q_ref[:, sl], k, dims, preferred_element_type=F32)
            s = s * (ATTN_SCALE * LOG2E) + bias_ref[hh]
            if mask_keys:
                s = jnp.where(valid, s, -jnp.inf)
            m = jnp.max(s, axis=-1, keepdims=True)
            p = jnp.exp2(s - m)
            l = jnp.sum(p, axis=-1, keepdims=True)
            o = _mm(p.astype(BF16), v) / l
            o_ref[:, sl] = (o * _silu(z_ref[:, sl])).astype(BF16)

    pl.when(b < 2)(lambda: run(True))
    pl.when(b >= 2)(lambda: run(False))


def _attn_prompt(q, k, v, z, bias, *, tq, heads):
    s, da = q.shape
    nh = da // HEAD_DIM
    wcols = heads * HEAD_DIM
    assert s % tq == 0 and nh % heads == 0 and tq % CHUNK == 0 and 2 * tq >= BAND_PAST
    kspec = lambda back: pl.BlockSpec((tq, wcols), lambda g, b, back=back: (jnp.maximum(b - back, 0), g))
    blocks = ([((tq, wcols), BF16)] * 8 + [((tq, wcols), F32)] + [((heads, tq, 3 * tq), F32)]
              + [((tq, 3 * tq), F32)] * 8)
    return pl.pallas_call(
        functools.partial(_attn_prompt_kernel, heads=heads, tq=tq),
        grid=(nh // heads, s // tq),
        in_specs=[pl.BlockSpec((tq, wcols), lambda g, b: (b, g)),
                  kspec(2), kspec(1), kspec(0), kspec(2), kspec(1), kspec(0),
                  pl.BlockSpec((tq, wcols), lambda g, b: (b, g)),
                  pl.BlockSpec((heads, tq, 3 * tq), lambda g, b: (g, 0, 0))],
        out_specs=pl.BlockSpec((tq, wcols), lambda g, b: (b, g)),
        out_shape=jax.ShapeDtypeStruct((s, da), BF16),
        compiler_params=_params(("arbitrary", "arbitrary"), blocks),
        name="attn_prompt",
    )(q, k, k, k, v, v, v, z, bias)


def _attn_sample_kernel(q_ref, kn_ref, vn_ref, ck_ref, cv_ref, z_ref, bias_ref, o_ref, *, nh, r, t):
    dims = (((1,), (1,)), ((), ()))
    for bb in range(q_ref.shape[0]):
        for hh in range(nh):
            sl = slice(hh * HEAD_DIM, (hh + 1) * HEAD_DIM)
            head_rows = pl.ds(hh, r, stride=nh)
            q = q_ref[bb, :, sl].astype(BF16)
            sc = lax.dot_general(q, ck_ref[bb, head_rows, :].astype(BF16), dims, preferred_element_type=F32)
            sn = lax.dot_general(q, kn_ref[bb, :, sl].astype(BF16), dims, preferred_element_type=F32)
            sc = sc * (ATTN_SCALE * LOG2E) + bias_ref[hh, :, 0:r]
            sn = sn * (ATTN_SCALE * LOG2E) + bias_ref[hh, :, r:r + t]
            m = jnp.maximum(jnp.max(sc, axis=-1, keepdims=True), jnp.max(sn, axis=-1, keepdims=True))
            pc = jnp.exp2(sc - m)
            pn = jnp.exp2(sn - m)
            l = jnp.sum(pc, axis=-1, keepdims=True) + jnp.sum(pn, axis=-1, keepdims=True)
            o = _mm(pc.astype(BF16), cv_ref[bb, head_rows, :].astype(BF16))
            o = o + _mm(pn.astype(BF16), vn_ref[bb, :, sl].astype(BF16))
            o_ref[bb, :, sl] = ((o / l) * _silu(z_ref[bb, :, sl])).astype(BF16)


def _attn_sample(q, kn, vn, cache_k, cache_v, z, bias, *, nbs):
    nb, t, da = q.shape
    nh = da // HEAD_DIM
    r = cache_k.shape[1] // nh
    assert r % V7X_LANES == 0 and bias.shape[2] >= r + t and nb % nbs == 0
    new = pl.BlockSpec((nbs, t, da), lambda b: (b, 0, 0))
    old = pl.BlockSpec((nbs, r * nh, HEAD_DIM), lambda b: (b, 0, 0))
    blocks = ([((nbs, t, da), F32)] * 5 + [((nbs, r, da), F32)] * 2 + [(bias.shape, F32)]
              + [((nbs, r, da), BF16)] * 2)
    return pl.pallas_call(
        functools.partial(_attn_sample_kernel, nh=nh, r=r, t=t),
        grid=(nb // nbs,),
        in_specs=[new, new, new, old, old, new, pl.BlockSpec(bias.shape, lambda b: (0, 0, 0))],
        out_specs=new,
        out_shape=jax.ShapeDtypeStruct((nb, t, da), BF16),
        compiler_params=_params(("arbitrary",), blocks),
        name="attn_sample",
    )(q, kn, vn, cache_k, cache_v, z, bias)


def _out_kernel(x_ref, ma_ref, mb_ref, wa_ref, wb_ref, gate_ref, gf_ref, y_ref, ss_ref, *, nj, tn):
    j = pl.program_id(1)
    acc = _mm(ma_ref[...], wa_ref[...]) + _mm(mb_ref[...], wb_ref[...])
    nb, t, _ = x_ref.shape
    res = x_ref[...] + gate_ref[...] * acc.reshape(nb, t, tn)
    y_ref[:, :, pl.ds(pl.multiple_of(j * tn, tn), tn)] = res
    part = jnp.sum(res * res, axis=-1, keepdims=True)

    @pl.when(j == 0)
    def _():
        ss_ref[...] = part

    @pl.when(j > 0)
    def _():
        ss_ref[...] += part

    @pl.when(j == nj - 1)
    def _():
        inv = lax.rsqrt(ss_ref[...] * (1.0 / (nj * tn)) + NORM_EPS)
        for n in range(nj):
            cols = slice(n * tn, (n + 1) * tn)
            y_ref[:, :, cols] = (y_ref[:, :, cols] * inv) * gf_ref[:, :, cols]


def _out(x, mix_a, mix_b, w_out, gate, g_final, *, nb, t, tn):
    b, s, d = x.shape
    half = mix_a.shape[1]
    tm = nb * t
    assert b % nb == 0 and s % t == 0 and d % tn == 0 and w_out.shape[0] == 2 * half
    nj = d // tn
    nt = s // t
    assert nb == 1 or nt == 1, "a row tile must be contiguous in the flattened (B*S) mixed rows"
    blocks = ([((nb, t, tn), F32)] + [((tm, half), BF16)] * 2 + [((half, tn), BF16)] * 2
              + [((nb, t, d), F32)] + [((tm, tn), F32)] * 4)
    return pl.pallas_call(
        functools.partial(_out_kernel, nj=nj, tn=tn),
        grid=((b // nb) * nt, nj),
        in_specs=[pl.BlockSpec((nb, t, tn), lambda i, j: (i // nt, i % nt, j)),
                  pl.BlockSpec((tm, half), lambda i, j: (i, 0)),
                  pl.BlockSpec((tm, half), lambda i, j: (i, 0)),
                  pl.BlockSpec((half, tn), lambda i, j: (0, j)),
                  pl.BlockSpec((half, tn), lambda i, j: (1, j)),
                  pl.BlockSpec((nb, 1, tn), lambda i, j: (i // nt, 0, j)),
                  pl.BlockSpec((1, 1, d), lambda i, j: (0, 0, 0))],
        out_specs=pl.BlockSpec((nb, t, d), lambda i, j: (i // nt, i % nt, 0)),
        out_shape=jax.ShapeDtypeStruct((b, s, d), F32),
        scratch_shapes=[pltpu.VMEM((nb, t, 1), F32)],
        compiler_params=_params(("arbitrary", "arbitrary"), blocks),
        name="out_proj",
    )(x, mix_a, mix_b, w_out, w_out, gate, g_final)


def kernel(x_prompt, x_sample, cache_k, cache_v, cache_conv, c_prompt, c_sample,
           g_norm, w_ada, b_ada, w_in, conv_w, conv_b, rel_bias, w_out, g_final):
    depth = g_norm.shape[0]
    assert depth == 1, "single-layer trunk"
    bp, sp, d = x_prompt.shape
    bs, ts, _ = x_sample.shape
    assert bp == 1
    d_conv = conv_w.shape[-1]
    d_attn = w_out.shape[1] - d_conv
    nh = d_attn // HEAD_DIM
    r = cache_k.shape[2]
    rows_kept = min(BAND_PAST, sp)

    n_c = bp + bs
    pad = (-n_c) % V7X_SUBLANES
    c_all = jnp.concatenate([c_prompt, c_sample, jnp.zeros((pad, d), F32)], axis=0)
    mod = _ada(c_all, w_ada[0], b_ada)
    shift, scale, gate = (mod[:n_c, i * d:(i + 1) * d].reshape(n_c, 1, d) for i in range(3))

    tn = 256
    nbc, nba = d_conv // tn, d_attn // tn
    w_blk = _blocked_weight(w_in[0], d_conv, d_attn, tn)
    w_out_b = w_out[0].astype(BF16)
    g3 = g_norm.reshape(1, 1, d)
    gf3 = g_final.reshape(1, 1, d)
    conv_wb = jnp.concatenate([conv_w[0], conv_b], axis=0)

    tm, tq, halves = 1024, 256, 2
    h_first = _prep(x_prompt, g3, scale[:bp], shift[:bp], nb=1, t=512, s0=0, s_len=tm).reshape(tm, d)
    norm_p = jnp.concatenate([g_norm, scale[0], shift[0]], axis=0)
    mix_conv_p, u_tail_p, qp, kp, vp, zp, k_keep, v_keep = _inproj_prompt(
        x_prompt.reshape(sp, d), norm_p, h_first, w_blk, conv_wb,
        d_conv=d_conv, d_attn=d_attn, tm=tm, tn=tn, halves=halves, n_new=rows_kept)
    bias_p = _bias_table(rel_bias[0], rows=tq, cols=3 * tq, rel0=2 * tq, band_cols=2 * tq)
    mix_attn_p = _attn_prompt(qp, kp, vp, zp, bias_p, tq=tq, heads=16)
    y_prompt = _out(x_prompt, mix_conv_p, mix_attn_p, w_out_b, gate[:bp], gf3, nb=1, t=512, tn=1024)

    hs = _prep(x_sample, g3, scale[bp:], shift[bp:], nb=8, t=ts).reshape(bs * ts, d)
    mix_conv_s, u_tail_s = _conv_sample(hs, w_blk, conv_wb, cache_conv[0], d_conv=d_conv, nb=bs, t=ts, tn=tn)
    qs, ks, vs, zs = _proj(hs, w_blk, block0=nbc, n_blocks=nba, parts=(0, 1, 2, 3), out_dtypes=[F32] * 4,
                           tm=bs * ts, tn=tn, name="qkvz_sample")
    bias_s = _bias_table(rel_bias[0], rows=ts, cols=r + V7X_LANES * (-(-ts // V7X_LANES)), rel0=r)
    to3 = lambda a: a.reshape(bs, ts, d_attn)
    mix_attn_s = _attn_sample(to3(qs), to3(ks), to3(vs), cache_k[0].reshape(bs, r * nh, HEAD_DIM),
                              cache_v[0].reshape(bs, r * nh, HEAD_DIM), to3(zs), bias_s, nbs=1)
    y_sample = _out(x_sample, mix_conv_s, mix_attn_s.reshape(bs * ts, d_attn), w_out_b, gate[bp:], gf3,
                    nb=16, t=ts, tn=1024)

    keep = CONV_WIDTH - 1
    new_k_prompt = k_keep[-1].reshape(1, bp, rows_kept, nh, HEAD_DIM)
    new_v_prompt = v_keep[-1].reshape(1, bp, rows_kept, nh, HEAD_DIM)
    new_conv_prompt = u_tail_p[-1, V7X_SUBLANES - keep:, :].reshape(1, bp, keep, d_conv)
    new_k_sample = ks.reshape(1, bs, ts, nh, HEAD_DIM)
    new_v_sample = vs.reshape(1, bs, ts, nh, HEAD_DIM)
    new_conv_sample = u_tail_s[:, V7X_SUBLANES - keep:, :].reshape(1, bs, keep, d_conv)
    return (y_prompt, y_sample, new_k_prompt, new_v_prompt, new_conv_prompt,
            new_k_sample, new_v_sample, new_conv_sample)
```

```python
import functools
import math

import jax
import jax.numpy as jnp
from jax import lax
from jax.experimental import pallas as pl
from jax.experimental.pallas import tpu as pltpu

CHUNK = 64
N_PAST_CHUNKS = 8
BAND_PAST = N_PAST_CHUNKS * CHUNK
HEAD_DIM = 128
CONV_WIDTH = 3
MAX_REL = 256
NORM_EPS = 1e-6
ATTN_SCALE = HEAD_DIM ** -0.5
LOG2E = math.log2(math.e)
N_PARTS = 4

V7X_LANES = 128
V7X_SUBLANES = 8
V7X_VMEM_BYTES = 64 * 1024 * 1024

F32 = jnp.float32
BF16 = jnp.bfloat16


def _nbytes(shape, dtype):
    n = 1
    for s in shape:
        n *= s
    return n * jnp.dtype(dtype).itemsize


def _params(semantics, blocks, extra_bytes=0):
    need = 2 * sum(_nbytes(s, d) for s, d in blocks) + extra_bytes + 16 * 1024 * 1024
    limit = int(min(need, V7X_VMEM_BYTES - 4 * 1024 * 1024))
    return pltpu.CompilerParams(dimension_semantics=semantics, vmem_limit_bytes=limit)


def _silu(z):
    return z * jax.nn.sigmoid(z)


def _mm(a, b):
    return jnp.dot(a, b, preferred_element_type=F32)


def _ada_kernel(c_ref, w_ref, b_ref, o_ref):
    o_ref[...] = _mm(c_ref[...].astype(BF16), w_ref[...].astype(BF16)) + b_ref[...]


def _ada(c, w_ada, b_ada, *, tn=512):
    r, d = c.shape
    n = w_ada.shape[1]
    assert n % tn == 0 and r % V7X_SUBLANES == 0
    blocks = [((r, d), F32), ((d, tn), F32), ((1, tn), F32), ((r, tn), F32), ((d, tn), BF16)]
    return pl.pallas_call(
        _ada_kernel,
        grid=(n // tn,),
        in_specs=[pl.BlockSpec((r, d), lambda j: (0, 0)),
                  pl.BlockSpec((d, tn), lambda j: (0, j)),
                  pl.BlockSpec((1, tn), lambda j: (0, j))],
        out_specs=pl.BlockSpec((r, tn), lambda j: (0, j)),
        out_shape=jax.ShapeDtypeStruct((r, n), F32),
        compiler_params=_params(("arbitrary",), blocks),
        name="ada",
    )(c, w_ada, b_ada)


def _modulated_norm(x, g, scale, shift):
    ms = jnp.mean(x * x, axis=-1, keepdims=True)
    xn = x * lax.rsqrt(ms + NORM_EPS)
    return (xn * g * (1.0 + scale) + shift).astype(BF16)


def _prep_kernel(x_ref, g_ref, scale_ref, shift_ref, h_ref):
    h_ref[...] = _modulated_norm(x_ref[...], g_ref[...], scale_ref[...], shift_ref[...])


def _prep(x, g, scale, shift, *, nb, t, s0=0, s_len=None):
    b, s, d = x.shape
    s_len = s if s_len is None else s_len
    assert b % nb == 0 and s_len % t == 0 and s0 % t == 0 and s0 + s_len <= s
    j0 = s0 // t
    blocks = [((nb, t, d), F32), ((nb, t, d), BF16), ((nb, t, d), F32)]
    return pl.pallas_call(
        _prep_kernel,
        grid=(b // nb, s_len // t),
        in_specs=[pl.BlockSpec((nb, t, d), lambda i, j: (i, j + j0, 0)),
                  pl.BlockSpec((1, 1, d), lambda i, j: (0, 0, 0)),
                  pl.BlockSpec((nb, 1, d), lambda i, j: (i, 0, 0)),
                  pl.BlockSpec((nb, 1, d), lambda i, j: (i, 0, 0))],
        out_specs=pl.BlockSpec((nb, t, d), lambda i, j: (i, j, 0)),
        out_shape=jax.ShapeDtypeStruct((b, s_len, d), BF16),
        compiler_params=_params(("arbitrary", "arbitrary"), blocks),
        name="prep",
    )(x, g, scale, shift)


def _blocked_weight_kernel(*refs, tn):
    o_ref = refs[-1]
    for p, w_ref in enumerate(refs[:-1]):
        o_ref[:, p * tn:(p + 1) * tn] = w_ref[...].astype(BF16)


def _blocked_weight(w_in, d_conv, d_attn, tn, *, tr=2048):
    d, n = w_in.shape
    nbc, nba = d_conv // tn, d_attn // tn
    assert d % tr == 0 and n == N_PARTS * (d_conv + d_attn) and d_conv % tn == 0 and d_attn % tn == 0

    def wspec(p):
        src = lambda j: jnp.where(j < nbc, p * nbc + j, N_PARTS * nbc + p * nba + j - nbc)
        return pl.BlockSpec((tr, tn), lambda i, j: (i, src(j)))

    blocks = [((tr, tn), F32)] * N_PARTS + [((tr, N_PARTS * tn), BF16)]
    return pl.pallas_call(
        functools.partial(_blocked_weight_kernel, tn=tn),
        grid=(d // tr, nbc + nba),
        in_specs=[wspec(p) for p in range(N_PARTS)],
        out_specs=pl.BlockSpec((None, tr, N_PARTS * tn), lambda i, j: (j, i, 0)),
        out_shape=jax.ShapeDtypeStruct((nbc + nba, d, N_PARTS * tn), BF16),
        compiler_params=_params(("arbitrary", "arbitrary"), blocks),
        name="blocked_weight",
    )(*([w_in] * N_PARTS))


def _weight_block_spec(d, tn, index_map):
    return pl.BlockSpec((None, d, N_PARTS * tn), lambda *ids: (index_map(*ids), 0, 0))


def _conv_epilogue(u, prev1, prev2, bg, z, cwb_ref):
    conv = cwb_ref[CONV_WIDTH:CONV_WIDTH + 1, :] + cwb_ref[0:1, :] * prev2
    conv = conv + cwb_ref[1:2, :] * prev1
    conv = conv + cwb_ref[2:3, :] * u
    return (bg * conv) * _silu(z)


def _inproj_prompt_kernel(x_ref, norm_ref, h0_ref, w_ref, cwb_ref, wo_ref,
                          mix_ref, tail_ref, q_ref, k_ref, v_ref, z_ref, k_new_ref, v_new_ref, wo_cast_ref,
                          h_buf, halo_ref, *, nbc, halves, tn):
    i = pl.program_id(0)
    j = pl.program_id(1)
    slot = lax.rem(i, 2)
    sm = x_ref.shape[0]
    hm = h_buf.shape[1] // halves

    @pl.when((i == 0) & (j == 0))
    def _():
        pltpu.sync_copy(h0_ref, h_buf.at[0])

    def prep_next_slab():
        h = _modulated_norm(x_ref[...], norm_ref[0:1, :], norm_ref[1:2, :], norm_ref[2:3, :])
        h_buf[1 - slot, pl.ds(pl.multiple_of(j * sm, sm), sm), :] = h

    def part(h, p):
        return _mm(h, w_ref[:, p * tn:(p + 1) * tn])

    def sub_tiles():
        wo_cast_ref[...] = wo_ref[...].astype(BF16)
        for s in range(halves):
            if s == halves - 1:
                prep_next_slab()
            yield slice(s * hm, (s + 1) * hm)

    @pl.when(j < nbc)
    def _conv_group():
        @pl.when(i == 0)
        def _():
            halo_ref[j] = jnp.zeros(halo_ref.shape[1:], F32)

        last8 = halo_ref[j]
        for rows in sub_tiles():
            h = h_buf[slot, rows, :]
            xin, bg, cg, z = part(h, 0), part(h, 1), part(h, 2), part(h, 3)
            u = cg * xin
            last1 = last8[V7X_SUBLANES - 1:V7X_SUBLANES, :]
            last2 = last8[V7X_SUBLANES - 2:V7X_SUBLANES - 1, :]
            row = lax.broadcasted_iota(jnp.int32, u.shape, 0)
            prev1 = jnp.where(row == 0, last1, pltpu.roll(u, 1, 0))
            prev2 = jnp.where(row == 0, last2, jnp.where(row == 1, last1, pltpu.roll(u, 2, 0)))
            mix_ref[rows, :] = _conv_epilogue(u, prev1, prev2, bg, z, cwb_ref).astype(BF16)
            last8 = u[hm - V7X_SUBLANES:, :]
        halo_ref[j] = last8
        tail_ref[0] = last8

    @pl.when(j >= nbc)
    def _attn_group():
        first_new = h_buf.shape[1] - k_new_ref.shape[1]
        for rows in sub_tiles():
            h = h_buf[slot, rows, :]
            k, v = part(h, 1), part(h, 2)
            q_ref[rows, :] = part(h, 0).astype(BF16)
            k_ref[rows, :] = k.astype(BF16)
            v_ref[rows, :] = v.astype(BF16)
            z_ref[rows, :] = part(h, 3)
            if rows.start >= first_new:
                new_rows = slice(rows.start - first_new, rows.stop - first_new)
                k_new_ref[0, new_rows, :] = k
                v_new_ref[0, new_rows, :] = v


def _inproj_prompt(x, norm, h0, w_blk, conv_wb, w_out, *, d_conv, d_attn, tm, tn, halves, n_new):
    s, d = x.shape
    assert s % tm == 0 and d_conv % tn == 0 and d_attn % tn == 0 and (tm // halves) % V7X_SUBLANES == 0
    nbc, nba = d_conv // tn, d_attn // tn
    nsteps = nbc + nba
    nrt = s // tm
    hm = tm // halves
    sm = tm // nsteps
    assert tm % nsteps == 0 and sm % (2 * V7X_SUBLANES) == 0 and h0.shape == (tm, d)
    assert n_new % hm == 0 and n_new <= tm, "the kept K/V rows are whole sub-tiles of the last row tile"
    cj = lambda j: jnp.minimum(j, nbc - 1)
    aj = lambda j: jnp.maximum(j - nbc, 0)
    attn_spec = pl.BlockSpec((tm, tn), lambda i, j: (i, aj(j)))
    newest_spec = pl.BlockSpec((1, n_new, tn), lambda i, j: (i, 0, aj(j)))
    wo_rows, wo_cols = w_out.shape
    ro = wo_rows // (nrt * nsteps)
    assert wo_rows % (nrt * nsteps) == 0 and ro % (2 * V7X_SUBLANES) == 0
    wo_spec = pl.BlockSpec((ro, wo_cols), lambda i, j: (i * nsteps + j, 0))
    blocks = ([((sm, d), F32), ((d, N_PARTS * tn), BF16)] + [((tm, tn), BF16)] * 4 + [((tm, tn), F32)]
              + [((n_new, tn), F32)] * 2 + [((hm, tn), F32)] * 8 + [((ro, wo_cols), F32)] * 2)
    return pl.pallas_call(
        functools.partial(_inproj_prompt_kernel, nbc=nbc, halves=halves, tn=tn),
        grid=(nrt, nsteps),
        in_specs=[pl.BlockSpec((sm, d), lambda i, j: (jnp.minimum(i + 1, nrt - 1) * nsteps + j, 0)),
                  pl.BlockSpec(norm.shape, lambda i, j: (0, 0)),
                  pl.BlockSpec(memory_space=pl.ANY),
                  _weight_block_spec(d, tn, lambda i, j: j),
                  pl.BlockSpec((CONV_WIDTH + 1, tn), lambda i, j: (0, cj(j))),
                  wo_spec],
        out_specs=[pl.BlockSpec((tm, tn), lambda i, j: (i, cj(j))),
                   pl.BlockSpec((1, V7X_SUBLANES, tn), lambda i, j: (i, 0, cj(j))),
                   attn_spec, attn_spec, attn_spec, attn_spec, newest_spec, newest_spec, wo_spec],
        out_shape=[jax.ShapeDtypeStruct((s, d_conv), BF16),
                   jax.ShapeDtypeStruct((nrt, V7X_SUBLANES, d_conv), F32),
                   jax.ShapeDtypeStruct((s, d_attn), BF16),
                   jax.ShapeDtypeStruct((s, d_attn), BF16),
                   jax.ShapeDtypeStruct((s, d_attn), BF16),
                   jax.ShapeDtypeStruct((s, d_attn), F32),
                   jax.ShapeDtypeStruct((nrt, n_new, d_attn), F32),
                   jax.ShapeDtypeStruct((nrt, n_new, d_attn), F32),
                   jax.ShapeDtypeStruct(w_out.shape, BF16)],
        scratch_shapes=[pltpu.VMEM((2, tm, d), BF16), pltpu.VMEM((nbc, V7X_SUBLANES, tn), F32)],
        compiler_params=_params(("arbitrary", "arbitrary"), blocks, extra_bytes=_nbytes((2, tm, d), BF16)),
        name="inproj_prompt",
    )(x, norm, h0, w_blk, conv_wb, w_out)


def _conv_sample_kernel(h_ref, w_ref, cwb_ref, cache_ref, mix_ref, tail_ref, *, nb, t, tn):
    h = h_ref[...]
    xin, bg, cg, z = (_mm(h, w_ref[:, p * tn:(p + 1) * tn]) for p in range(N_PARTS))
    u = cg * xin
    cache = cache_ref[...]
    c2 = jnp.broadcast_to(cache[:, 0:1, :], (nb, t, tn)).reshape(nb * t, tn)
    c1 = jnp.broadcast_to(cache[:, 1:2, :], (nb, t, tn)).reshape(nb * t, tn)
    pos = lax.broadcasted_iota(jnp.int32, (nb, t, tn), 1).reshape(nb * t, tn)
    prev1 = jnp.where(pos == 0, c1, pltpu.roll(u, 1, 0))
    prev2 = jnp.where(pos == 0, c2, jnp.where(pos == 1, c1, pltpu.roll(u, 2, 0)))
    mix_ref[...] = _conv_epilogue(u, prev1, prev2, bg, z, cwb_ref).astype(BF16)
    tail_ref[...] = u.reshape(nb, t, tn)[:, t - V7X_SUBLANES:, :]


def _conv_sample(h, w_blk, conv_wb, cache_conv, *, d_conv, nb, t, tn):
    m, d = h.shape
    assert m == nb * t and d_conv % tn == 0 and t % V7X_SUBLANES == 0 and t >= CONV_WIDTH - 1
    blocks = [((m, d), BF16), ((d, N_PARTS * tn), BF16), ((m, tn), BF16)] + [((m, tn), F32)] * 10
    return pl.pallas_call(
        functools.partial(_conv_sample_kernel, nb=nb, t=t, tn=tn),
        grid=(d_conv // tn,),
        in_specs=[pl.BlockSpec((m, d), lambda j: (0, 0)),
                  _weight_block_spec(d, tn, lambda j: j),
                  pl.BlockSpec((CONV_WIDTH + 1, tn), lambda j: (0, j)),
                  pl.BlockSpec((nb, CONV_WIDTH - 1, tn), lambda j: (0, 0, j))],
        out_specs=[pl.BlockSpec((m, tn), lambda j: (0, j)),
                   pl.BlockSpec((nb, V7X_SUBLANES, tn), lambda j: (0, 0, j))],
        out_shape=[jax.ShapeDtypeStruct((m, d_conv), BF16),
                   jax.ShapeDtypeStruct((nb, V7X_SUBLANES, d_conv), F32)],
        compiler_params=_params(("arbitrary",), blocks),
        name="conv_sample",
    )(h, w_blk, conv_wb, cache_conv)


def _proj_kernel(h_ref, w_ref, *o_refs, parts, tn):
    h = h_ref[...]
    for p, o_ref in zip(parts, o_refs):
        o_ref[...] = _mm(h, w_ref[:, p * tn:(p + 1) * tn]).astype(o_ref.dtype)


def _proj(h, w_blk, *, block0, n_blocks, parts, out_dtypes, tm, tn, name):
    m, d = h.shape
    assert m % tm == 0 and len(parts) == len(out_dtypes)
    blocks = ([((tm, d), BF16), ((d, N_PARTS * tn), BF16)]
              + [((tm, tn), dt) for dt in out_dtypes] + [((tm, tn), F32)] * len(parts))
    return pl.pallas_call(
        functools.partial(_proj_kernel, parts=parts, tn=tn),
        grid=(m // tm, n_blocks),
        in_specs=[pl.BlockSpec((tm, d), lambda i, j: (i, 0)),
                  _weight_block_spec(d, tn, lambda i, j: block0 + j)],
        out_specs=[pl.BlockSpec((tm, tn), lambda i, j: (i, j)) for _ in out_dtypes],
        out_shape=[jax.ShapeDtypeStruct((m, n_blocks * tn), dt) for dt in out_dtypes],
        compiler_params=_params(("arbitrary", "arbitrary"), blocks),
        name=name,
    )(h, w_blk)


def _bias_seq(rel_bias, offset, n):
    rev = rel_bias[:, ::-1]
    left = offset - MAX_REL
    assert left >= 0
    right = max(n - left - rev.shape[1], 0)
    return jnp.pad(rev, ((0, 0), (left, right)), mode="edge")[:, :n]


def _bias_table_kernel(seq_ref, o_ref, *, rows, cols, lane0, band_cols):
    width = seq_ref.shape[-1]
    seq = jnp.broadcast_to(seq_ref[0], (rows, width))
    table = pltpu.roll(seq, width - lane0, 1, stride=1, stride_axis=0)[:, :cols] * LOG2E
    if band_cols is not None:
        qc = lax.broadcasted_iota(jnp.int32, (rows, cols), 0) // CHUNK
        kc = lax.broadcasted_iota(jnp.int32, (rows, cols), 1) // CHUNK - band_cols // CHUNK
        table = jnp.where((kc <= qc) & (kc >= qc - N_PAST_CHUNKS), table, -jnp.inf)
    o_ref[0] = table


def _bias_table(rel_bias, *, rows, cols, rel0, band_cols=None):
    nh = rel_bias.shape[0]
    lane0 = V7X_LANES * (-(-rows // V7X_LANES))
    width = 1 << (lane0 + cols - 1).bit_length()
    seq = _bias_seq(rel_bias, rel0 + lane0, width)
    blocks = [((1, width), F32), ((rows, cols), F32), ((rows, width), F32), ((rows, width), F32)]
    return pl.pallas_call(
        functools.partial(_bias_table_kernel, rows=rows, cols=cols, lane0=lane0, band_cols=band_cols),
        grid=(nh,),
        in_specs=[pl.BlockSpec((1, 1, width), lambda h: (h, 0, 0))],
        out_specs=pl.BlockSpec((1, rows, cols), lambda h: (h, 0, 0)),
        out_shape=jax.ShapeDtypeStruct((nh, rows, cols), F32),
        compiler_params=_params(("arbitrary",), blocks),
        name="bias_table",
    )(seq.reshape(nh, 1, width))


def _attn_prompt_kernel(q_ref, k0_ref, k1_ref, k2_ref, v0_ref, v1_ref, v2_ref, z_ref, bias_ref, o_ref,
                        *, heads, tq):
    b = pl.program_id(1)
    nk = 3 * tq
    dims = (((1,), (1,)), ((), ()))

    def run(mask_keys):
        if mask_keys:
            kc = lax.broadcasted_iota(jnp.int32, (1, nk), 1) // CHUNK
            valid = kc >= (2 - b) * (tq // CHUNK)
        for hh in range(heads):
            sl = slice(hh * HEAD_DIM, (hh + 1) * HEAD_DIM)
            k = jnp.concatenate([k0_ref[:, sl], k1_ref[:, sl], k2_ref[:, sl]], axis=0)
            v = jnp.concatenate([v0_ref[:, sl], v1_ref[:, sl], v2_ref[:, sl]], axis=0)
            s = lax.dot_general(q_ref[:, sl], k, dims, preferred_element_type=F32)
            s = s * (ATTN_SCALE * LOG2E) + bias_ref[hh]
            if mask_keys:
                s = jnp.where(valid, s, -jnp.inf)
            m = jnp.max(s, axis=-1, keepdims=True)
            p = jnp.exp2(s - m)
            l = jnp.sum(p, axis=-1, keepdims=True)
            o = _mm(p.astype(BF16), v) / l
            o_ref[:, sl] = (o * _silu(z_ref[:, sl])).astype(BF16)

    pl.when(b < 2)(lambda: run(True))
    pl.when(b >= 2)(lambda: run(False))


def _attn_prompt(q, k, v, z, bias, *, tq, heads):
    s, da = q.shape
    nh = da // HEAD_DIM
    wcols = heads * HEAD_DIM
    assert s % tq == 0 and nh % heads == 0 and tq % CHUNK == 0 and 2 * tq >= BAND_PAST
    kspec = lambda back: pl.BlockSpec((tq, wcols), lambda g, b, back=back: (jnp.maximum(b - back, 0), g))
    blocks = ([((tq, wcols), BF16)] * 8 + [((tq, wcols), F32)] + [((heads, tq, 3 * tq), F32)]
              + [((tq, 3 * tq), F32)] * 8)
    return pl.pallas_call(
        functools.partial(_attn_prompt_kernel, heads=heads, tq=tq),
        grid=(nh // heads, s // tq),
        in_specs=[pl.BlockSpec((tq, wcols), lambda g, b: (b, g)),
                  kspec(2), kspec(1), kspec(0), kspec(2), kspec(1), kspec(0),
                  pl.BlockSpec((tq, wcols), lambda g, b: (b, g)),
                  pl.BlockSpec((heads, tq, 3 * tq), lambda g, b: (g, 0, 0))],
        out_specs=pl.BlockSpec((tq, wcols), lambda g, b: (b, g)),
        out_shape=jax.ShapeDtypeStruct((s, da), BF16),
        compiler_params=_params(("arbitrary", "arbitrary"), blocks),
        name="attn_prompt",
    )(q, k, k, k, v, v, v, z, bias)


def _attn_sample_kernel(q_ref, kn_ref, vn_ref, ck_ref, cv_ref, z_ref, bias_ref, o_ref, *, nh, r, t):
    dims = (((1,), (1,)), ((), ()))
    for bb in range(q_ref.shape[0]):
        for hh in range(nh):
            sl = slice(hh * HEAD_DIM, (hh + 1) * HEAD_DIM)
            head_rows = pl.ds(hh, r, stride=nh)
            q = q_ref[bb, :, sl].astype(BF16)
            sc = lax.dot_general(q, ck_ref[bb, head_rows, :].astype(BF16), dims, preferred_element_type=F32)
            sn = lax.dot_general(q, kn_ref[bb, :, sl].astype(BF16), dims, preferred_element_type=F32)
            sc = sc * (ATTN_SCALE * LOG2E) + bias_ref[hh, :, 0:r]
            sn = sn * (ATTN_SCALE * LOG2E) + bias_ref[hh, :, r:r + t]
            m = jnp.maximum(jnp.max(sc, axis=-1, keepdims=True), jnp.max(sn, axis=-1, keepdims=True))
            pc = jnp.exp2(sc - m)
            pn = jnp.exp2(sn - m)
            l = jnp.sum(pc, axis=-1, keepdims=True) + jnp.sum(pn, axis=-1, keepdims=True)
            o = _mm(pc.astype(BF16), cv_ref[bb, head_rows, :].astype(BF16))
            o = o + _mm(pn.astype(BF16), vn_ref[bb, :, sl].astype(BF16))
            o_ref[bb, :, sl] = ((o / l) * _silu(z_ref[bb, :, sl])).astype(BF16)


def _attn_sample(q, kn, vn, cache_k, cache_v, z, bias, *, nbs):
    nb, t, da = q.shape
    nh = da // HEAD_DIM
    r = cache_k.shape[1] // nh
    assert r % V7X_LANES == 0 and bias.shape[2] >= r + t and nb % nbs == 0
    new = pl.BlockSpec((nbs, t, da), lambda b: (b, 0, 0))
    old = pl.BlockSpec((nbs, r * nh, HEAD_DIM), lambda b: (b, 0, 0))
    blocks = ([((nbs, t, da), F32)] * 5 + [((nbs, r, da), F32)] * 2 + [(bias.shape, F32)]
              + [((nbs, r, da), BF16)] * 2)
    return pl.pallas_call(
        functools.partial(_attn_sample_kernel, nh=nh, r=r, t=t),
        grid=(nb // nbs,),
        in_specs=[new, new, new, old, old, new, pl.BlockSpec(bias.shape, lambda b: (0, 0, 0))],
        out_specs=new,
        out_shape=jax.ShapeDtypeStruct((nb, t, da), BF16),
        compiler_params=_params(("arbitrary",), blocks),
        name="attn_sample",
    )(q, kn, vn, cache_k, cache_v, z, bias)


def _out_kernel(x_ref, ma_ref, mb_ref, wa_ref, wb_ref, gate_ref, gf_ref, y_ref, ss_ref, *, nj, tn):
    j = pl.program_id(1)
    acc = _mm(ma_ref[...], wa_ref[...]) + _mm(mb_ref[...], wb_ref[...])
    nb, t, _ = x_ref.shape
    res = x_ref[...] + gate_ref[...] * acc.reshape(nb, t, tn)
    y_ref[:, :, pl.ds(pl.multiple_of(j * tn, tn), tn)] = res
    part = jnp.sum(res * res, axis=-1, keepdims=True)

    @pl.when(j == 0)
    def _():
        ss_ref[...] = part

    @pl.when(j > 0)
    def _():
        ss_ref[...] += part

    @pl.when(j == nj - 1)
    def _():
        inv = lax.rsqrt(ss_ref[...] * (1.0 / (nj * tn)) + NORM_EPS)
        for n in range(nj):
            cols = slice(n * tn, (n + 1) * tn)
            y_ref[:, :, cols] = (y_ref[:, :, cols] * inv) * gf_ref[:, :, cols]


def _out(x, mix_a, mix_b, w_out, gate, g_final, *, nb, t, tn):
    b, s, d = x.shape
    half = mix_a.shape[1]
    tm = nb * t
    assert b % nb == 0 and s % t == 0 and d % tn == 0 and w_out.shape[0] == 2 * half
    nj = d // tn
    nt = s // t
    assert nb == 1 or nt == 1, "a row tile must be contiguous in the flattened (B*S) mixed rows"
    blocks = ([((nb, t, tn), F32)] + [((tm, half), BF16)] * 2 + [((half, tn), BF16)] * 2
              + [((nb, t, d), F32)] + [((tm, tn), F32)] * 4)
    return pl.pallas_call(
        functools.partial(_out_kernel, nj=nj, tn=tn),
        grid=((b // nb) * nt, nj),
        in_specs=[pl.BlockSpec((nb, t, tn), lambda i, j: (i // nt, i % nt, j)),
                  pl.BlockSpec((tm, half), lambda i, j: (i, 0)),
                  pl.BlockSpec((tm, half), lambda i, j: (i, 0)),
                  pl.BlockSpec((half, tn), lambda i, j: (0, j)),
                  pl.BlockSpec((half, tn), lambda i, j: (1, j)),
                  pl.BlockSpec((nb, 1, tn), lambda i, j: (i // nt, 0, j)),
                  pl.BlockSpec((1, 1, d), lambda i, j: (0, 0, 0))],
        out_specs=pl.BlockSpec((nb, t, d), lambda i, j: (i // nt, i % nt, 0)),
        out_shape=jax.ShapeDtypeStruct((b, s, d), F32),
        scratch_shapes=[pltpu.VMEM((nb, t, 1), F32)],
        compiler_params=_params(("arbitrary", "arbitrary"), blocks),
        name="out_proj",
    )(x, mix_a, mix_b, w_out, w_out, gate, g_final)


def kernel(x_prompt, x_sample, cache_k, cache_v, cache_conv, c_prompt, c_sample,
           g_norm, w_ada, b_ada, w_in, conv_w, conv_b, rel_bias, w_out, g_final):
    depth = g_norm.shape[0]
    assert depth == 1, "single-layer trunk"
    bp, sp, d = x_prompt.shape
    bs, ts, _ = x_sample.shape
    assert bp == 1
    d_conv = conv_w.shape[-1]
    d_attn = w_out.shape[1] - d_conv
    nh = d_attn // HEAD_DIM
    r = cache_k.shape[2]
    rows_kept = min(BAND_PAST, sp)

    n_c = bp + bs
    pad = (-n_c) % V7X_SUBLANES
    c_all = jnp.concatenate([c_prompt, c_sample, jnp.zeros((pad, d), F32)], axis=0)
    mod = _ada(c_all, w_ada[0], b_ada)
    shift, scale, gate = (mod[:n_c, i * d:(i + 1) * d].reshape(n_c, 1, d) for i in range(3))

    tn = 256
    nbc, nba = d_conv // tn, d_attn // tn
    w_blk = _blocked_weight(w_in[0], d_conv, d_attn, tn)
    g3 = g_norm.reshape(1, 1, d)
    gf3 = g_final.reshape(1, 1, d)
    conv_wb = jnp.concatenate([conv_w[0], conv_b], axis=0)

    tm, tq, halves = 1024, 256, 2
    h_first = _prep(x_prompt, g3, scale[:bp], shift[:bp], nb=1, t=512, s0=0, s_len=tm).reshape(tm, d)
    norm_p = jnp.concatenate([g_norm, scale[0], shift[0]], axis=0)
    mix_conv_p, u_tail_p, qp, kp, vp, zp, k_keep, v_keep, w_out_b = _inproj_prompt(
        x_prompt.reshape(sp, d), norm_p, h_first, w_blk, conv_wb, w_out[0],
        d_conv=d_conv, d_attn=d_attn, tm=tm, tn=tn, halves=halves, n_new=rows_kept)
    bias_p = _bias_table(rel_bias[0], rows=tq, cols=3 * tq, rel0=2 * tq, band_cols=2 * tq)
    mix_attn_p = _attn_prompt(qp, kp, vp, zp, bias_p, tq=tq, heads=16)
    y_prompt = _out(x_prompt, mix_conv_p, mix_attn_p, w_out_b, gate[:bp], gf3, nb=1, t=512, tn=1024)

    hs = _prep(x_sample, g3, scale[bp:], shift[bp:], nb=8, t=ts).reshape(bs * ts, d)
    mix_conv_s, u_tail_s = _conv_sample(hs, w_blk, conv_wb, cache_conv[0], d_conv=d_conv, nb=bs, t=ts, tn=tn)
    qs, ks, vs, zs = _proj(hs, w_blk, block0=nbc, n_blocks=nba, parts=(0, 1, 2, 3), out_dtypes=[F32] * 4,
                           tm=bs * ts, tn=tn, name="qkvz_sample")
    bias_s = _bias_table(rel_bias[0], rows=ts, cols=r + V7X_LANES * (-(-ts // V7X_LANES)), rel0=r)
    to3 = lambda a: a.reshape(bs, ts, d_attn)
    mix_attn_s = _attn_sample(to3(qs), to3(ks), to3(vs), cache_k[0].reshape(bs, r * nh, HEAD_DIM),
                              cache_v[0].reshape(bs, r * nh, HEAD_DIM), to3(zs), bias_s, nbs=1)
    y_sample = _out(x_sample, mix_conv_s, mix_attn_s.reshape(bs * ts, d_attn), w_out_b, gate[bp:], gf3,
                    nb=16, t=ts, tn=1024)

    keep = CONV_WIDTH - 1
    new_k_prompt = k_keep[-1].reshape(1, bp, rows_kept, nh, HEAD_DIM)
    new_v_prompt = v_keep[-1].reshape(1, bp, rows_kept, nh, HEAD_DIM)
    new_conv_prompt = u_tail_p[-1, V7X_SUBLANES - keep:, :].reshape(1, bp, keep, d_conv)
    new_k_sample = ks.reshape(1, bs, ts, nh, HEAD_DIM)
    new_v_sample = vs.reshape(1, bs, ts, nh, HEAD_DIM)
    new_conv_sample = u_tail_s[:, V7X_SUBLANES - keep:, :].reshape(1, bs, keep, d_conv)
    return (y_prompt, y_sample, new_k_prompt, new_v_prompt, new_conv_prompt,
            new_k_sample, new_v_sample, new_conv_sample)
```

```python
import functools
import math

import jax
import jax.numpy as jnp
from jax import lax
from jax.experimental import pallas as pl
from jax.experimental.pallas import tpu as pltpu

CHUNK = 64
N_PAST_CHUNKS = 8
BAND_PAST = N_PAST_CHUNKS * CHUNK
HEAD_DIM = 128
CONV_WIDTH = 3
MAX_REL = 256
NORM_EPS = 1e-6
ATTN_SCALE = HEAD_DIM ** -0.5
LOG2E = math.log2(math.e)
N_PARTS = 4

V7X_LANES = 128
V7X_SUBLANES = 8
V7X_VMEM_BYTES = 64 * 1024 * 1024

F32 = jnp.float32
BF16 = jnp.bfloat16


def _nbytes(shape, dtype):
    n = 1
    for s in shape:
        n *= s
    return n * jnp.dtype(dtype).itemsize


def _params(semantics, blocks, extra_bytes=0):
    need = 2 * sum(_nbytes(s, d) for s, d in blocks) + extra_bytes + 16 * 1024 * 1024
    limit = int(min(need, V7X_VMEM_BYTES - 4 * 1024 * 1024))
    return pltpu.CompilerParams(dimension_semantics=semantics, vmem_limit_bytes=limit)


def _silu(z):
    return z * jax.nn.sigmoid(z)


def _mm(a, b):
    return jnp.dot(a, b, preferred_element_type=F32)


def _ada_kernel(c_ref, w_ref, b_ref, o_ref):
    o_ref[...] = _mm(c_ref[...].astype(BF16), w_ref[...].astype(BF16)) + b_ref[...]


def _ada(c, w_ada, b_ada, *, tn=512):
    r, d = c.shape
    n = w_ada.shape[1]
    assert n % tn == 0 and r % V7X_SUBLANES == 0
    blocks = [((r, d), F32), ((d, tn), F32), ((1, tn), F32), ((r, tn), F32), ((d, tn), BF16)]
    return pl.pallas_call(
        _ada_kernel,
        grid=(n // tn,),
        in_specs=[pl.BlockSpec((r, d), lambda j: (0, 0)),
                  pl.BlockSpec((d, tn), lambda j: (0, j)),
                  pl.BlockSpec((1, tn), lambda j: (0, j))],
        out_specs=pl.BlockSpec((r, tn), lambda j: (0, j)),
        out_shape=jax.ShapeDtypeStruct((r, n), F32),
        compiler_params=_params(("arbitrary",), blocks),
        name="ada",
    )(c, w_ada, b_ada)


def _modulated_norm(x, g, scale, shift):
    ms = jnp.mean(x * x, axis=-1, keepdims=True)
    xn = x * lax.rsqrt(ms + NORM_EPS)
    return (xn * g * (1.0 + scale) + shift).astype(BF16)


def _prep_kernel(x_ref, g_ref, scale_ref, shift_ref, h_ref):
    h_ref[...] = _modulated_norm(x_ref[...], g_ref[...], scale_ref[...], shift_ref[...])


def _prep(x, g, scale, shift, *, nb, t, s0=0, s_len=None):
    b, s, d = x.shape
    s_len = s if s_len is None else s_len
    assert b % nb == 0 and s_len % t == 0 and s0 % t == 0 and s0 + s_len <= s
    j0 = s0 // t
    blocks = [((nb, t, d), F32), ((nb, t, d), BF16), ((nb, t, d), F32)]
    return pl.pallas_call(
        _prep_kernel,
        grid=(b // nb, s_len // t),
        in_specs=[pl.BlockSpec((nb, t, d), lambda i, j: (i, j + j0, 0)),
                  pl.BlockSpec((1, 1, d), lambda i, j: (0, 0, 0)),
                  pl.BlockSpec((nb, 1, d), lambda i, j: (i, 0, 0)),
                  pl.BlockSpec((nb, 1, d), lambda i, j: (i, 0, 0))],
        out_specs=pl.BlockSpec((nb, t, d), lambda i, j: (i, j, 0)),
        out_shape=jax.ShapeDtypeStruct((b, s_len, d), BF16),
        compiler_params=_params(("arbitrary", "arbitrary"), blocks),
        name="prep",
    )(x, g, scale, shift)


def _weight_block_spec(d, tn, index_map):
    return pl.BlockSpec((None, d, N_PARTS * tn), lambda *ids: (index_map(*ids), 0, 0))


def _conv_epilogue(u, prev1, prev2, bg, z, cwb_ref):
    conv = cwb_ref[CONV_WIDTH:CONV_WIDTH + 1, :] + cwb_ref[0:1, :] * prev2
    conv = conv + cwb_ref[1:2, :] * prev1
    conv = conv + cwb_ref[2:3, :] * u
    return (bg * conv) * _silu(z)


def _inproj_prompt_kernel(x_ref, norm_ref, h0_ref, w_ref, cwb_ref, wo_ref,
                          mix_ref, tail_ref, q_ref, k_ref, v_ref, z_ref, k_new_ref, v_new_ref, wo_cast_ref,
                          h_buf, halo_ref, *, nbc, halves, tn):
    i = pl.program_id(0)
    j = pl.program_id(1)
    slot = lax.rem(i, 2)
    sm = x_ref.shape[0]
    hm = h_buf.shape[1] // halves

    @pl.when((i == 0) & (j == 0))
    def _():
        pltpu.sync_copy(h0_ref, h_buf.at[0])

    def prep_next_slab():
        h = _modulated_norm(x_ref[...], norm_ref[0:1, :], norm_ref[1:2, :], norm_ref[2:3, :])
        h_buf[1 - slot, pl.ds(pl.multiple_of(j * sm, sm), sm), :] = h

    def part(h, p):
        return _mm(h, w_ref[:, p * tn:(p + 1) * tn])

    def sub_tiles():
        wo_cast_ref[...] = wo_ref[...].astype(BF16)
        for s in range(halves):
            if s == halves - 1:
                prep_next_slab()
            yield slice(s * hm, (s + 1) * hm)

    @pl.when(j < nbc)
    def _conv_group():
        @pl.when(i == 0)
        def _():
            halo_ref[j] = jnp.zeros(halo_ref.shape[1:], F32)

        last8 = halo_ref[j]
        for rows in sub_tiles():
            h = h_buf[slot, rows, :]
            xin, bg, cg, z = part(h, 0), part(h, 1), part(h, 2), part(h, 3)
            u = cg * xin
            last1 = last8[V7X_SUBLANES - 1:V7X_SUBLANES, :]
            last2 = last8[V7X_SUBLANES - 2:V7X_SUBLANES - 1, :]
            row = lax.broadcasted_iota(jnp.int32, u.shape, 0)
            prev1 = jnp.where(row == 0, last1, pltpu.roll(u, 1, 0))
            prev2 = jnp.where(row == 0, last2, jnp.where(row == 1, last1, pltpu.roll(u, 2, 0)))
            mix_ref[rows, :] = _conv_epilogue(u, prev1, prev2, bg, z, cwb_ref).astype(BF16)
            last8 = u[hm - V7X_SUBLANES:, :]
        halo_ref[j] = last8
        tail_ref[0] = last8

    @pl.when(j >= nbc)
    def _attn_group():
        first_new = h_buf.shape[1] - k_new_ref.shape[1]
        for rows in sub_tiles():
            h = h_buf[slot, rows, :]
            k, v = part(h, 1), part(h, 2)
            q_ref[rows, :] = part(h, 0).astype(BF16)
            k_ref[rows, :] = k.astype(BF16)
            v_ref[rows, :] = v.astype(BF16)
            z_ref[rows, :] = part(h, 3)
            if rows.start >= first_new:
                new_rows = slice(rows.start - first_new, rows.stop - first_new)
                k_new_ref[0, new_rows, :] = k
                v_new_ref[0, new_rows, :] = v


def _inproj_prompt(x, norm, h0, w_blk, conv_wb, w_out, *, d_conv, d_attn, tm, tn, halves, n_new):
    s, d = x.shape
    assert s % tm == 0 and d_conv % tn == 0 and d_attn % tn == 0 and (tm // halves) % V7X_SUBLANES == 0
    nbc, nba = d_conv // tn, d_attn // tn
    nsteps = nbc + nba
    nrt = s // tm
    hm = tm // halves
    sm = tm // nsteps
    assert tm % nsteps == 0 and sm % (2 * V7X_SUBLANES) == 0 and h0.shape == (tm, d)
    assert n_new % hm == 0 and n_new <= tm, "the kept K/V rows are whole sub-tiles of the last row tile"
    cj = lambda j: jnp.minimum(j, nbc - 1)
    aj = lambda j: jnp.maximum(j - nbc, 0)
    attn_spec = pl.BlockSpec((tm, tn), lambda i, j: (i, aj(j)))
    newest_spec = pl.BlockSpec((1, n_new, tn), lambda i, j: (i, 0, aj(j)))
    wo_rows, wo_cols = w_out.shape
    ro = wo_rows // (nrt * nsteps)
    assert wo_rows % (nrt * nsteps) == 0 and ro % (2 * V7X_SUBLANES) == 0
    wo_spec = pl.BlockSpec((ro, wo_cols), lambda i, j: (i * nsteps + j, 0))
    blocks = ([((sm, d), F32), ((d, N_PARTS * tn), BF16)] + [((tm, tn), BF16)] * 4 + [((tm, tn), F32)]
              + [((n_new, tn), F32)] * 2 + [((hm, tn), F32)] * 8 + [((ro, wo_cols), F32)] * 2)
    return pl.pallas_call(
        functools.partial(_inproj_prompt_kernel, nbc=nbc, halves=halves, tn=tn),
        grid=(nrt, nsteps),
        in_specs=[pl.BlockSpec((sm, d), lambda i, j: (jnp.minimum(i + 1, nrt - 1) * nsteps + j, 0)),
                  pl.BlockSpec(norm.shape, lambda i, j: (0, 0)),
                  pl.BlockSpec(memory_space=pl.ANY),
                  _weight_block_spec(d, tn, lambda i, j: j),
                  pl.BlockSpec((CONV_WIDTH + 1, tn), lambda i, j: (0, cj(j))),
                  wo_spec],
        out_specs=[pl.BlockSpec((tm, tn), lambda i, j: (i, cj(j))),
                   pl.BlockSpec((1, V7X_SUBLANES, tn), lambda i, j: (i, 0, cj(j))),
                   attn_spec, attn_spec, attn_spec, attn_spec, newest_spec, newest_spec, wo_spec],
        out_shape=[jax.ShapeDtypeStruct((s, d_conv), BF16),
                   jax.ShapeDtypeStruct((nrt, V7X_SUBLANES, d_conv), F32),
                   jax.ShapeDtypeStruct((s, d_attn), BF16),
                   jax.ShapeDtypeStruct((s, d_attn), BF16),
                   jax.ShapeDtypeStruct((s, d_attn), BF16),
                   jax.ShapeDtypeStruct((s, d_attn), F32),
                   jax.ShapeDtypeStruct((nrt, n_new, d_attn), F32),
                   jax.ShapeDtypeStruct((nrt, n_new, d_attn), F32),
                   jax.ShapeDtypeStruct(w_out.shape, BF16)],
        scratch_shapes=[pltpu.VMEM((2, tm, d), BF16), pltpu.VMEM((nbc, V7X_SUBLANES, tn), F32)],
        compiler_params=_params(("arbitrary", "arbitrary"), blocks, extra_bytes=_nbytes((2, tm, d), BF16)),
        name="inproj_prompt",
    )(x, norm, h0, w_blk, conv_wb, w_out)


def _inproj_sample_kernel(h_ref, w0_ref, w1_ref, w2_ref, w3_ref, cwb_ref, cache_ref,
                          mix_ref, tail_ref, q_ref, k_ref, v_ref, z_ref, wblk_ref, *, nb, t, tn, n_conv):
    j = pl.program_id(0)
    tc = w0_ref.shape[1]
    piece = lax.rem(j, tn // tc)
    w = [w_ref[...].astype(BF16) for w_ref in (w0_ref, w1_ref, w2_ref, w3_ref)]
    for p in range(N_PARTS):
        wblk_ref[:, pl.ds(pl.multiple_of(p * tn + piece * tc, tc), tc)] = w[p]
    acc = _mm(h_ref[...], jnp.concatenate(w, axis=1))
    p0, p1, p2, p3 = (acc[:, p * tc:(p + 1) * tc] for p in range(N_PARTS))

    @pl.when(j < n_conv)
    def _conv_group():
        u = p2 * p0
        cache = cache_ref[...]
        c2 = jnp.broadcast_to(cache[:, 0:1, :], (nb, t, tc)).reshape(nb * t, tc)
        c1 = jnp.broadcast_to(cache[:, 1:2, :], (nb, t, tc)).reshape(nb * t, tc)
        pos = lax.broadcasted_iota(jnp.int32, (nb, t, tc), 1).reshape(nb * t, tc)
        prev1 = jnp.where(pos == 0, c1, pltpu.roll(u, 1, 0))
        prev2 = jnp.where(pos == 0, c2, jnp.where(pos == 1, c1, pltpu.roll(u, 2, 0)))
        mix_ref[...] = _conv_epilogue(u, prev1, prev2, p1, p3, cwb_ref).astype(BF16)
        tail_ref[...] = u.reshape(nb, t, tc)[:, t - V7X_SUBLANES:, :]

    @pl.when(j >= n_conv)
    def _attn_group():
        q_ref[...] = p0
        k_ref[...] = p1
        v_ref[...] = p2
        z_ref[...] = p3


def _inproj_sample(h, w_in, conv_wb, cache_conv, *, d_conv, d_attn, nb, t, tn, tc):
    m, d = h.shape
    assert m == nb * t and t % V7X_SUBLANES == 0 and t >= CONV_WIDTH - 1
    assert tn % tc == 0 and d_conv % tn == 0 and d_attn % tn == 0 and tc % V7X_LANES == 0
    n_conv, n_attn = d_conv // tc, d_attn // tc

    def wspec(p):
        src = lambda j: jnp.where(j < n_conv, p * n_conv + j, N_PARTS * n_conv + p * n_attn + j - n_conv)
        return pl.BlockSpec((d, tc), lambda j: (0, src(j)))

    cj = lambda j: jnp.minimum(j, n_conv - 1)
    aj = lambda j: jnp.maximum(j - n_conv, 0)
    attn_spec = pl.BlockSpec((m, tc), lambda j: (0, aj(j)))
    blocks = ([((d, tc), F32)] * N_PARTS + [((d, N_PARTS * tn), BF16)] + [((m, tc), F32)] * 6
              + [((d, N_PARTS * tc), BF16)] * 2)
    n_blocks = (d_conv + d_attn) // tn
    return pl.pallas_call(
        functools.partial(_inproj_sample_kernel, nb=nb, t=t, tn=tn, n_conv=n_conv),
        grid=(n_conv + n_attn,),
        in_specs=[pl.BlockSpec((m, d), lambda j: (0, 0), pipeline_mode=pl.Buffered(1)),
                  wspec(0), wspec(1), wspec(2), wspec(3),
                  pl.BlockSpec((CONV_WIDTH + 1, tc), lambda j: (0, cj(j))),
                  pl.BlockSpec((nb, CONV_WIDTH - 1, tc), lambda j: (0, 0, cj(j)))],
        out_specs=[pl.BlockSpec((m, tc), lambda j: (0, cj(j))),
                   pl.BlockSpec((nb, V7X_SUBLANES, tc), lambda j: (0, 0, cj(j))),
                   attn_spec, attn_spec, attn_spec, attn_spec,
                   pl.BlockSpec((None, d, N_PARTS * tn), lambda j: (j // (tn // tc), 0, 0))],
        out_shape=[jax.ShapeDtypeStruct((m, d_conv), BF16),
                   jax.ShapeDtypeStruct((nb, V7X_SUBLANES, d_conv), F32),
                   jax.ShapeDtypeStruct((m, d_attn), F32),
                   jax.ShapeDtypeStruct((m, d_attn), F32),
                   jax.ShapeDtypeStruct((m, d_attn), F32),
                   jax.ShapeDtypeStruct((m, d_attn), F32),
                   jax.ShapeDtypeStruct((n_blocks, d, N_PARTS * tn), BF16)],
        compiler_params=_params(("arbitrary",), blocks, extra_bytes=_nbytes((m, d), BF16)),
        name="inproj_sample",
    )(h, w_in, w_in, w_in, w_in, conv_wb, cache_conv)


def _bias_seq(rel_bias, offset, n):
    rev = rel_bias[:, ::-1]
    left = offset - MAX_REL
    assert left >= 0
    right = max(n - left - rev.shape[1], 0)
    return jnp.pad(rev, ((0, 0), (left, right)), mode="edge")[:, :n]


def _bias_table_kernel(seq_ref, o_ref, *, rows, cols, lane0, band_cols):
    width = seq_ref.shape[-1]
    seq = jnp.broadcast_to(seq_ref[0], (rows, width))
    table = pltpu.roll(seq, width - lane0, 1, stride=1, stride_axis=0)[:, :cols] * LOG2E
    if band_cols is not None:
        qc = lax.broadcasted_iota(jnp.int32, (rows, cols), 0) // CHUNK
        kc = lax.broadcasted_iota(jnp.int32, (rows, cols), 1) // CHUNK - band_cols // CHUNK
        table = jnp.where((kc <= qc) & (kc >= qc - N_PAST_CHUNKS), table, -jnp.inf)
    o_ref[0] = table


def _bias_table(rel_bias, *, rows, cols, rel0, band_cols=None):
    nh = rel_bias.shape[0]
    lane0 = V7X_LANES * (-(-rows // V7X_LANES))
    width = 1 << (lane0 + cols - 1).bit_length()
    seq = _bias_seq(rel_bias, rel0 + lane0, width)
    blocks = [((1, width), F32), ((rows, cols), F32), ((rows, width), F32), ((rows, width), F32)]
    return pl.pallas_call(
        functools.partial(_bias_table_kernel, rows=rows, cols=cols, lane0=lane0, band_cols=band_cols),
        grid=(nh,),
        in_specs=[pl.BlockSpec((1, 1, width), lambda h: (h, 0, 0))],
        out_specs=pl.BlockSpec((1, rows, cols), lambda h: (h, 0, 0)),
        out_shape=jax.ShapeDtypeStruct((nh, rows, cols), F32),
        compiler_params=_params(("arbitrary",), blocks),
        name="bias_table",
    )(seq.reshape(nh, 1, width))


def _attn_prompt_kernel(q_ref, k0_ref, k1_ref, k2_ref, v0_ref, v1_ref, v2_ref, z_ref, bias_ref, o_ref,
                        *, heads, tq):
    b = pl.program_id(1)
    nk = 3 * tq
    dims = (((1,), (1,)), ((), ()))

    def run(mask_keys):
        if mask_keys:
            kc = lax.broadcasted_iota(jnp.int32, (1, nk), 1) // CHUNK
            valid = kc >= (2 - b) * (tq // CHUNK)
        for hh in range(heads):
            sl = slice(hh * HEAD_DIM, (hh + 1) * HEAD_DIM)
            k = jnp.concatenate([k0_ref[:, sl], k1_ref[:, sl], k2_ref[:, sl]], axis=0)
            v = jnp.concatenate([v0_ref[:, sl], v1_ref[:, sl], v2_ref[:, sl]], axis=0)
            s = lax.dot_general(q_ref[:, sl], k, dims, preferred_element_type=F32)
            s = s * (ATTN_SCALE * LOG2E) + bias_ref[hh]
            if mask_keys:
                s = jnp.where(valid, s, -jnp.inf)
            m = jnp.max(s, axis=-1, keepdims=True)
            p = jnp.exp2(s - m)
            l = jnp.sum(p, axis=-1, keepdims=True)
            o = _mm(p.astype(BF16), v) / l
            o_ref[:, sl] = (o * _silu(z_ref[:, sl])).astype(BF16)

    pl.when(b < 2)(lambda: run(True))
    pl.when(b >= 2)(lambda: run(False))


def _attn_prompt(q, k, v, z, bias, *, tq, heads):
    s, da = q.shape
    nh = da // HEAD_DIM
    wcols = heads * HEAD_DIM
    assert s % tq == 0 and nh % heads == 0 and tq % CHUNK == 0 and 2 * tq >= BAND_PAST
    kspec = lambda back: pl.BlockSpec((tq, wcols), lambda g, b, back=back: (jnp.maximum(b - back, 0), g))
    blocks = ([((tq, wcols), BF16)] * 8 + [((tq, wcols), F32)] + [((heads, tq, 3 * tq), F32)]
              + [((tq, 3 * tq), F32)] * 8)
    return pl.pallas_call(
        functools.partial(_attn_prompt_kernel, heads=heads, tq=tq),
        grid=(nh // heads, s // tq),
        in_specs=[pl.BlockSpec((tq, wcols), lambda g, b: (b, g)),
                  kspec(2), kspec(1), kspec(0), kspec(2), kspec(1), kspec(0),
                  pl.BlockSpec((tq, wcols), lambda g, b: (b, g)),
                  pl.BlockSpec((heads, tq, 3 * tq), lambda g, b: (g, 0, 0))],
        out_specs=pl.BlockSpec((tq, wcols), lambda g, b: (b, g)),
        out_shape=jax.ShapeDtypeStruct((s, da), BF16),
        compiler_params=_params(("arbitrary", "arbitrary"), blocks),
        name="attn_prompt",
    )(q, k, k, k, v, v, v, z, bias)


def _attn_sample_kernel(q_ref, kn_ref, vn_ref, ck_ref, cv_ref, z_ref, bias_ref, o_ref, *, nh, r, t):
    dims = (((1,), (1,)), ((), ()))
    for bb in range(q_ref.shape[0]):
        for hh in range(nh):
            sl = slice(hh * HEAD_DIM, (hh + 1) * HEAD_DIM)
            head_rows = pl.ds(hh, r, stride=nh)
            q = q_ref[bb, :, sl].astype(BF16)
            sc = lax.dot_general(q, ck_ref[bb, head_rows, :].astype(BF16), dims, preferred_element_type=F32)
            sn = lax.dot_general(q, kn_ref[bb, :, sl].astype(BF16), dims, preferred_element_type=F32)
            sc = sc * (ATTN_SCALE * LOG2E) + bias_ref[hh, :, 0:r]
            sn = sn * (ATTN_SCALE * LOG2E) + bias_ref[hh, :, r:r + t]
            m = jnp.maximum(jnp.max(sc, axis=-1, keepdims=True), jnp.max(sn, axis=-1, keepdims=True))
            pc = jnp.exp2(sc - m)
            pn = jnp.exp2(sn - m)
            l = jnp.sum(pc, axis=-1, keepdims=True) + jnp.sum(pn, axis=-1, keepdims=True)
            o = _mm(pc.astype(BF16), cv_ref[bb, head_rows, :].astype(BF16))
            o = o + _mm(pn.astype(BF16), vn_ref[bb, :, sl].astype(BF16))
            o_ref[bb, :, sl] = ((o / l) * _silu(z_ref[bb, :, sl])).astype(BF16)


def _attn_sample(q, kn, vn, cache_k, cache_v, z, bias, *, nbs):
    nb, t, da = q.shape
    nh = da // HEAD_DIM
    r = cache_k.shape[1] // nh
    assert r % V7X_LANES == 0 and bias.shape[2] >= r + t and nb % nbs == 0
    new = pl.BlockSpec((nbs, t, da), lambda b: (b, 0, 0))
    old = pl.BlockSpec((nbs, r * nh, HEAD_DIM), lambda b: (b, 0, 0))
    blocks = ([((nbs, t, da), F32)] * 5 + [((nbs, r, da), F32)] * 2 + [(bias.shape, F32)]
              + [((nbs, r, da), BF16)] * 2)
    return pl.pallas_call(
        functools.partial(_attn_sample_kernel, nh=nh, r=r, t=t),
        grid=(nb // nbs,),
        in_specs=[new, new, new, old, old, new, pl.BlockSpec(bias.shape, lambda b: (0, 0, 0))],
        out_specs=new,
        out_shape=jax.ShapeDtypeStruct((nb, t, da), BF16),
        compiler_params=_params(("arbitrary",), blocks),
        name="attn_sample",
    )(q, kn, vn, cache_k, cache_v, z, bias)


def _out_kernel(x_ref, ma_ref, mb_ref, wa_ref, wb_ref, gate_ref, gf_ref, y_ref, ss_ref, *, nj, tn):
    j = pl.program_id(1)
    acc = _mm(ma_ref[...], wa_ref[...]) + _mm(mb_ref[...], wb_ref[...])
    nb, t, _ = x_ref.shape
    res = x_ref[...] + gate_ref[...] * acc.reshape(nb, t, tn)
    y_ref[:, :, pl.ds(pl.multiple_of(j * tn, tn), tn)] = res
    part = jnp.sum(res * res, axis=-1, keepdims=True)

    @pl.when(j == 0)
    def _():
        ss_ref[...] = part

    @pl.when(j > 0)
    def _():
        ss_ref[...] += part

    @pl.when(j == nj - 1)
    def _():
        inv = lax.rsqrt(ss_ref[...] * (1.0 / (nj * tn)) + NORM_EPS)
        for n in range(nj):
            cols = slice(n * tn, (n + 1) * tn)
            y_ref[:, :, cols] = (y_ref[:, :, cols] * inv) * gf_ref[:, :, cols]


def _out(x, mix_a, mix_b, w_out, gate, g_final, *, nb, t, tn):
    b, s, d = x.shape
    half = mix_a.shape[1]
    tm = nb * t
    assert b % nb == 0 and s % t == 0 and d % tn == 0 and w_out.shape[0] == 2 * half
    nj = d // tn
    nt = s // t
    assert nb == 1 or nt == 1, "a row tile must be contiguous in the flattened (B*S) mixed rows"
    blocks = ([((nb, t, tn), F32)] + [((tm, half), BF16)] * 2 + [((half, tn), BF16)] * 2
              + [((nb, t, d), F32)] + [((tm, tn), F32)] * 4)
    return pl.pallas_call(
        functools.partial(_out_kernel, nj=nj, tn=tn),
        grid=((b // nb) * nt, nj),
        in_specs=[pl.BlockSpec((nb, t, tn), lambda i, j: (i // nt, i % nt, j)),
                  pl.BlockSpec((tm, half), lambda i, j: (i, 0)),
                  pl.BlockSpec((tm, half), lambda i, j: (i, 0)),
                  pl.BlockSpec((half, tn), lambda i, j: (0, j)),
                  pl.BlockSpec((half, tn), lambda i, j: (1, j)),
                  pl.BlockSpec((nb, 1, tn), lambda i, j: (i // nt, 0, j)),
                  pl.BlockSpec((1, 1, d), lambda i, j: (0, 0, 0))],
        out_specs=pl.BlockSpec((nb, t, d), lambda i, j: (i // nt, i % nt, 0)),
        out_shape=jax.ShapeDtypeStruct((b, s, d), F32),
        scratch_shapes=[pltpu.VMEM((nb, t, 1), F32)],
        compiler_params=_params(("arbitrary", "arbitrary"), blocks),
        name="out_proj",
    )(x, mix_a, mix_b, w_out, w_out, gate, g_final)


def kernel(x_prompt, x_sample, cache_k, cache_v, cache_conv, c_prompt, c_sample,
           g_norm, w_ada, b_ada, w_in, conv_w, conv_b, rel_bias, w_out, g_final):
    depth = g_norm.shape[0]
    assert depth == 1, "single-layer trunk"
    bp, sp, d = x_prompt.shape
    bs, ts, _ = x_sample.shape
    assert bp == 1
    d_conv = conv_w.shape[-1]
    d_attn = w_out.shape[1] - d_conv
    nh = d_attn // HEAD_DIM
    r = cache_k.shape[2]
    rows_kept = min(BAND_PAST, sp)

    n_c = bp + bs
    pad = (-n_c) % V7X_SUBLANES
    c_all = jnp.concatenate([c_prompt, c_sample, jnp.zeros((pad, d), F32)], axis=0)
    mod = _ada(c_all, w_ada[0], b_ada)
    shift, scale, gate = (mod[:n_c, i * d:(i + 1) * d].reshape(n_c, 1, d) for i in range(3))

    tn = 256
    g3 = g_norm.reshape(1, 1, d)
    gf3 = g_final.reshape(1, 1, d)
    conv_wb = jnp.concatenate([conv_w[0], conv_b], axis=0)

    hs = _prep(x_sample, g3, scale[bp:], shift[bp:], nb=8, t=ts).reshape(bs * ts, d)
    mix_conv_s, u_tail_s, qs, ks, vs, zs, w_blk = _inproj_sample(
        hs, w_in[0], conv_wb, cache_conv[0], d_conv=d_conv, d_attn=d_attn, nb=bs, t=ts, tn=tn, tc=V7X_LANES)

    tm, tq, halves = 1024, 256, 2
    h_first = _prep(x_prompt, g3, scale[:bp], shift[:bp], nb=1, t=512, s0=0, s_len=tm).reshape(tm, d)
    norm_p = jnp.concatenate([g_norm, scale[0], shift[0]], axis=0)
    mix_conv_p, u_tail_p, qp, kp, vp, zp, k_keep, v_keep, w_out_b = _inproj_prompt(
        x_prompt.reshape(sp, d), norm_p, h_first, w_blk, conv_wb, w_out[0],
        d_conv=d_conv, d_attn=d_attn, tm=tm, tn=tn, halves=halves, n_new=rows_kept)
    bias_p = _bias_table(rel_bias[0], rows=tq, cols=3 * tq, rel0=2 * tq, band_cols=2 * tq)
    mix_attn_p = _attn_prompt(qp, kp, vp, zp, bias_p, tq=tq, heads=16)
    y_prompt = _out(x_prompt, mix_conv_p, mix_attn_p, w_out_b, gate[:bp], gf3, nb=1, t=512, tn=1024)

    bias_s =_bias_table(rel_bias[0], rows=ts, cols=r + V7X_LANES * (-(-ts // V7X_LANES)), rel0=r)
    to3 = lambda a: a.reshape(bs, ts, d_attn)
    mix_attn_s = _attn_sample(to3(qs), to3(ks), to3(vs), cache_k[0].reshape(bs, r * nh, HEAD_DIM),
                              cache_v[0].reshape(bs, r * nh, HEAD_DIM), to3(zs), bias_s, nbs=1)
    y_sample = _out(x_sample, mix_conv_s, mix_attn_s.reshape(bs * ts, d_attn), w_out_b, gate[bp:], gf3,
                    nb=16, t=ts, tn=1024)

    keep = CONV_WIDTH - 1
    new_k_prompt = k_keep[-1].reshape(1, bp, rows_kept, nh, HEAD_DIM)
    new_v_prompt = v_keep[-1].reshape(1, bp, rows_kept, nh, HEAD_DIM)
    new_conv_prompt = u_tail_p[-1, V7X_SUBLANES - keep:, :].reshape(1, bp, keep, d_conv)
    new_k_sample = ks.reshape(1, bs, ts, nh, HEAD_DIM)
    new_v_sample = vs.reshape(1, bs, ts, nh, HEAD_DIM)
    new_conv_sample = u_tail_s[:, V7X_SUBLANES - keep:, :].reshape(1, bs, keep, d_conv)
    return (y_prompt, y_sample, new_k_prompt, new_v_prompt, new_conv_prompt,
            new_k_sample, new_v_sample, new_conv_sample)
```

```python
import functools
import math

import jax
import jax.numpy as jnp
from jax import lax
from jax.experimental import pallas as pl
from jax.experimental.pallas import tpu as pltpu

CHUNK = 64
N_PAST_CHUNKS = 8
BAND_PAST = N_PAST_CHUNKS * CHUNK
HEAD_DIM = 128
CONV_WIDTH = 3
MAX_REL = 256
NORM_EPS = 1e-6
ATTN_SCALE = HEAD_DIM ** -0.5
LOG2E = math.log2(math.e)
N_PARTS = 4

V7X_LANES = 128
V7X_SUBLANES = 8
V7X_VMEM_BYTES = 64 * 1024 * 1024

F32 = jnp.float32
BF16 = jnp.bfloat16


def _nbytes(shape, dtype):
    n = 1
    for s in shape:
        n *= s
    return n * jnp.dtype(dtype).itemsize


def _params(semantics, blocks, extra_bytes=0):
    need = 2 * sum(_nbytes(s, d) for s, d in blocks) + extra_bytes + 16 * 1024 * 1024
    limit = int(min(need, V7X_VMEM_BYTES - 4 * 1024 * 1024))
    return pltpu.CompilerParams(dimension_semantics=semantics, vmem_limit_bytes=limit)


def _silu(z):
    return z * jax.nn.sigmoid(z)


def _mm(a, b):
    return jnp.dot(a, b, preferred_element_type=F32)


def _ada_kernel(c_ref, w_ref, b_ref, o_ref):
    o_ref[...] = _mm(c_ref[...].astype(BF16), w_ref[...].astype(BF16)) + b_ref[...]


def _ada(c, w_ada, b_ada, *, tn=512):
    r, d = c.shape
    n = w_ada.shape[1]
    assert n % tn == 0 and r % V7X_SUBLANES == 0
    blocks = [((r, d), F32), ((d, tn), F32), ((1, tn), F32), ((r, tn), F32), ((d, tn), BF16)]
    return pl.pallas_call(
        _ada_kernel,
        grid=(n // tn,),
        in_specs=[pl.BlockSpec((r, d), lambda j: (0, 0)),
                  pl.BlockSpec((d, tn), lambda j: (0, j)),
                  pl.BlockSpec((1, tn), lambda j: (0, j))],
        out_specs=pl.BlockSpec((r, tn), lambda j: (0, j)),
        out_shape=jax.ShapeDtypeStruct((r, n), F32),
        compiler_params=_params(("arbitrary",), blocks),
        name="ada",
    )(c, w_ada, b_ada)


def _modulated_norm(x, g, scale, shift):
    ms = jnp.mean(x * x, axis=-1, keepdims=True)
    xn = x * lax.rsqrt(ms + NORM_EPS)
    return (xn * g * (1.0 + scale) + shift).astype(BF16)


def _prep_kernel(x_ref, g_ref, scale_ref, shift_ref, h_ref):
    h_ref[...] = _modulated_norm(x_ref[...], g_ref[...], scale_ref[...], shift_ref[...])


def _prep(x, g, scale, shift, *, nb, t, s0=0, s_len=None):
    b, s, d = x.shape
    s_len = s if s_len is None else s_len
    assert b % nb == 0 and s_len % t == 0 and s0 % t == 0 and s0 + s_len <= s
    j0 = s0 // t
    blocks = [((nb, t, d), F32), ((nb, t, d), BF16), ((nb, t, d), F32)]
    return pl.pallas_call(
        _prep_kernel,
        grid=(b // nb, s_len // t),
        in_specs=[pl.BlockSpec((nb, t, d), lambda i, j: (i, j + j0, 0)),
                  pl.BlockSpec((1, 1, d), lambda i, j: (0, 0, 0)),
                  pl.BlockSpec((nb, 1, d), lambda i, j: (i, 0, 0)),
                  pl.BlockSpec((nb, 1, d), lambda i, j: (i, 0, 0))],
        out_specs=pl.BlockSpec((nb, t, d), lambda i, j: (i, j, 0)),
        out_shape=jax.ShapeDtypeStruct((b, s_len, d), BF16),
        compiler_params=_params(("arbitrary", "arbitrary"), blocks),
        name="prep",
    )(x, g, scale, shift)


def _weight_block_spec(d, tn, index_map):
    return pl.BlockSpec((None, d, N_PARTS * tn), lambda *ids: (index_map(*ids), 0, 0))


def _conv_epilogue(u, prev1, prev2, bg, z, cwb_ref):
    conv = cwb_ref[CONV_WIDTH:CONV_WIDTH + 1, :] + cwb_ref[0:1, :] * prev2
    conv = conv + cwb_ref[1:2, :] * prev1
    conv = conv + cwb_ref[2:3, :] * u
    return (bg * conv) * _silu(z)


def _inproj_prompt_kernel(x_ref, norm_ref, h0_ref, w_ref, cwb_ref, wo_ref,
                          mix_ref, tail_ref, q_ref, k_ref, v_ref, z_ref, k_new_ref, v_new_ref, wo_cast_ref,
                          h_buf, halo_ref, *, nbc, halves, tn):
    i = pl.program_id(0)
    j = pl.program_id(1)
    slot = lax.rem(i, 2)
    sm = x_ref.shape[0]
    hm = h_buf.shape[1] // halves

    @pl.when((i == 0) & (j == 0))
    def _():
        pltpu.sync_copy(h0_ref, h_buf.at[0])

    def prep_next_slab():
        h = _modulated_norm(x_ref[...], norm_ref[0:1, :], norm_ref[1:2, :], norm_ref[2:3, :])
        h_buf[1 - slot, pl.ds(pl.multiple_of(j * sm, sm), sm), :] = h

    def part(h, p):
        return _mm(h, w_ref[:, p * tn:(p + 1) * tn])

    def sub_tiles():
        wo_cast_ref[...] = wo_ref[...].astype(BF16)
        for s in range(halves):
            if s == halves - 1:
                prep_next_slab()
            yield slice(s * hm, (s + 1) * hm)

    @pl.when(j < nbc)
    def _conv_group():
        @pl.when(i == 0)
        def _():
            halo_ref[j] = jnp.zeros(halo_ref.shape[1:], F32)

        last8 = halo_ref[j]
        for rows in sub_tiles():
            h = h_buf[slot, rows, :]
            xin, bg, cg, z = part(h, 0), part(h, 1), part(h, 2), part(h, 3)
            u = cg * xin
            last1 = last8[V7X_SUBLANES - 1:V7X_SUBLANES, :]
            last2 = last8[V7X_SUBLANES - 2:V7X_SUBLANES - 1, :]
            row = lax.broadcasted_iota(jnp.int32, u.shape, 0)
            prev1 = jnp.where(row == 0, last1, pltpu.roll(u, 1, 0))
            prev2 = jnp.where(row == 0, last2, jnp.where(row == 1, last1, pltpu.roll(u, 2, 0)))
            mix_ref[rows, :] = _conv_epilogue(u, prev1, prev2, bg, z, cwb_ref).astype(BF16)
            last8 = u[hm - V7X_SUBLANES:, :]
        halo_ref[j] = last8
        tail_ref[0] = last8

    @pl.when(j >= nbc)
    def _attn_group():
        first_new = h_buf.shape[1] - k_new_ref.shape[0]
        for rows in sub_tiles():
            h = h_buf[slot, rows, :]
            k, v = part(h, 1), part(h, 2)
            q_ref[rows, :] = part(h, 0).astype(BF16)
            k_ref[rows, :] = k.astype(BF16)
            v_ref[rows, :] = v.astype(BF16)
            z_ref[rows, :] = part(h, 3)
            if rows.start >= first_new:
                new_rows = slice(rows.start - first_new, rows.stop - first_new)
                k_new_ref[new_rows, :] = k
                v_new_ref[new_rows, :] = v


def _inproj_prompt(x, norm, h0, w_blk, conv_wb, w_out, *, d_conv, d_attn, tm, tn, halves, n_new):
    s, d = x.shape
    assert s % tm == 0 and d_conv % tn == 0 and d_attn % tn == 0 and (tm // halves) % V7X_SUBLANES == 0
    nbc, nba = d_conv // tn, d_attn // tn
    nsteps = nbc + nba
    nrt = s // tm
    hm = tm // halves
    sm = tm // nsteps
    assert tm % nsteps == 0 and sm % (2 * V7X_SUBLANES) == 0 and h0.shape == (tm, d)
    assert n_new % hm == 0 and n_new <= tm, "the kept K/V rows are whole sub-tiles of the last row tile"
    cj = lambda j: jnp.minimum(j, nbc - 1)
    aj = lambda j: jnp.maximum(j - nbc, 0)
    attn_spec = pl.BlockSpec((None, tm, tn), lambda i, j: (aj(j), i, 0))
    newest_spec = pl.BlockSpec((None, None, n_new, tn), lambda i, j: (i, aj(j), 0, 0))
    wo_rows, wo_cols = w_out.shape
    ro = wo_rows // (nrt * nsteps)
    assert wo_rows % (nrt * nsteps) == 0 and ro % (2 * V7X_SUBLANES) == 0
    wo_spec = pl.BlockSpec((ro, wo_cols), lambda i, j: (i * nsteps + j, 0))
    blocks = ([((sm, d), F32), ((d, N_PARTS * tn), BF16)] + [((tm, tn), BF16)] * 4 + [((tm, tn), F32)]
              + [((n_new, tn), F32)] * 2 + [((hm, tn), F32)] * 8 + [((ro, wo_cols), F32)] * 2)
    return pl.pallas_call(
        functools.partial(_inproj_prompt_kernel, nbc=nbc, halves=halves, tn=tn),
        grid=(nrt, nsteps),
        in_specs=[pl.BlockSpec((sm, d), lambda i, j: (jnp.minimum(i + 1, nrt - 1) * nsteps + j, 0)),
                  pl.BlockSpec(norm.shape, lambda i, j: (0, 0)),
                  pl.BlockSpec(memory_space=pl.ANY),
                  _weight_block_spec(d, tn, lambda i, j: j),
                  pl.BlockSpec((CONV_WIDTH + 1, tn), lambda i, j: (0, cj(j))),
                  wo_spec],
        out_specs=[pl.BlockSpec((tm, tn), lambda i, j: (i, cj(j))),
                   pl.BlockSpec((1, V7X_SUBLANES, tn), lambda i, j: (i, 0, cj(j))),
                   attn_spec, attn_spec, attn_spec, attn_spec, newest_spec, newest_spec, wo_spec],
        out_shape=[jax.ShapeDtypeStruct((s, d_conv), BF16),
                   jax.ShapeDtypeStruct((nrt, V7X_SUBLANES, d_conv), F32),
                   jax.ShapeDtypeStruct((nba, s, tn), BF16),
                   jax.ShapeDtypeStruct((nba, s, tn), BF16),
                   jax.ShapeDtypeStruct((nba, s, tn), BF16),
                   jax.ShapeDtypeStruct((nba, s, tn), F32),
                   jax.ShapeDtypeStruct((nrt, nba, n_new, tn), F32),
                   jax.ShapeDtypeStruct((nrt, nba, n_new, tn), F32),
                   jax.ShapeDtypeStruct(w_out.shape, BF16)],
        scratch_shapes=[pltpu.VMEM((2, tm, d), BF16), pltpu.VMEM((nbc, V7X_SUBLANES, tn), F32)],
        compiler_params=_params(("arbitrary", "arbitrary"), blocks, extra_bytes=_nbytes((2, tm, d), BF16)),
        name="inproj_prompt",
    )(x, norm, h0, w_blk, conv_wb, w_out)


def _inproj_sample_kernel(h_ref, w0_ref, w1_ref, w2_ref, w3_ref, cwb_ref, cache_ref,
                          mix_ref, tail_ref, q_ref, k_ref, v_ref, z_ref, wblk_ref, *, nb, t, tn, n_conv):
    j = pl.program_id(0)
    tc = w0_ref.shape[1]
    piece = lax.rem(j, tn // tc)
    w = [w_ref[...].astype(BF16) for w_ref in (w0_ref, w1_ref, w2_ref, w3_ref)]
    for p in range(N_PARTS):
        wblk_ref[:, pl.ds(pl.multiple_of(p * tn + piece * tc, tc), tc)] = w[p]
    acc = _mm(h_ref[...], jnp.concatenate(w, axis=1))
    p0, p1, p2, p3 = (acc[:, p * tc:(p + 1) * tc] for p in range(N_PARTS))

    @pl.when(j < n_conv)
    def _conv_group():
        u = p2 * p0
        cache = cache_ref[...]
        c2 = jnp.broadcast_to(cache[:, 0:1, :], (nb, t, tc)).reshape(nb * t, tc)
        c1 = jnp.broadcast_to(cache[:, 1:2, :], (nb, t, tc)).reshape(nb * t, tc)
        pos = lax.broadcasted_iota(jnp.int32, (nb, t, tc), 1).reshape(nb * t, tc)
        prev1 = jnp.where(pos == 0, c1, pltpu.roll(u, 1, 0))
        prev2 = jnp.where(pos == 0, c2, jnp.where(pos == 1, c1, pltpu.roll(u, 2, 0)))
        mix_ref[...] = _conv_epilogue(u, prev1, prev2, p1, p3, cwb_ref).astype(BF16)
        tail_ref[...] = u.reshape(nb, t, tc)[:, t - V7X_SUBLANES:, :]

    @pl.when(j >= n_conv)
    def _attn_group():
        q_ref[...] = p0
        k_ref[...] = p1
        v_ref[...] = p2
        z_ref[...] = p3


def _inproj_sample(h, w_in, conv_wb, cache_conv, *, d_conv, d_attn, nb, t, tn, tc):
    m, d = h.shape
    assert m == nb * t and t % V7X_SUBLANES == 0 and t >= CONV_WIDTH - 1
    assert tn % tc == 0 and d_conv % tn == 0 and d_attn % tn == 0 and tc % V7X_LANES == 0
    n_conv, n_attn = d_conv // tc, d_attn // tc

    def wspec(p):
        src = lambda j: jnp.where(j < n_conv, p * n_conv + j, N_PARTS * n_conv + p * n_attn + j - n_conv)
        return pl.BlockSpec((d, tc), lambda j: (0, src(j)))

    cj = lambda j: jnp.minimum(j, n_conv - 1)
    aj = lambda j: jnp.maximum(j - n_conv, 0)
    attn_spec = pl.BlockSpec((m, tc), lambda j: (0, aj(j)))
    blocks = ([((d, tc), F32)] * N_PARTS + [((d, N_PARTS * tn), BF16)] + [((m, tc), F32)] * 6
              + [((d, N_PARTS * tc), BF16)] * 2)
    n_blocks = (d_conv + d_attn) // tn
    return pl.pallas_call(
        functools.partial(_inproj_sample_kernel, nb=nb, t=t, tn=tn, n_conv=n_conv),
        grid=(n_conv + n_attn,),
        in_specs=[pl.BlockSpec((m, d), lambda j: (0, 0), pipeline_mode=pl.Buffered(1)),
                  wspec(0), wspec(1), wspec(2), wspec(3),
                  pl.BlockSpec((CONV_WIDTH + 1, tc), lambda j: (0, cj(j))),
                  pl.BlockSpec((nb, CONV_WIDTH - 1, tc), lambda j: (0, 0, cj(j)))],
        out_specs=[pl.BlockSpec((m, tc), lambda j: (0, cj(j))),
                   pl.BlockSpec((nb, V7X_SUBLANES, tc), lambda j: (0, 0, cj(j))),
                   attn_spec, attn_spec, attn_spec, attn_spec,
                   pl.BlockSpec((None, d, N_PARTS * tn), lambda j: (j // (tn // tc), 0, 0))],
        out_shape=[jax.ShapeDtypeStruct((m, d_conv), BF16),
                   jax.ShapeDtypeStruct((nb, V7X_SUBLANES, d_conv), F32),
                   jax.ShapeDtypeStruct((m, d_attn), F32),
                   jax.ShapeDtypeStruct((m, d_attn), F32),
                   jax.ShapeDtypeStruct((m, d_attn), F32),
                   jax.ShapeDtypeStruct((m, d_attn), F32),
                   jax.ShapeDtypeStruct((n_blocks, d, N_PARTS * tn), BF16)],
        compiler_params=_params(("arbitrary",), blocks, extra_bytes=_nbytes((m, d), BF16)),
        name="inproj_sample",
    )(h, w_in, w_in, w_in, w_in, conv_wb, cache_conv)


def _bias_seq(rel_bias, offset, n):
    rev = rel_bias[:, ::-1]
    left = offset - MAX_REL
    assert left >= 0
    right = max(n - left - rev.shape[1], 0)
    return jnp.pad(rev, ((0, 0), (left, right)), mode="edge")[:, :n]


def _bias_table_kernel(seq_ref, o_ref, *, rows, cols, lane0, band_cols):
    width = seq_ref.shape[-1]
    seq = jnp.broadcast_to(seq_ref[0], (rows, width))
    table = pltpu.roll(seq, width - lane0, 1, stride=1, stride_axis=0)[:, :cols] * LOG2E
    if band_cols is not None:
        qc = lax.broadcasted_iota(jnp.int32, (rows, cols), 0) // CHUNK
        kc = lax.broadcasted_iota(jnp.int32, (rows, cols), 1) // CHUNK - band_cols // CHUNK
        table = jnp.where((kc <= qc) & (kc >= qc - N_PAST_CHUNKS), table, -jnp.inf)
    o_ref[0] = table


def _bias_table(rel_bias, *, rows, cols, rel0, band_cols=None):
    nh = rel_bias.shape[0]
    lane0 = V7X_LANES * (-(-rows // V7X_LANES))
    width = 1 << (lane0 + cols - 1).bit_length()
    seq = _bias_seq(rel_bias, rel0 + lane0, width)
    blocks = [((1, width), F32), ((rows, cols), F32), ((rows, width), F32), ((rows, width), F32)]
    return pl.pallas_call(
        functools.partial(_bias_table_kernel, rows=rows, cols=cols, lane0=lane0, band_cols=band_cols),
        grid=(nh,),
        in_specs=[pl.BlockSpec((1, 1, width), lambda h: (h, 0, 0))],
        out_specs=pl.BlockSpec((1, rows, cols), lambda h: (h, 0, 0)),
        out_shape=jax.ShapeDtypeStruct((nh, rows, cols), F32),
        compiler_params=_params(("arbitrary",), blocks),
        name="bias_table",
    )(seq.reshape(nh, 1, width))


def _attn_prompt_kernel(q_ref, k0_ref, k1_ref, k2_ref, v0_ref, v1_ref, v2_ref, z_ref, bias_ref, o_ref,
                        *, heads, tq):
    b = pl.program_id(1)
    nk = 3 * tq
    tn = q_ref.shape[-1]
    dims = (((1,), (1,)), ((), ()))

    def run(mask_keys):
        if mask_keys:
            kc = lax.broadcasted_iota(jnp.int32, (1, nk), 1) // CHUNK
            valid = kc >= (2 - b) * (tq // CHUNK)
        for hh in range(heads):
            blk, off = divmod(hh * HEAD_DIM, tn)
            head = (blk, slice(None), slice(off, off + HEAD_DIM))
            k = jnp.concatenate([k0_ref[head], k1_ref[head], k2_ref[head]], axis=0)
            v = jnp.concatenate([v0_ref[head], v1_ref[head], v2_ref[head]], axis=0)
            s = lax.dot_general(q_ref[head], k, dims, preferred_element_type=F32)
            s = s * (ATTN_SCALE * LOG2E) + bias_ref[hh]
            if mask_keys:
                s = jnp.where(valid, s, -jnp.inf)
            m = jnp.max(s, axis=-1, keepdims=True)
            p = jnp.exp2(s - m)
            l = jnp.sum(p, axis=-1, keepdims=True)
            o = _mm(p.astype(BF16), v) / l
            o_ref[:, hh * HEAD_DIM:(hh + 1) * HEAD_DIM] = (o * _silu(z_ref[head])).astype(BF16)

    pl.when(b < 2)(lambda: run(True))
    pl.when(b >= 2)(lambda: run(False))


def _attn_prompt(q, k, v, z, bias, *, tq, heads):
    nblk, s, tn = q.shape
    da = nblk * tn
    nh = da // HEAD_DIM
    wcols = heads * HEAD_DIM
    assert s % tq == 0 and nh % heads == 0 and tq % CHUNK == 0 and 2 * tq >= BAND_PAST
    assert tn % HEAD_DIM == 0 and wcols % tn == 0
    gb = wcols // tn
    kspec = lambda back: pl.BlockSpec((gb, tq, tn), lambda g, b, back=back: (g, jnp.maximum(b - back, 0), 0))
    blocks = ([((tq, wcols), BF16)] * 8 + [((tq, wcols), F32)] + [((heads, tq, 3 * tq), F32)]
              + [((tq, 3 * tq), F32)] * 8)
    return pl.pallas_call(
        functools.partial(_attn_prompt_kernel, heads=heads, tq=tq),
        grid=(nh // heads, s // tq),
        in_specs=[pl.BlockSpec((gb, tq, tn), lambda g, b: (g, b, 0)),
                  kspec(2), kspec(1), kspec(0), kspec(2), kspec(1), kspec(0),
                  pl.BlockSpec((gb, tq, tn), lambda g, b: (g, b, 0)),
                  pl.BlockSpec((heads, tq, 3 * tq), lambda g, b: (g, 0, 0))],
        out_specs=pl.BlockSpec((tq, wcols), lambda g, b: (b, g)),
        out_shape=jax.ShapeDtypeStruct((s, da), BF16),
        compiler_params=_params(("arbitrary", "arbitrary"), blocks),
        name="attn_prompt",
    )(q, k, k, k, v, v, v, z, bias)


def _attn_sample_kernel(q_ref, kn_ref, vn_ref, ck_ref, cv_ref, z_ref, bias_ref, o_ref, *, nh, r, t):
    dims = (((1,), (1,)), ((), ()))
    for bb in range(q_ref.shape[0]):
        for hh in range(nh):
            sl = slice(hh * HEAD_DIM, (hh + 1) * HEAD_DIM)
            head_rows = pl.ds(hh, r, stride=nh)
            q = q_ref[bb, :, sl].astype(BF16)
            sc = lax.dot_general(q, ck_ref[bb, head_rows, :].astype(BF16), dims, preferred_element_type=F32)
            sn = lax.dot_general(q, kn_ref[bb, :, sl].astype(BF16), dims, preferred_element_type=F32)
            sc = sc * (ATTN_SCALE * LOG2E) + bias_ref[hh, :, 0:r]
            sn = sn * (ATTN_SCALE * LOG2E) + bias_ref[hh, :, r:r + t]
            m = jnp.maximum(jnp.max(sc, axis=-1, keepdims=True), jnp.max(sn, axis=-1, keepdims=True))
            pc = jnp.exp2(sc - m)
            pn = jnp.exp2(sn - m)
            l = jnp.sum(pc, axis=-1, keepdims=True) + jnp.sum(pn, axis=-1, keepdims=True)
            o = _mm(pc.astype(BF16), cv_ref[bb, head_rows, :].astype(BF16))
            o = o + _mm(pn.astype(BF16), vn_ref[bb, :, sl].astype(BF16))
            o_ref[bb, :, sl] = ((o / l) * _silu(z_ref[bb, :, sl])).astype(BF16)


def _attn_sample(q, kn, vn, cache_k, cache_v, z, bias, *, nbs):
    nb, t, da = q.shape
    nh = da // HEAD_DIM
    r = cache_k.shape[1] // nh
    assert r % V7X_LANES == 0 and bias.shape[2] >= r + t and nb % nbs == 0
    new = pl.BlockSpec((nbs, t, da), lambda b: (b, 0, 0))
    old = pl.BlockSpec((nbs, r * nh, HEAD_DIM), lambda b: (b, 0, 0))
    blocks = ([((nbs, t, da), F32)] * 5 + [((nbs, r, da), F32)] * 2 + [(bias.shape, F32)]
              + [((nbs, r, da), BF16)] * 2)
    return pl.pallas_call(
        functools.partial(_attn_sample_kernel, nh=nh, r=r, t=t),
        grid=(nb // nbs,),
        in_specs=[new, new, new, old, old, new, pl.BlockSpec(bias.shape, lambda b: (0, 0, 0))],
        out_specs=new,
        out_shape=jax.ShapeDtypeStruct((nb, t, da), BF16),
        compiler_params=_params(("arbitrary",), blocks),
        name="attn_sample",
    )(q, kn, vn, cache_k, cache_v, z, bias)


def _out_kernel(x_ref, ma_ref, mb_ref, wa_ref, wb_ref, gate_ref, gf_ref, y_ref, ss_ref, *, nj, tn):
    j = pl.program_id(1)
    acc = _mm(ma_ref[...], wa_ref[...]) + _mm(mb_ref[...], wb_ref[...])
    nb, t, _ = x_ref.shape
    res = x_ref[...] + gate_ref[...] * acc.reshape(nb, t, tn)
    y_ref[:, :, pl.ds(pl.multiple_of(j * tn, tn), tn)] = res
    part = jnp.sum(res * res, axis=-1, keepdims=True)

    @pl.when(j == 0)
    def _():
        ss_ref[...] = part

    @pl.when(j > 0)
    def _():
        ss_ref[...] += part

    @pl.when(j == nj - 1)
    def _():
        inv = lax.rsqrt(ss_ref[...] * (1.0 / (nj * tn)) + NORM_EPS)
        for n in range(nj):
            cols = slice(n * tn, (n + 1) * tn)
            y_ref[:, :, cols] = (y_ref[:, :, cols] * inv) * gf_ref[:, :, cols]


def _out(x, mix_a, mix_b, w_out, gate, g_final, *, nb, t, tn):
    b, s, d = x.shape
    half = mix_a.shape[1]
    tm = nb * t
    assert b % nb == 0 and s % t == 0 and d % tn == 0 and w_out.shape[0] == 2 * half
    nj = d // tn
    nt = s // t
    assert nb == 1 or nt == 1, "a row tile must be contiguous in the flattened (B*S) mixed rows"
    blocks = ([((nb, t, tn), F32)] + [((tm, half), BF16)] * 2 + [((half, tn), BF16)] * 2
              + [((nb, t, d), F32)] + [((tm, tn), F32)] * 4)
    return pl.pallas_call(
        functools.partial(_out_kernel, nj=nj, tn=tn),
        grid=((b // nb) * nt, nj),
        in_specs=[pl.BlockSpec((nb, t, tn), lambda i, j: (i // nt, i % nt, j)),
                  pl.BlockSpec((tm, half), lambda i, j: (i, 0)),
                  pl.BlockSpec((tm, half), lambda i, j: (i, 0)),
                  pl.BlockSpec((half, tn), lambda i, j: (0, j)),
                  pl.BlockSpec((half, tn), lambda i, j: (1, j)),
                  pl.BlockSpec((nb, 1, tn), lambda i, j: (i // nt, 0, j)),
                  pl.BlockSpec((1, 1, d), lambda i, j: (0, 0, 0))],
        out_specs=pl.BlockSpec((nb, t, d), lambda i, j: (i // nt, i % nt, 0)),
        out_shape=jax.ShapeDtypeStruct((b, s, d), F32),
        scratch_shapes=[pltpu.VMEM((nb, t, 1), F32)],
        compiler_params=_params(("arbitrary", "arbitrary"), blocks),
        name="out_proj",
    )(x, mix_a, mix_b, w_out, w_out, gate, g_final)


def kernel(x_prompt, x_sample, cache_k, cache_v, cache_conv, c_prompt, c_sample,
           g_norm, w_ada, b_ada, w_in, conv_w, conv_b, rel_bias, w_out, g_final):
    depth = g_norm.shape[0]
    assert depth == 1, "single-layer trunk"
    bp, sp, d = x_prompt.shape
    bs, ts, _ = x_sample.shape
    assert bp == 1
    d_conv = conv_w.shape[-1]
    d_attn = w_out.shape[1] - d_conv
    nh = d_attn // HEAD_DIM
    r = cache_k.shape[2]
    rows_kept = min(BAND_PAST, sp)

    n_c = bp + bs
    pad = (-n_c) % V7X_SUBLANES
    c_all = jnp.concatenate([c_prompt, c_sample, jnp.zeros((pad, d), F32)], axis=0)
    mod = _ada(c_all, w_ada[0], b_ada)
    shift, scale, gate = (mod[:n_c, i * d:(i + 1) * d].reshape(n_c, 1, d) for i in range(3))

    tn = 256
    g3 = g_norm.reshape(1, 1, d)
    gf3 = g_final.reshape(1, 1, d)
    conv_wb = jnp.concatenate([conv_w[0], conv_b], axis=0)

    hs = _prep(x_sample, g3, scale[bp:], shift[bp:], nb=8, t=ts).reshape(bs * ts, d)
    mix_conv_s, u_tail_s, qs, ks, vs, zs, w_blk = _inproj_sample(
        hs, w_in[0], conv_wb, cache_conv[0], d_conv=d_conv, d_attn=d_attn, nb=bs, t=ts, tn=tn, tc=V7X_LANES)

    tm, tq, halves = 1024, 256, 2
    h_first = _prep(x_prompt, g3, scale[:bp], shift[:bp], nb=1, t=512, s0=0, s_len=tm).reshape(tm, d)
    norm_p = jnp.concatenate([g_norm, scale[0], shift[0]], axis=0)
    mix_conv_p, u_tail_p, qp, kp, vp, zp, k_keep, v_keep, w_out_b = _inproj_prompt(
        x_prompt.reshape(sp, d), norm_p, h_first, w_blk, conv_wb, w_out[0],
        d_conv=d_conv, d_attn=d_attn, tm=tm, tn=tn, halves=halves, n_new=rows_kept)
    bias_p = _bias_table(rel_bias[0], rows=tq, cols=3 * tq, rel0=2 * tq, band_cols=2 * tq)
    mix_attn_p = _attn_prompt(qp, kp, vp, zp, bias_p, tq=tq, heads=16)
    y_prompt = _out(x_prompt, mix_conv_p, mix_attn_p, w_out_b, gate[:bp], gf3, nb=1, t=512, tn=1024)

    bias_s =_bias_table(rel_bias[0], rows=ts, cols=r + V7X_LANES * (-(-ts // V7X_LANES)), rel0=r)
    to3 = lambda a: a.reshape(bs, ts, d_attn)
    mix_attn_s = _attn_sample(to3(qs), to3(ks), to3(vs), cache_k[0].reshape(bs, r * nh, HEAD_DIM),
                              cache_v[0].reshape(bs, r * nh, HEAD_DIM), to3(zs), bias_s, nbs=1)
    y_sample = _out(x_sample, mix_conv_s, mix_attn_s.reshape(bs * ts, d_attn), w_out_b, gate[bp:], gf3,
                    nb=16, t=ts, tn=1024)

    keep = CONV_WIDTH - 1
    rows_major = lambda a: jnp.swapaxes(a[-1], 0, 1)
    new_k_prompt = rows_major(k_keep).reshape(1, bp, rows_kept, nh, HEAD_DIM)
    new_v_prompt = rows_major(v_keep).reshape(1, bp, rows_kept, nh, HEAD_DIM)
    new_conv_prompt = u_tail_p[-1, V7X_SUBLANES - keep:, :].reshape(1, bp, keep, d_conv)
    new_k_sample = ks.reshape(1, bs, ts, nh, HEAD_DIM)
    new_v_sample = vs.reshape(1, bs, ts, nh, HEAD_DIM)
    new_conv_sample = u_tail_s[:, V7X_SUBLANES - keep:, :].reshape(1, bs, keep, d_conv)
    return (y_prompt, y_sample, new_k_prompt, new_v_prompt, new_conv_prompt,
            new_k_sample, new_v_sample, new_conv_sample)
```

```python
import functools
import math

import jax
import jax.numpy as jnp
from jax import lax
from jax.experimental import pallas as pl
from jax.experimental.pallas import tpu as pltpu

CHUNK = 64
N_PAST_CHUNKS = 8
BAND_PAST = N_PAST_CHUNKS * CHUNK
HEAD_DIM = 128
CONV_WIDTH = 3
MAX_REL = 256
NORM_EPS = 1e-6
ATTN_SCALE = HEAD_DIM ** -0.5
LOG2E = math.log2(math.e)
N_PARTS = 4

V7X_LANES = 128
V7X_SUBLANES = 8
V7X_VMEM_BYTES = 64 * 1024 * 1024

F32 = jnp.float32
BF16 = jnp.bfloat16


def _nbytes(shape, dtype):
    n = 1
    for s in shape:
        n *= s
    return n * jnp.dtype(dtype).itemsize


def _params(semantics, blocks, extra_bytes=0):
    need = 2 * sum(_nbytes(s, d) for s, d in blocks) + extra_bytes + 16 * 1024 * 1024
    limit = int(min(need, V7X_VMEM_BYTES - 4 * 1024 * 1024))
    return pltpu.CompilerParams(dimension_semantics=semantics, vmem_limit_bytes=limit)


def _silu(z):
    return z * jax.nn.sigmoid(z)


def _mm(a, b):
    return jnp.dot(a, b, preferred_element_type=F32)


def _ada_kernel(c_ref, w_ref, b_ref, o_ref):
    o_ref[...] = _mm(c_ref[...].astype(BF16), w_ref[...].astype(BF16)) + b_ref[...]


def _ada(c, w_ada, b_ada, *, tn=512):
    r, d = c.shape
    n = w_ada.shape[1]
    assert n % tn == 0 and r % V7X_SUBLANES == 0
    blocks = [((r, d), F32), ((d, tn), F32), ((1, tn), F32), ((r, tn), F32), ((d, tn), BF16)]
    return pl.pallas_call(
        _ada_kernel,
        grid=(n // tn,),
        in_specs=[pl.BlockSpec((r, d), lambda j: (0, 0)),
                  pl.BlockSpec((d, tn), lambda j: (0, j)),
                  pl.BlockSpec((1, tn), lambda j: (0, j))],
        out_specs=pl.BlockSpec((r, tn), lambda j: (0, j)),
        out_shape=jax.ShapeDtypeStruct((r, n), F32),
        compiler_params=_params(("arbitrary",), blocks),
        name="ada",
    )(c, w_ada, b_ada)


def _modulated_norm(x, g, scale, shift):
    ms = jnp.mean(x * x, axis=-1, keepdims=True)
    xn = x * lax.rsqrt(ms + NORM_EPS)
    return (xn * g * (1.0 + scale) + shift).astype(BF16)


def _prep_kernel(x_ref, g_ref, scale_ref, shift_ref, h_ref):
    h_ref[...] = _modulated_norm(x_ref[...], g_ref[...], scale_ref[...], shift_ref[...])


def _prep(x, g, scale, shift, *, nb, t, s0=0, s_len=None):
    b, s, d = x.shape
    s_len = s if s_len is None else s_len
    assert b % nb == 0 and s_len % t == 0 and s0 % t == 0 and s0 + s_len <= s
    j0 = s0 // t
    blocks = [((nb, t, d), F32), ((nb, t, d), BF16), ((nb, t, d), F32)]
    return pl.pallas_call(
        _prep_kernel,
        grid=(b // nb, s_len // t),
        in_specs=[pl.BlockSpec((nb, t, d), lambda i, j: (i, j + j0, 0)),
                  pl.BlockSpec((1, 1, d), lambda i, j: (0, 0, 0)),
                  pl.BlockSpec((nb, 1, d), lambda i, j: (i, 0, 0)),
                  pl.BlockSpec((nb, 1, d), lambda i, j: (i, 0, 0))],
        out_specs=pl.BlockSpec((nb, t, d), lambda i, j: (i, j, 0)),
        out_shape=jax.ShapeDtypeStruct((b, s_len, d), BF16),
        compiler_params=_params(("arbitrary", "arbitrary"), blocks),
        name="prep",
    )(x, g, scale, shift)


def _weight_block_spec(d, tn, index_map):
    return pl.BlockSpec((None, d, N_PARTS * tn), lambda *ids: (index_map(*ids), 0, 0))


def _conv_epilogue(u, prev1, prev2, bg, z, cwb_ref):
    conv = cwb_ref[CONV_WIDTH:CONV_WIDTH + 1, :] + cwb_ref[0:1, :] * prev2
    conv = conv + cwb_ref[1:2, :] * prev1
    conv = conv + cwb_ref[2:3, :] * u
    return (bg * conv) * _silu(z)


def _inproj_prompt_kernel(x_ref, norm_ref, h0_ref, w_ref, cwb_ref, wo_ref,
                          mix_ref, tail_ref, q_ref, k_ref, v_ref, z_ref, k_new_ref, v_new_ref, wo_cast_ref,
                          h_buf, halo_ref, *, nbc, halves, tn):
    i = pl.program_id(0)
    j = pl.program_id(1)
    slot = lax.rem(i, 2)
    sm = x_ref.shape[0]
    hm = h_buf.shape[1] // halves

    @pl.when((i == 0) & (j == 0))
    def _():
        pltpu.sync_copy(h0_ref, h_buf.at[0])

    def prep_next_slab():
        h = _modulated_norm(x_ref[...], norm_ref[0:1, :], norm_ref[1:2, :], norm_ref[2:3, :])
        h_buf[1 - slot, pl.ds(pl.multiple_of(j * sm, sm), sm), :] = h

    def part(h, p):
        return _mm(h, w_ref[:, p * tn:(p + 1) * tn])

    def sub_tiles():
        wo_cast_ref[...] = wo_ref[...].astype(BF16)
        for s in range(halves):
            if s == halves - 1:
                prep_next_slab()
            yield slice(s * hm, (s + 1) * hm)

    @pl.when(j < nbc)
    def _conv_group():
        @pl.when(i == 0)
        def _():
            halo_ref[j] = jnp.zeros(halo_ref.shape[1:], F32)

        last8 = halo_ref[j]
        for rows in sub_tiles():
            h = h_buf[slot, rows, :]
            xin, bg, cg, z = part(h, 0), part(h, 1), part(h, 2), part(h, 3)
            u = cg * xin
            last1 = last8[V7X_SUBLANES - 1:V7X_SUBLANES, :]
            last2 = last8[V7X_SUBLANES - 2:V7X_SUBLANES - 1, :]
            row = lax.broadcasted_iota(jnp.int32, u.shape, 0)
            prev1 = jnp.where(row == 0, last1, pltpu.roll(u, 1, 0))
            prev2 = jnp.where(row == 0, last2, jnp.where(row == 1, last1, pltpu.roll(u, 2, 0)))
            mix_ref[rows, :] = _conv_epilogue(u, prev1, prev2, bg, z, cwb_ref).astype(BF16)
            last8 = u[hm - V7X_SUBLANES:, :]
        halo_ref[j] = last8
        tail_ref[0] = last8

    @pl.when(j >= nbc)
    def _attn_group():
        first_new = h_buf.shape[1] - k_new_ref.shape[0]
        for rows in sub_tiles():
            h = h_buf[slot, rows, :]
            k, v = part(h, 1), part(h, 2)
            q_ref[rows, :] = part(h, 0).astype(BF16)
            k_ref[rows, :] = k.astype(BF16)
            v_ref[rows, :] = v.astype(BF16)
            z_ref[rows, :] = part(h, 3)
            if rows.start >= first_new:
                new_rows = slice(rows.start - first_new, rows.stop - first_new)
                k_new_ref[new_rows, :] = k
                v_new_ref[new_rows, :] = v


def _inproj_prompt(x, norm, h0, w_blk, conv_wb, w_out, *, d_conv, d_attn, tm, tn, halves, n_new):
    s, d = x.shape
    assert s % tm == 0 and d_conv % tn == 0 and d_attn % tn == 0 and (tm // halves) % V7X_SUBLANES == 0
    nbc, nba = d_conv // tn, d_attn // tn
    nsteps = nbc + nba
    nrt = s // tm
    hm = tm // halves
    sm = tm // nsteps
    assert tm % nsteps == 0 and sm % (2 * V7X_SUBLANES) == 0 and h0.shape == (tm, d)
    assert n_new % hm == 0 and n_new <= tm, "the kept K/V rows are whole sub-tiles of the last row tile"
    cj = lambda j: jnp.minimum(j, nbc - 1)
    aj = lambda j: jnp.maximum(j - nbc, 0)
    attn_spec = pl.BlockSpec((None, tm, tn), lambda i, j: (aj(j), i, 0))
    newest_spec = pl.BlockSpec((None, None, n_new, tn), lambda i, j: (i, aj(j), 0, 0))
    wo_rows, wo_cols = w_out.shape
    ro = wo_rows // (nrt * nsteps)
    assert wo_rows % (nrt * nsteps) == 0 and ro % (2 * V7X_SUBLANES) == 0
    wo_spec = pl.BlockSpec((ro, wo_cols), lambda i, j: (i * nsteps + j, 0))
    blocks = ([((sm, d), F32), ((d, N_PARTS * tn), BF16)] + [((tm, tn), BF16)] * 4 + [((tm, tn), F32)]
              + [((n_new, tn), F32)] * 2 + [((hm, tn), F32)] * 8 + [((ro, wo_cols), F32)] * 2)
    return pl.pallas_call(
        functools.partial(_inproj_prompt_kernel, nbc=nbc, halves=halves, tn=tn),
        grid=(nrt, nsteps),
        in_specs=[pl.BlockSpec((sm, d), lambda i, j: (jnp.minimum(i + 1, nrt - 1) * nsteps + j, 0)),
                  pl.BlockSpec(norm.shape, lambda i, j: (0, 0)),
                  pl.BlockSpec(memory_space=pl.ANY),
                  _weight_block_spec(d, tn, lambda i, j: j),
                  pl.BlockSpec((CONV_WIDTH + 1, tn), lambda i, j: (0, cj(j))),
                  wo_spec],
        out_specs=[pl.BlockSpec((tm, tn), lambda i, j: (i, cj(j))),
                   pl.BlockSpec((1, V7X_SUBLANES, tn), lambda i, j: (i, 0, cj(j))),
                   attn_spec, attn_spec, attn_spec, attn_spec, newest_spec, newest_spec, wo_spec],
        out_shape=[jax.ShapeDtypeStruct((s, d_conv), BF16),
                   jax.ShapeDtypeStruct((nrt, V7X_SUBLANES, d_conv), F32),
                   jax.ShapeDtypeStruct((nba, s, tn), BF16),
                   jax.ShapeDtypeStruct((nba, s, tn), BF16),
                   jax.ShapeDtypeStruct((nba, s, tn), BF16),
                   jax.ShapeDtypeStruct((nba, s, tn), F32),
                   jax.ShapeDtypeStruct((nrt, nba, n_new, tn), F32),
                   jax.ShapeDtypeStruct((nrt, nba, n_new, tn), F32),
                   jax.ShapeDtypeStruct(w_out.shape, BF16)],
        scratch_shapes=[pltpu.VMEM((2, tm, d), BF16), pltpu.VMEM((nbc, V7X_SUBLANES, tn), F32)],
        compiler_params=_params(("arbitrary", "arbitrary"), blocks, extra_bytes=_nbytes((2, tm, d), BF16)),
        name="inproj_prompt",
    )(x, norm, h0, w_blk, conv_wb, w_out)


def _inproj_sample_kernel(h_ref, w0_ref, w1_ref, w2_ref, w3_ref, cwb_ref, cache_ref,
                          mix_ref, tail_ref, q_ref, k_ref, v_ref, z_ref, wblk_ref, *, nb, t, tn, n_conv):
    j = pl.program_id(0)
    tc = w0_ref.shape[1]
    piece = lax.rem(j, tn // tc)
    w = [w_ref[...].astype(BF16) for w_ref in (w0_ref, w1_ref, w2_ref, w3_ref)]
    for p in range(N_PARTS):
        wblk_ref[:, pl.ds(pl.multiple_of(p * tn + piece * tc, tc), tc)] = w[p]
    acc = _mm(h_ref[...], jnp.concatenate(w, axis=1))
    p0, p1, p2, p3 = (acc[:, p * tc:(p + 1) * tc] for p in range(N_PARTS))

    @pl.when(j < n_conv)
    def _conv_group():
        u = p2 * p0
        cache = cache_ref[...]
        c2 = jnp.broadcast_to(cache[:, 0:1, :], (nb, t, tc)).reshape(nb * t, tc)
        c1 = jnp.broadcast_to(cache[:, 1:2, :], (nb, t, tc)).reshape(nb * t, tc)
        pos = lax.broadcasted_iota(jnp.int32, (nb, t, tc), 1).reshape(nb * t, tc)
        prev1 = jnp.where(pos == 0, c1, pltpu.roll(u, 1, 0))
        prev2 = jnp.where(pos == 0, c2, jnp.where(pos == 1, c1, pltpu.roll(u, 2, 0)))
        mix_ref[...] = _conv_epilogue(u, prev1, prev2, p1, p3, cwb_ref).astype(BF16)
        tail_ref[...] = u.reshape(nb, t, tc)[:, t - V7X_SUBLANES:, :]

    @pl.when(j >= n_conv)
    def _attn_group():
        q_ref[...] = p0
        k_ref[...] = p1
        v_ref[...] = p2
        z_ref[...] = p3


def _inproj_sample(h, w_in, conv_wb, cache_conv, *, d_conv, d_attn, nb, t, tn, tc):
    m, d = h.shape
    assert m == nb * t and t % V7X_SUBLANES == 0 and t >= CONV_WIDTH - 1
    assert tn % tc == 0 and d_conv % tn == 0 and d_attn % tn == 0 and tc % V7X_LANES == 0
    n_conv, n_attn = d_conv // tc, d_attn // tc

    def wspec(p):
        src = lambda j: jnp.where(j < n_conv, p * n_conv + j, N_PARTS * n_conv + p * n_attn + j - n_conv)
        return pl.BlockSpec((d, tc), lambda j: (0, src(j)))

    cj = lambda j: jnp.minimum(j, n_conv - 1)
    aj = lambda j: jnp.maximum(j - n_conv, 0)
    attn_spec = pl.BlockSpec((m, tc), lambda j: (0, aj(j)))
    blocks = ([((d, tc), F32)] * N_PARTS + [((d, N_PARTS * tn), BF16)] + [((m, tc), F32)] * 6
              + [((d, N_PARTS * tc), BF16)] * 2)
    n_blocks = (d_conv + d_attn) // tn
    return pl.pallas_call(
        functools.partial(_inproj_sample_kernel, nb=nb, t=t, tn=tn, n_conv=n_conv),
        grid=(n_conv + n_attn,),
        in_specs=[pl.BlockSpec((m, d), lambda j: (0, 0), pipeline_mode=pl.Buffered(1)),
                  wspec(0), wspec(1), wspec(2), wspec(3),
                  pl.BlockSpec((CONV_WIDTH + 1, tc), lambda j: (0, cj(j))),
                  pl.BlockSpec((nb, CONV_WIDTH - 1, tc), lambda j: (0, 0, cj(j)))],
        out_specs=[pl.BlockSpec((m, tc), lambda j: (0, cj(j))),
                   pl.BlockSpec((nb, V7X_SUBLANES, tc), lambda j: (0, 0, cj(j))),
                   attn_spec, attn_spec, attn_spec, attn_spec,
                   pl.BlockSpec((None, d, N_PARTS * tn), lambda j: (j // (tn // tc), 0, 0))],
        out_shape=[jax.ShapeDtypeStruct((m, d_conv), BF16),
                   jax.ShapeDtypeStruct((nb, V7X_SUBLANES, d_conv), F32),
                   jax.ShapeDtypeStruct((m, d_attn), F32),
                   jax.ShapeDtypeStruct((m, d_attn), F32),
                   jax.ShapeDtypeStruct((m, d_attn), F32),
                   jax.ShapeDtypeStruct((m, d_attn), F32),
                   jax.ShapeDtypeStruct((n_blocks, d, N_PARTS * tn), BF16)],
        compiler_params=_params(("arbitrary",), blocks, extra_bytes=_nbytes((m, d), BF16)),
        name="inproj_sample",
    )(h, w_in, w_in, w_in, w_in, conv_wb, cache_conv)


def _bias_seq(rel_bias, offset, n):
    rev = rel_bias[:, ::-1]
    left = offset - MAX_REL
    assert left >= 0
    right = max(n - left - rev.shape[1], 0)
    return jnp.pad(rev, ((0, 0), (left, right)), mode="edge")[:, :n]


def _bias_table_kernel(seq_ref, o_ref, *, rows, cols, lane0, band_cols):
    width = seq_ref.shape[-1]
    seq = jnp.broadcast_to(seq_ref[0], (rows, width))
    table = pltpu.roll(seq, width - lane0, 1, stride=1, stride_axis=0)[:, :cols] * LOG2E
    if band_cols is not None:
        qc = lax.broadcasted_iota(jnp.int32, (rows, cols), 0) // CHUNK
        kc = lax.broadcasted_iota(jnp.int32, (rows, cols), 1) // CHUNK - band_cols // CHUNK
        table = jnp.where((kc <= qc) & (kc >= qc - N_PAST_CHUNKS), table, -jnp.inf)
    o_ref[0] = table


def _bias_table(rel_bias, *, rows, cols, rel0, band_cols=None):
    nh = rel_bias.shape[0]
    lane0 = V7X_LANES * (-(-rows // V7X_LANES))
    width = 1 << (lane0 + cols - 1).bit_length()
    seq = _bias_seq(rel_bias, rel0 + lane0, width)
    blocks = [((1, width), F32), ((rows, cols), F32), ((rows, width), F32), ((rows, width), F32)]
    return pl.pallas_call(
        functools.partial(_bias_table_kernel, rows=rows, cols=cols, lane0=lane0, band_cols=band_cols),
        grid=(nh,),
        in_specs=[pl.BlockSpec((1, 1, width), lambda h: (h, 0, 0))],
        out_specs=pl.BlockSpec((1, rows, cols), lambda h: (h, 0, 0)),
        out_shape=jax.ShapeDtypeStruct((nh, rows, cols), F32),
        compiler_params=_params(("arbitrary",), blocks),
        name="bias_table",
    )(seq.reshape(nh, 1, width))


def _attn_prompt_kernel(q_ref, k0_ref, k1_ref, k2_ref, k3_ref, v0_ref, v1_ref, v2_ref, v3_ref, z_ref, bias_ref,
                        o_ref, *, heads, tq):
    pair = pl.program_id(1)
    nk = 3 * tq
    tn = q_ref.shape[-1]
    dims = (((1,), (1,)), ((), ()))

    def run(mask_keys):
        if mask_keys:
            kc = lax.broadcasted_iota(jnp.int32, (1, nk), 1) // CHUNK
            valid = (kc >= 2 * (tq // CHUNK), kc >= tq // CHUNK)
        for hh in range(heads):
            blk, off = divmod(hh * HEAD_DIM, tn)
            head = (blk, slice(None), slice(off, off + HEAD_DIM))
            q = q_ref[head]
            s_a0 = lax.dot_general(q[:tq], k0_ref[head], dims, preferred_element_type=F32)
            s_1 = lax.dot_general(q, k1_ref[head], dims, preferred_element_type=F32)
            s_2 = lax.dot_general(q, k2_ref[head], dims, preferred_element_type=F32)
            s_b3 = lax.dot_general(q[tq:], k3_ref[head], dims, preferred_element_type=F32)
            probs, sums = [], []
            for which, pieces in enumerate(((s_a0, s_1[:tq], s_2[:tq]), (s_1[tq:], s_2[tq:], s_b3))):
                s = jnp.concatenate(pieces, axis=1) * (ATTN_SCALE * LOG2E) + bias_ref[hh]
                if mask_keys:
                    s = jnp.where(valid[which], s, -jnp.inf)
                p = jnp.exp2(s - jnp.max(s, axis=-1, keepdims=True))
                sums.append(jnp.sum(p, axis=-1, keepdims=True))
                probs.append(p.astype(BF16))
            pa, pb = probs
            t0 = _mm(pa[:, :tq], v0_ref[head])
            t1 = _mm(jnp.concatenate([pa[:, tq:2 * tq], pb[:, :tq]], axis=0), v1_ref[head])
            t2 = _mm(jnp.concatenate([pa[:, 2 * tq:], pb[:, tq:2 * tq]], axis=0), v2_ref[head])
            t3 = _mm(pb[:, 2 * tq:], v3_ref[head])
            o = jnp.concatenate([(t0 + t1[:tq] + t2[:tq]) / sums[0], (t1[tq:] + t2[tq:] + t3) / sums[1]], axis=0)
            o_ref[:, hh * HEAD_DIM:(hh + 1) * HEAD_DIM] = (o * _silu(z_ref[head])).astype(BF16)

    pl.when(pair == 0)(lambda: run(True))
    pl.when(pair > 0)(lambda: run(False))


def _attn_prompt(q, k, v, z, bias, *, tq, heads):
    nblk, s, tn = q.shape
    da = nblk * tn
    nh = da // HEAD_DIM
    wcols = heads * HEAD_DIM
    assert s % (2 * tq) == 0 and nh % heads == 0 and tq % CHUNK == 0 and 2 * tq >= BAND_PAST
    assert tn % HEAD_DIM == 0 and wcols % tn == 0
    gb = wcols // tn
    kspec = lambda back: pl.BlockSpec((gb, tq, tn), lambda g, p, back=back: (g, jnp.maximum(2 * p + 1 - back, 0), 0))
    both = pl.BlockSpec((gb, 2 * tq, tn), lambda g, p: (g, p, 0))
    blocks = ([((tq, wcols), BF16)] * 12 + [((2 * tq, wcols), F32)] + [((heads, tq, 3 * tq), F32)]
              + [((tq, 3 * tq), F32)] * 12)
    return pl.pallas_call(
        functools.partial(_attn_prompt_kernel, heads=heads, tq=tq),
        grid=(nh // heads, s // (2 * tq)),
        in_specs=[both, kspec(3), kspec(2), kspec(1), kspec(0), kspec(3), kspec(2), kspec(1), kspec(0), both,
                  pl.BlockSpec((heads, tq, 3 * tq), lambda g, p: (g, 0, 0))],
        out_specs=pl.BlockSpec((2 * tq, wcols), lambda g, p: (p, g)),
        out_shape=jax.ShapeDtypeStruct((s, da), BF16),
        compiler_params=_params(("arbitrary", "arbitrary"), blocks),
        name="attn_prompt",
    )(q, k, k, k, k, v, v, v, v, z, bias)


def _attn_sample_kernel(q_ref, kn_ref, vn_ref, ck_ref, cv_ref, z_ref, bias_ref, o_ref, *, nh, r, t):
    dims = (((1,), (1,)), ((), ()))
    for bb in range(q_ref.shape[0]):
        for hh in range(nh):
            sl = slice(hh * HEAD_DIM, (hh + 1) * HEAD_DIM)
            head_rows = pl.ds(hh, r, stride=nh)
            q = q_ref[bb, :, sl].astype(BF16)
            sc = lax.dot_general(q, ck_ref[bb, head_rows, :].astype(BF16), dims, preferred_element_type=F32)
            sn = lax.dot_general(q, kn_ref[bb, :, sl].astype(BF16), dims, preferred_element_type=F32)
            sc = sc * (ATTN_SCALE * LOG2E) + bias_ref[hh, :, 0:r]
            sn = sn * (ATTN_SCALE * LOG2E) + bias_ref[hh, :, r:r + t]
            m = jnp.maximum(jnp.max(sc, axis=-1, keepdims=True), jnp.max(sn, axis=-1, keepdims=True))
            pc = jnp.exp2(sc - m)
            pn = jnp.exp2(sn - m)
            l = jnp.sum(pc, axis=-1, keepdims=True) + jnp.sum(pn, axis=-1, keepdims=True)
            o = _mm(pc.astype(BF16), cv_ref[bb, head_rows, :].astype(BF16))
            o = o + _mm(pn.astype(BF16), vn_ref[bb, :, sl].astype(BF16))
            o_ref[bb, :, sl] = ((o / l) * _silu(z_ref[bb, :, sl])).astype(BF16)


def _attn_sample(q, kn, vn, cache_k, cache_v, z, bias, *, nbs):
    nb, t, da = q.shape
    nh = da // HEAD_DIM
    r = cache_k.shape[1] // nh
    assert r % V7X_LANES == 0 and bias.shape[2] >= r + t and nb % nbs == 0
    new = pl.BlockSpec((nbs, t, da), lambda b: (b, 0, 0))
    old = pl.BlockSpec((nbs, r * nh, HEAD_DIM), lambda b: (b, 0, 0))
    blocks = ([((nbs, t, da), F32)] * 5 + [((nbs, r, da), F32)] * 2 + [(bias.shape, F32)]
              + [((nbs, r, da), BF16)] * 2)
    return pl.pallas_call(
        functools.partial(_attn_sample_kernel, nh=nh, r=r, t=t),
        grid=(nb // nbs,),
        in_specs=[new, new, new, old, old, new, pl.BlockSpec(bias.shape, lambda b: (0, 0, 0))],
        out_specs=new,
        out_shape=jax.ShapeDtypeStruct((nb, t, da), BF16),
        compiler_params=_params(("arbitrary",), blocks),
        name="attn_sample",
    )(q, kn, vn, cache_k, cache_v, z, bias)


def _out_kernel(x_ref, ma_ref, mb_ref, wa_ref, wb_ref, gate_ref, gf_ref, y_ref, ss_ref, *, nj, tn):
    j = pl.program_id(1)
    acc = _mm(ma_ref[...], wa_ref[...]) + _mm(mb_ref[...], wb_ref[...])
    nb, t, _ = x_ref.shape
    res = x_ref[...] + gate_ref[...] * acc.reshape(nb, t, tn)
    y_ref[:, :, pl.ds(pl.multiple_of(j * tn, tn), tn)] = res
    part = jnp.sum(res * res, axis=-1, keepdims=True)

    @pl.when(j == 0)
    def _():
        ss_ref[...] = part

    @pl.when(j > 0)
    def _():
        ss_ref[...] += part

    @pl.when(j == nj - 1)
    def _():
        inv = lax.rsqrt(ss_ref[...] * (1.0 / (nj * tn)) + NORM_EPS)
        for n in range(nj):
            cols = slice(n * tn, (n + 1) * tn)
            y_ref[:, :, cols] = (y_ref[:, :, cols] * inv) * gf_ref[:, :, cols]


def _out(x, mix_a, mix_b, w_out, gate, g_final, *, nb, t, tn):
    b, s, d = x.shape
    half = mix_a.shape[1]
    tm = nb * t
    assert b % nb == 0 and s % t == 0 and d % tn == 0 and w_out.shape[0] == 2 * half
    nj = d // tn
    nt = s // t
    assert nb == 1 or nt == 1, "a row tile must be contiguous in the flattened (B*S) mixed rows"
    blocks = ([((nb, t, tn), F32)] + [((tm, half), BF16)] * 2 + [((half, tn), BF16)] * 2
              + [((nb, t, d), F32)] + [((tm, tn), F32)] * 4)
    return pl.pallas_call(
        functools.partial(_out_kernel, nj=nj, tn=tn),
        grid=((b // nb) * nt, nj),
        in_specs=[pl.BlockSpec((nb, t, tn), lambda i, j: (i // nt, i % nt, j)),
                  pl.BlockSpec((tm, half), lambda i, j: (i, 0)),
                  pl.BlockSpec((tm, half), lambda i, j: (i, 0)),
                  pl.BlockSpec((half, tn), lambda i, j: (0, j)),
                  pl.BlockSpec((half, tn), lambda i, j: (1, j)),
                  pl.BlockSpec((nb, 1, tn), lambda i, j: (i // nt, 0, j)),
                  pl.BlockSpec((1, 1, d), lambda i, j: (0, 0, 0))],
        out_specs=pl.BlockSpec((nb, t, d), lambda i, j: (i // nt, i % nt, 0)),
        out_shape=jax.ShapeDtypeStruct((b, s, d), F32),
        scratch_shapes=[pltpu.VMEM((nb, t, 1), F32)],
        compiler_params=_params(("arbitrary", "arbitrary"), blocks),
        name="out_proj",
    )(x, mix_a, mix_b, w_out, w_out, gate, g_final)


def kernel(x_prompt, x_sample, cache_k, cache_v, cache_conv, c_prompt, c_sample,
           g_norm, w_ada, b_ada, w_in, conv_w, conv_b, rel_bias, w_out, g_final):
    depth = g_norm.shape[0]
    assert depth == 1, "single-layer trunk"
    bp, sp, d = x_prompt.shape
    bs, ts, _ = x_sample.shape
    assert bp == 1
    d_conv = conv_w.shape[-1]
    d_attn = w_out.shape[1] - d_conv
    nh = d_attn // HEAD_DIM
    r = cache_k.shape[2]
    rows_kept = min(BAND_PAST, sp)

    n_c = bp + bs
    pad = (-n_c) % V7X_SUBLANES
    c_all = jnp.concatenate([c_prompt, c_sample, jnp.zeros((pad, d), F32)], axis=0)
    mod = _ada(c_all, w_ada[0], b_ada)
    shift, scale, gate = (mod[:n_c, i * d:(i + 1) * d].reshape(n_c, 1, d) for i in range(3))

    tn = 256
    g3 = g_norm.reshape(1, 1, d)
    gf3 = g_final.reshape(1, 1, d)
    conv_wb = jnp.concatenate([conv_w[0], conv_b], axis=0)

    hs = _prep(x_sample, g3, scale[bp:], shift[bp:], nb=8, t=ts).reshape(bs * ts, d)
    mix_conv_s, u_tail_s, qs, ks, vs, zs, w_blk = _inproj_sample(
        hs, w_in[0], conv_wb, cache_conv[0], d_conv=d_conv, d_attn=d_attn, nb=bs, t=ts, tn=tn, tc=V7X_LANES)

    tm, tq, halves = 1024, 256, 2
    h_first = _prep(x_prompt, g3, scale[:bp], shift[:bp], nb=1, t=512, s0=0, s_len=tm).reshape(tm, d)
    norm_p = jnp.concatenate([g_norm, scale[0], shift[0]], axis=0)
    mix_conv_p, u_tail_p, qp, kp, vp, zp, k_keep, v_keep, w_out_b = _inproj_prompt(
        x_prompt.reshape(sp, d), norm_p, h_first, w_blk, conv_wb, w_out[0],
        d_conv=d_conv, d_attn=d_attn, tm=tm, tn=tn, halves=halves, n_new=rows_kept)
    bias_p = _bias_table(rel_bias[0], rows=tq, cols=3 * tq, rel0=2 * tq, band_cols=2 * tq)
    mix_attn_p = _attn_prompt(qp, kp, vp, zp, bias_p, tq=tq, heads=8)
    y_prompt = _out(x_prompt, mix_conv_p, mix_attn_p, w_out_b, gate[:bp], gf3, nb=1, t=512, tn=1024)

    bias_s =_bias_table(rel_bias[0], rows=ts, cols=r + V7X_LANES * (-(-ts // V7X_LANES)), rel0=r)
    to3 = lambda a: a.reshape(bs, ts, d_attn)
    mix_attn_s = _attn_sample(to3(qs), to3(ks), to3(vs), cache_k[0].reshape(bs, r * nh, HEAD_DIM),
                              cache_v[0].reshape(bs, r * nh, HEAD_DIM), to3(zs), bias_s, nbs=1)
    y_sample = _out(x_sample, mix_conv_s, mix_attn_s.reshape(bs * ts, d_attn), w_out_b, gate[bp:], gf3,
                    nb=16, t=ts, tn=1024)

    keep = CONV_WIDTH - 1
    rows_major = lambda a: jnp.swapaxes(a[-1], 0, 1)
    new_k_prompt = rows_major(k_keep).reshape(1, bp, rows_kept, nh, HEAD_DIM)
    new_v_prompt = rows_major(v_keep).reshape(1, bp, rows_kept, nh, HEAD_DIM)
    new_conv_prompt = u_tail_p[-1, V7X_SUBLANES - keep:, :].reshape(1, bp, keep, d_conv)
    new_k_sample = ks.reshape(1, bs, ts, nh, HEAD_DIM)
    new_v_sample = vs.reshape(1, bs, ts, nh, HEAD_DIM)
    new_conv_sample = u_tail_s[:, V7X_SUBLANES - keep:, :].reshape(1, bs, keep, d_conv)
    return (y_prompt, y_sample, new_k_prompt, new_v_prompt, new_conv_prompt,
            new_k_sample, new_v_sample, new_conv_sample)
```

```python
import functools
import math

import jax
import jax.numpy as jnp
from jax import lax
from jax.experimental import pallas as pl
from jax.experimental.pallas import tpu as pltpu

CHUNK = 64
N_PAST_CHUNKS = 8
BAND_PAST = N_PAST_CHUNKS * CHUNK
HEAD_DIM = 128
CONV_WIDTH = 3
MAX_REL = 256
NORM_EPS = 1e-6
ATTN_SCALE = HEAD_DIM ** -0.5
LOG2E = math.log2(math.e)
N_PARTS = 4

V7X_LANES = 128
V7X_SUBLANES = 8
V7X_VMEM_BYTES = 64 * 1024 * 1024

F32 = jnp.float32
BF16 = jnp.bfloat16


def _nbytes(shape, dtype):
    n = 1
    for s in shape:
        n *= s
    return n * jnp.dtype(dtype).itemsize


def _params(semantics, blocks, extra_bytes=0):
    need = 2 * sum(_nbytes(s, d) for s, d in blocks) + extra_bytes + 16 * 1024 * 1024
    limit = int(min(need, V7X_VMEM_BYTES - 4 * 1024 * 1024))
    return pltpu.CompilerParams(dimension_semantics=semantics, vmem_limit_bytes=limit)


def _silu(z):
    return z * jax.nn.sigmoid(z)


def _mm(a, b):
    return jnp.dot(a, b, preferred_element_type=F32)


def _ada_kernel(c_ref, w_ref, b_ref, o_ref):
    o_ref[...] = _mm(c_ref[...].astype(BF16), w_ref[...].astype(BF16)) + b_ref[...]


def _ada(c, w_ada, b_ada, *, tn=512):
    r, d = c.shape
    n = w_ada.shape[1]
    assert n % tn == 0 and r % V7X_SUBLANES == 0
    blocks = [((r, d), F32), ((d, tn), F32), ((1, tn), F32), ((r, tn), F32), ((d, tn), BF16)]
    return pl.pallas_call(
        _ada_kernel,
        grid=(n // tn,),
        in_specs=[pl.BlockSpec((r, d), lambda j: (0, 0)),
                  pl.BlockSpec((d, tn), lambda j: (0, j)),
                  pl.BlockSpec((1, tn), lambda j: (0, j))],
        out_specs=pl.BlockSpec((r, tn), lambda j: (0, j)),
        out_shape=jax.ShapeDtypeStruct((r, n), F32),
        compiler_params=_params(("arbitrary",), blocks),
        name="ada",
    )(c, w_ada, b_ada)


def _modulated_norm(x, g, scale, shift):
    ms = jnp.mean(x * x, axis=-1, keepdims=True)
    xn = x * lax.rsqrt(ms + NORM_EPS)
    return (xn * g * (1.0 + scale) + shift).astype(BF16)


def _prep_kernel(x_ref, g_ref, scale_ref, shift_ref, h_ref):
    h_ref[...] = _modulated_norm(x_ref[...], g_ref[...], scale_ref[...], shift_ref[...])


def _prep(x, g, scale, shift, *, nb, t, s0=0, s_len=None):
    b, s, d = x.shape
    s_len = s if s_len is None else s_len
    assert b % nb == 0 and s_len % t == 0 and s0 % t == 0 and s0 + s_len <= s
    j0 = s0 // t
    blocks = [((nb, t, d), F32), ((nb, t, d), BF16), ((nb, t, d), F32)]
    return pl.pallas_call(
        _prep_kernel,
        grid=(b // nb, s_len // t),
        in_specs=[pl.BlockSpec((nb, t, d), lambda i, j: (i, j + j0, 0)),
                  pl.BlockSpec((1, 1, d), lambda i, j: (0, 0, 0)),
                  pl.BlockSpec((nb, 1, d), lambda i, j: (i, 0, 0)),
                  pl.BlockSpec((nb, 1, d), lambda i, j: (i, 0, 0))],
        out_specs=pl.BlockSpec((nb, t, d), lambda i, j: (i, j, 0)),
        out_shape=jax.ShapeDtypeStruct((b, s_len, d), BF16),
        compiler_params=_params(("arbitrary", "arbitrary"), blocks),
        name="prep",
    )(x, g, scale, shift)


def _weight_block_spec(d, tn, index_map):
    return pl.BlockSpec((None, d, N_PARTS * tn), lambda *ids: (index_map(*ids), 0, 0))


def _conv_epilogue(u, prev1, prev2, bg, z, cwb_ref):
    conv = cwb_ref[CONV_WIDTH:CONV_WIDTH + 1, :] + cwb_ref[0:1, :] * prev2
    conv = conv + cwb_ref[1:2, :] * prev1
    conv = conv + cwb_ref[2:3, :] * u
    return (bg * conv) * _silu(z)


def _inproj_prompt_kernel(x_ref, norm_ref, h0_ref, w_ref, cwb_ref, wo_ref,
                          mix_ref, tail_ref, q_ref, k_ref, v_ref, z_ref, k_new_ref, v_new_ref, wo_cast_ref,
                          h_buf, halo_ref, *, nbc, halves, tn):
    i = pl.program_id(0)
    j = pl.program_id(1)
    slot = lax.rem(i, 2)
    sm = x_ref.shape[0]
    hm = h_buf.shape[1] // halves

    @pl.when((i == 0) & (j == 0))
    def _():
        pltpu.sync_copy(h0_ref, h_buf.at[0])

    def prep_next_slab():
        h = _modulated_norm(x_ref[...], norm_ref[0:1, :], norm_ref[1:2, :], norm_ref[2:3, :])
        h_buf[1 - slot, pl.ds(pl.multiple_of(j * sm, sm), sm), :] = h

    def part(h, p):
        return _mm(h, w_ref[:, p * tn:(p + 1) * tn])

    def sub_tiles():
        wo_cast_ref[...] = wo_ref[...].astype(BF16)
        for s in range(halves):
            if s == halves - 1:
                prep_next_slab()
            yield slice(s * hm, (s + 1) * hm)

    @pl.when(j < nbc)
    def _conv_group():
        @pl.when(i == 0)
        def _():
            halo_ref[j] = jnp.zeros(halo_ref.shape[1:], F32)

        last8 = halo_ref[j]
        for rows in sub_tiles():
            h = h_buf[slot, rows, :]
            xin, bg, cg, z = part(h, 0), part(h, 1), part(h, 2), part(h, 3)
            u = cg * xin
            last1 = last8[V7X_SUBLANES - 1:V7X_SUBLANES, :]
            last2 = last8[V7X_SUBLANES - 2:V7X_SUBLANES - 1, :]
            row = lax.broadcasted_iota(jnp.int32, u.shape, 0)
            prev1 = jnp.where(row == 0, last1, pltpu.roll(u, 1, 0))
            prev2 = jnp.where(row == 0, last2, jnp.where(row == 1, last1, pltpu.roll(u, 2, 0)))
            mix_ref[rows, :] = _conv_epilogue(u, prev1, prev2, bg, z, cwb_ref).astype(BF16)
            last8 = u[hm - V7X_SUBLANES:, :]
        halo_ref[j] = last8
        tail_ref[0] = last8

    @pl.when(j >= nbc)
    def _attn_group():
        first_new = h_buf.shape[1] - k_new_ref.shape[0]
        for rows in sub_tiles():
            h = h_buf[slot, rows, :]
            k, v = part(h, 1), part(h, 2)
            q_ref[rows, :] = part(h, 0).astype(BF16)
            k_ref[rows, :] = k.astype(BF16)
            v_ref[rows, :] = v.astype(BF16)
            z_ref[rows, :] = part(h, 3)
            if rows.start >= first_new:
                new_rows = slice(rows.start - first_new, rows.stop - first_new)
                k_new_ref[new_rows, :] = k
                v_new_ref[new_rows, :] = v


def _inproj_prompt(x, norm, h0, w_blk, conv_wb, w_out, *, d_conv, d_attn, tm, tn, halves, n_new):
    s, d = x.shape
    assert s % tm == 0 and d_conv % tn == 0 and d_attn % tn == 0 and (tm // halves) % V7X_SUBLANES == 0
    nbc, nba = d_conv // tn, d_attn // tn
    nsteps = nbc + nba
    nrt = s // tm
    hm = tm // halves
    sm = tm // nsteps
    assert tm % nsteps == 0 and sm % (2 * V7X_SUBLANES) == 0 and h0.shape == (tm, d)
    assert n_new % hm == 0 and n_new <= tm, "the kept K/V rows are whole sub-tiles of the last row tile"
    cj = lambda j: jnp.minimum(j, nbc - 1)
    aj = lambda j: jnp.maximum(j - nbc, 0)
    attn_spec = pl.BlockSpec((None, tm, tn), lambda i, j: (aj(j), i, 0))
    newest_spec = pl.BlockSpec((None, None, n_new, tn), lambda i, j: (i, aj(j), 0, 0))
    wo_rows, wo_cols = w_out.shape
    ro = wo_rows // (nrt * nsteps)
    assert wo_rows % (nrt * nsteps) == 0 and ro % (2 * V7X_SUBLANES) == 0
    wo_spec = pl.BlockSpec((ro, wo_cols), lambda i, j: (i * nsteps + j, 0))
    blocks = ([((sm, d), F32), ((d, N_PARTS * tn), BF16)] + [((tm, tn), BF16)] * 4 + [((tm, tn), F32)]
              + [((n_new, tn), F32)] * 2 + [((hm, tn), F32)] * 8 + [((ro, wo_cols), F32)] * 2)
    return pl.pallas_call(
        functools.partial(_inproj_prompt_kernel, nbc=nbc, halves=halves, tn=tn),
        grid=(nrt, nsteps),
        in_specs=[pl.BlockSpec((sm, d), lambda i, j: (jnp.minimum(i + 1, nrt - 1) * nsteps + j, 0)),
                  pl.BlockSpec(norm.shape, lambda i, j: (0, 0)),
                  pl.BlockSpec(memory_space=pl.ANY),
                  _weight_block_spec(d, tn, lambda i, j: j),
                  pl.BlockSpec((CONV_WIDTH + 1, tn), lambda i, j: (0, cj(j))),
                  wo_spec],
        out_specs=[pl.BlockSpec((tm, tn), lambda i, j: (i, cj(j))),
                   pl.BlockSpec((1, V7X_SUBLANES, tn), lambda i, j: (i, 0, cj(j))),
                   attn_spec, attn_spec, attn_spec, attn_spec, newest_spec, newest_spec, wo_spec],
        out_shape=[jax.ShapeDtypeStruct((s, d_conv), BF16),
                   jax.ShapeDtypeStruct((nrt, V7X_SUBLANES, d_conv), F32),
                   jax.ShapeDtypeStruct((nba, s, tn), BF16),
                   jax.ShapeDtypeStruct((nba, s, tn), BF16),
                   jax.ShapeDtypeStruct((nba, s, tn), BF16),
                   jax.ShapeDtypeStruct((nba, s, tn), F32),
                   jax.ShapeDtypeStruct((nrt, nba, n_new, tn), F32),
                   jax.ShapeDtypeStruct((nrt, nba, n_new, tn), F32),
                   jax.ShapeDtypeStruct(w_out.shape, BF16)],
        scratch_shapes=[pltpu.VMEM((2, tm, d), BF16), pltpu.VMEM((nbc, V7X_SUBLANES, tn), F32)],
        compiler_params=_params(("arbitrary", "arbitrary"), blocks, extra_bytes=_nbytes((2, tm, d), BF16)),
        name="inproj_prompt",
    )(x, norm, h0, w_blk, conv_wb, w_out)


def _inproj_sample_kernel(h_ref, w0_ref, w1_ref, w2_ref, w3_ref, cwb_ref, cache_ref,
                          mix_ref, tail_ref, q_ref, k_ref, v_ref, z_ref, wblk_ref, *, nb, t, tn, n_conv):
    j = pl.program_id(0)
    tc = w0_ref.shape[1]
    piece = lax.rem(j, tn // tc)
    w = [w_ref[...].astype(BF16) for w_ref in (w0_ref, w1_ref, w2_ref, w3_ref)]
    for p in range(N_PARTS):
        wblk_ref[:, pl.ds(pl.multiple_of(p * tn + piece * tc, tc), tc)] = w[p]
    acc = _mm(h_ref[...], jnp.concatenate(w, axis=1))
    p0, p1, p2, p3 = (acc[:, p * tc:(p + 1) * tc] for p in range(N_PARTS))

    @pl.when(j < n_conv)
    def _conv_group():
        u = p2 * p0
        cache = cache_ref[...]
        c2 = jnp.broadcast_to(cache[:, 0:1, :], (nb, t, tc)).reshape(nb * t, tc)
        c1 = jnp.broadcast_to(cache[:, 1:2, :], (nb, t, tc)).reshape(nb * t, tc)
        pos = lax.broadcasted_iota(jnp.int32, (nb, t, tc), 1).reshape(nb * t, tc)
        prev1 = jnp.where(pos == 0, c1, pltpu.roll(u, 1, 0))
        prev2 = jnp.where(pos == 0, c2, jnp.where(pos == 1, c1, pltpu.roll(u, 2, 0)))
        mix_ref[...] = _conv_epilogue(u, prev1, prev2, p1, p3, cwb_ref).astype(BF16)
        tail_ref[...] = u.reshape(nb, t, tc)[:, t - V7X_SUBLANES:, :]

    @pl.when(j >= n_conv)
    def _attn_group():
        q_ref[...] = p0
        k_ref[...] = p1
        v_ref[...] = p2
        z_ref[...] = p3


def _inproj_sample(h, w_in, conv_wb, cache_conv, *, d_conv, d_attn, nb, t, tn, tc):
    m, d = h.shape
    assert m == nb * t and t % V7X_SUBLANES == 0 and t >= CONV_WIDTH - 1
    assert tn % tc == 0 and d_conv % tn == 0 and d_attn % tn == 0 and tc % V7X_LANES == 0
    n_conv, n_attn = d_conv // tc, d_attn // tc

    def wspec(p):
        src = lambda j: jnp.where(j < n_conv, p * n_conv + j, N_PARTS * n_conv + p * n_attn + j - n_conv)
        return pl.BlockSpec((d, tc), lambda j: (0, src(j)))

    cj = lambda j: jnp.minimum(j, n_conv - 1)
    aj = lambda j: jnp.maximum(j - n_conv, 0)
    attn_spec = pl.BlockSpec((m, tc), lambda j: (0, aj(j)))
    blocks = ([((d, tc), F32)] * N_PARTS + [((d, N_PARTS * tn), BF16)] + [((m, tc), F32)] * 6
              + [((d, N_PARTS * tc), BF16)] * 2)
    n_blocks = (d_conv + d_attn) // tn
    return pl.pallas_call(
        functools.partial(_inproj_sample_kernel, nb=nb, t=t, tn=tn, n_conv=n_conv),
        grid=(n_conv + n_attn,),
        in_specs=[pl.BlockSpec((m, d), lambda j: (0, 0), pipeline_mode=pl.Buffered(1)),
                  wspec(0), wspec(1), wspec(2), wspec(3),
                  pl.BlockSpec((CONV_WIDTH + 1, tc), lambda j: (0, cj(j))),
                  pl.BlockSpec((nb, CONV_WIDTH - 1, tc), lambda j: (0, 0, cj(j)))],
        out_specs=[pl.BlockSpec((m, tc), lambda j: (0, cj(j))),
                   pl.BlockSpec((nb, V7X_SUBLANES, tc), lambda j: (0, 0, cj(j))),
                   attn_spec, attn_spec, attn_spec, attn_spec,
                   pl.BlockSpec((None, d, N_PARTS * tn), lambda j: (j // (tn // tc), 0, 0))],
        out_shape=[jax.ShapeDtypeStruct((m, d_conv), BF16),
                   jax.ShapeDtypeStruct((nb, V7X_SUBLANES, d_conv), F32),
                   jax.ShapeDtypeStruct((m, d_attn), F32),
                   jax.ShapeDtypeStruct((m, d_attn), F32),
                   jax.ShapeDtypeStruct((m, d_attn), F32),
                   jax.ShapeDtypeStruct((m, d_attn), F32),
                   jax.ShapeDtypeStruct((n_blocks, d, N_PARTS * tn), BF16)],
        compiler_params=_params(("arbitrary",), blocks, extra_bytes=_nbytes((m, d), BF16)),
        name="inproj_sample",
    )(h, w_in, w_in, w_in, w_in, conv_wb, cache_conv)


def _bias_seq(rel_bias, offset, n):
    rev = rel_bias[:, ::-1]
    left = offset - MAX_REL
    assert left >= 0
    right = max(n - left - rev.shape[1], 0)
    return jnp.pad(rev, ((0, 0), (left, right)), mode="edge")[:, :n]


def _bias_table_kernel(seq_ref, o_ref, *, rows, cols, lane0, band_cols):
    width = seq_ref.shape[-1]
    seq = jnp.broadcast_to(seq_ref[0], (rows, width))
    table = pltpu.roll(seq, width - lane0, 1, stride=1, stride_axis=0)[:, :cols] * LOG2E
    if band_cols is not None:
        qc = lax.broadcasted_iota(jnp.int32, (rows, cols), 0) // CHUNK
        kc = lax.broadcasted_iota(jnp.int32, (rows, cols), 1) // CHUNK - band_cols // CHUNK
        table = jnp.where((kc <= qc) & (kc >= qc - N_PAST_CHUNKS), table, -jnp.inf)
    o_ref[0] = table


def _bias_table(rel_bias, *, rows, cols, rel0, band_cols=None):
    nh = rel_bias.shape[0]
    lane0 = V7X_LANES * (-(-rows // V7X_LANES))
    width = 1 << (lane0 + cols - 1).bit_length()
    seq = _bias_seq(rel_bias, rel0 + lane0, width)
    blocks = [((1, width), F32), ((rows, cols), F32), ((rows, width), F32), ((rows, width), F32)]
    return pl.pallas_call(
        functools.partial(_bias_table_kernel, rows=rows, cols=cols, lane0=lane0, band_cols=band_cols),
        grid=(nh,),
        in_specs=[pl.BlockSpec((1, 1, width), lambda h: (h, 0, 0))],
        out_specs=pl.BlockSpec((1, rows, cols), lambda h: (h, 0, 0)),
        out_shape=jax.ShapeDtypeStruct((nh, rows, cols), F32),
        compiler_params=_params(("arbitrary",), blocks),
        name="bias_table",
    )(seq.reshape(nh, 1, width))


KEY_BLOCKS = 3


def _attn_prompt_kernel(*refs, heads, tq, nq):
    nkb = nq + KEY_BLOCKS - 1
    q_ref, k_refs, v_refs = refs[0], refs[1:1 + nkb], refs[1 + nkb:1 + 2 * nkb]
    z_ref, bias_ref, o_ref = refs[1 + 2 * nkb:]
    step = pl.program_id(1)
    tn = q_ref.shape[-1]
    dims = (((1,), (1,)), ((), ()))
    seen_by = [(max(kb - KEY_BLOCKS + 1, 0), min(kb, nq - 1) + 1) for kb in range(nkb)]

    def run(mask_keys):
        if mask_keys:
            kc = lax.broadcasted_iota(jnp.int32, (1, KEY_BLOCKS * tq), 1) // CHUNK
        for hh in range(heads):
            blk, off = divmod(hh * HEAD_DIM, tn)
            head = (blk, slice(None), slice(off, off + HEAD_DIM))
            q = q_ref[head]
            scores = [lax.dot_general(q[lo * tq:hi * tq], k_refs[kb][head], dims, preferred_element_type=F32)
                      for kb, (lo, hi) in enumerate(seen_by)]
            probs, sums = [], []
            for qb in range(nq):
                pieces = [scores[kb][(qb - seen_by[kb][0]) * tq:(qb - seen_by[kb][0] + 1) * tq]
                          for kb in range(qb, qb + KEY_BLOCKS)]
                s = jnp.concatenate(pieces, axis=1) * (ATTN_SCALE * LOG2E) + bias_ref[hh]
                if mask_keys and qb < KEY_BLOCKS - 1:
                    s = jnp.where(kc >= (KEY_BLOCKS - 1 - qb) * (tq // CHUNK), s, -jnp.inf)
                p = jnp.exp2(s - jnp.max(s, axis=-1, keepdims=True))
                sums.append(jnp.sum(p, axis=-1, keepdims=True))
                probs.append(p.astype(BF16))
            outs = [None] * nq
            for kb, (lo, hi) in enumerate(seen_by):
                lhs = [probs[qb][:, (kb - qb) * tq:(kb - qb + 1) * tq] for qb in range(lo, hi)]
                t = _mm(lhs[0] if len(lhs) == 1 else jnp.concatenate(lhs, axis=0), v_refs[kb][head])
                for qb in range(lo, hi):
                    part = t[(qb - lo) * tq:(qb - lo + 1) * tq]
                    outs[qb] = part if outs[qb] is None else outs[qb] + part
            o = jnp.concatenate([outs[qb] / sums[qb] for qb in range(nq)], axis=0)
            o_ref[:, hh * HEAD_DIM:(hh + 1) * HEAD_DIM] = (o * _silu(z_ref[head])).astype(BF16)

    pl.when(step == 0)(lambda: run(True))
    pl.when(step > 0)(lambda: run(False))


def _attn_prompt(q, k, v, z, bias, *, tq, heads, nq):
    nblk, s, tn = q.shape
    da = nblk * tn
    nh = da // HEAD_DIM
    wcols = heads * HEAD_DIM
    nkb = nq + KEY_BLOCKS - 1
    assert s % (nq * tq) == 0 and nh % heads == 0 and tq % CHUNK == 0 and (KEY_BLOCKS - 1) * tq >= BAND_PAST
    assert tn % HEAD_DIM == 0 and wcols % tn == 0 and nq >= KEY_BLOCKS - 1
    gb = wcols // tn
    kspec = lambda kb: pl.BlockSpec((gb, tq, tn),
                                    lambda g, p, kb=kb: (g, jnp.maximum(p * nq + kb - (KEY_BLOCKS - 1), 0), 0))
    rows = pl.BlockSpec((gb, nq * tq, tn), lambda g, p: (g, p, 0))
    kv_specs = [kspec(kb) for kb in range(nkb)]
    blocks = ([((tq, wcols), BF16)] * (2 * nkb + 2 * nq) + [((nq * tq, wcols), F32)]
              + [((heads, tq, KEY_BLOCKS * tq), F32)] + [((tq, KEY_BLOCKS * tq), F32)] * (4 * nq + 4))
    return pl.pallas_call(
        functools.partial(_attn_prompt_kernel, heads=heads, tq=tq, nq=nq),
        grid=(nh // heads, s // (nq * tq)),
        in_specs=[rows] + kv_specs + kv_specs + [rows,
                  pl.BlockSpec((heads, tq, KEY_BLOCKS * tq), lambda g, p: (g, 0, 0))],
        out_specs=pl.BlockSpec((nq * tq, wcols), lambda g, p: (p, g)),
        out_shape=jax.ShapeDtypeStruct((s, da), BF16),
        compiler_params=_params(("arbitrary", "arbitrary"), blocks),
        name="attn_prompt",
    )(q, *([k] * nkb), *([v] * nkb), z, bias)


def _attn_sample_kernel(q_ref, kn_ref, vn_ref, ck_ref, cv_ref, z_ref, bias_ref, o_ref, *, nh, r, t):
    dims = (((1,), (1,)), ((), ()))
    for bb in range(q_ref.shape[0]):
        for hh in range(nh):
            sl = slice(hh * HEAD_DIM, (hh + 1) * HEAD_DIM)
            head_rows = pl.ds(hh, r, stride=nh)
            q = q_ref[bb, :, sl].astype(BF16)
            sc = lax.dot_general(q, ck_ref[bb, head_rows, :].astype(BF16), dims, preferred_element_type=F32)
            sn = lax.dot_general(q, kn_ref[bb, :, sl].astype(BF16), dims, preferred_element_type=F32)
            sc = sc * (ATTN_SCALE * LOG2E) + bias_ref[hh, :, 0:r]
            sn = sn * (ATTN_SCALE * LOG2E) + bias_ref[hh, :, r:r + t]
            m = jnp.maximum(jnp.max(sc, axis=-1, keepdims=True), jnp.max(sn, axis=-1, keepdims=True))
            pc = jnp.exp2(sc - m)
            pn = jnp.exp2(sn - m)
            l = jnp.sum(pc, axis=-1, keepdims=True) + jnp.sum(pn, axis=-1, keepdims=True)
            o = _mm(pc.astype(BF16), cv_ref[bb, head_rows, :].astype(BF16))
            o = o + _mm(pn.astype(BF16), vn_ref[bb, :, sl].astype(BF16))
            o_ref[bb, :, sl] = ((o / l) * _silu(z_ref[bb, :, sl])).astype(BF16)


def _attn_sample(q, kn, vn, cache_k, cache_v, z, bias, *, nbs):
    nb, t, da = q.shape
    nh = da // HEAD_DIM
    r = cache_k.shape[1] // nh
    assert r % V7X_LANES == 0 and bias.shape[2] >= r + t and nb % nbs == 0
    new = pl.BlockSpec((nbs, t, da), lambda b: (b, 0, 0))
    old = pl.BlockSpec((nbs, r * nh, HEAD_DIM), lambda b: (b, 0, 0))
    blocks = ([((nbs, t, da), F32)] * 5 + [((nbs, r, da), F32)] * 2 + [(bias.shape, F32)]
              + [((nbs, r, da), BF16)] * 2)
    return pl.pallas_call(
        functools.partial(_attn_sample_kernel, nh=nh, r=r, t=t),
        grid=(nb // nbs,),
        in_specs=[new, new, new, old, old, new, pl.BlockSpec(bias.shape, lambda b: (0, 0, 0))],
        out_specs=new,
        out_shape=jax.ShapeDtypeStruct((nb, t, da), BF16),
        compiler_params=_params(("arbitrary",), blocks),
        name="attn_sample",
    )(q, kn, vn, cache_k, cache_v, z, bias)


def _out_kernel(x_ref, ma_ref, mb_ref, wa_ref, wb_ref, gate_ref, gf_ref, y_ref, ss_ref, *, nj, tn):
    j = pl.program_id(1)
    acc = _mm(ma_ref[...], wa_ref[...]) + _mm(mb_ref[...], wb_ref[...])
    nb, t, _ = x_ref.shape
    res = x_ref[...] + gate_ref[...] * acc.reshape(nb, t, tn)
    y_ref[:, :, pl.ds(pl.multiple_of(j * tn, tn), tn)] = res
    part = jnp.sum(res * res, axis=-1, keepdims=True)

    @pl.when(j == 0)
    def _():
        ss_ref[...] = part

    @pl.when(j > 0)
    def _():
        ss_ref[...] += part

    @pl.when(j == nj - 1)
    def _():
        inv = lax.rsqrt(ss_ref[...] * (1.0 / (nj * tn)) + NORM_EPS)
        for n in range(nj):
            cols = slice(n * tn, (n + 1) * tn)
            y_ref[:, :, cols] = (y_ref[:, :, cols] * inv) * gf_ref[:, :, cols]


def _out(x, mix_a, mix_b, w_out, gate, g_final, *, nb, t, tn):
    b, s, d = x.shape
    half = mix_a.shape[1]
    tm = nb * t
    assert b % nb == 0 and s % t == 0 and d % tn == 0 and w_out.shape[0] == 2 * half
    nj = d // tn
    nt = s // t
    assert nb == 1 or nt == 1, "a row tile must be contiguous in the flattened (B*S) mixed rows"
    blocks = ([((nb, t, tn), F32)] + [((tm, half), BF16)] * 2 + [((half, tn), BF16)] * 2
              + [((nb, t, d), F32)] + [((tm, tn), F32)] * 4)
    return pl.pallas_call(
        functools.partial(_out_kernel, nj=nj, tn=tn),
        grid=((b // nb) * nt, nj),
        in_specs=[pl.BlockSpec((nb, t, tn), lambda i, j: (i // nt, i % nt, j)),
                  pl.BlockSpec((tm, half), lambda i, j: (i, 0)),
                  pl.BlockSpec((tm, half), lambda i, j: (i, 0)),
                  pl.BlockSpec((half, tn), lambda i, j: (0, j)),
                  pl.BlockSpec((half, tn), lambda i, j: (1, j)),
                  pl.BlockSpec((nb, 1, tn), lambda i, j: (i // nt, 0, j)),
                  pl.BlockSpec((1, 1, d), lambda i, j: (0, 0, 0))],
        out_specs=pl.BlockSpec((nb, t, d), lambda i, j: (i // nt, i % nt, 0)),
        out_shape=jax.ShapeDtypeStruct((b, s, d), F32),
        scratch_shapes=[pltpu.VMEM((nb, t, 1), F32)],
        compiler_params=_params(("arbitrary", "arbitrary"), blocks),
        name="out_proj",
    )(x, mix_a, mix_b, w_out, w_out, gate, g_final)


def kernel(x_prompt, x_sample, cache_k, cache_v, cache_conv, c_prompt, c_sample,
           g_norm, w_ada, b_ada, w_in, conv_w, conv_b, rel_bias, w_out, g_final):
    depth = g_norm.shape[0]
    assert depth == 1, "single-layer trunk"
    bp, sp, d = x_prompt.shape
    bs, ts, _ = x_sample.shape
    assert bp == 1
    d_conv = conv_w.shape[-1]
    d_attn = w_out.shape[1] - d_conv
    nh = d_attn // HEAD_DIM
    r = cache_k.shape[2]
    rows_kept = min(BAND_PAST, sp)

    n_c = bp + bs
    pad = (-n_c) % V7X_SUBLANES
    c_all = jnp.concatenate([c_prompt, c_sample, jnp.zeros((pad, d), F32)], axis=0)
    mod = _ada(c_all, w_ada[0], b_ada)
    shift, scale, gate = (mod[:n_c, i * d:(i + 1) * d].reshape(n_c, 1, d) for i in range(3))

    tn = 256
    g3 = g_norm.reshape(1, 1, d)
    gf3 = g_final.reshape(1, 1, d)
    conv_wb = jnp.concatenate([conv_w[0], conv_b], axis=0)

    hs = _prep(x_sample, g3, scale[bp:], shift[bp:], nb=8, t=ts).reshape(bs * ts, d)
    mix_conv_s, u_tail_s, qs, ks, vs, zs, w_blk = _inproj_sample(
        hs, w_in[0], conv_wb, cache_conv[0], d_conv=d_conv, d_attn=d_attn, nb=bs, t=ts, tn=tn, tc=V7X_LANES)

    tm, tq, halves = 1024, 256, 2
    h_first = _prep(x_prompt, g3, scale[:bp], shift[:bp], nb=1, t=512, s0=0, s_len=tm).reshape(tm, d)
    norm_p = jnp.concatenate([g_norm, scale[0], shift[0]], axis=0)
    mix_conv_p, u_tail_p, qp, kp, vp, zp, k_keep, v_keep, w_out_b = _inproj_prompt(
        x_prompt.reshape(sp, d), norm_p, h_first, w_blk, conv_wb, w_out[0],
        d_conv=d_conv, d_attn=d_attn, tm=tm, tn=tn, halves=halves, n_new=rows_kept)
    bias_p = _bias_table(rel_bias[0], rows=tq, cols=3 * tq, rel0=2 * tq, band_cols=2 * tq)
    mix_attn_p = _attn_prompt(qp, kp, vp, zp, bias_p, tq=tq, heads=8, nq=4)
    y_prompt = _out(x_prompt, mix_conv_p, mix_attn_p, w_out_b, gate[:bp], gf3, nb=1, t=512, tn=1024)

    bias_s =_bias_table(rel_bias[0], rows=ts, cols=r + V7X_LANES * (-(-ts // V7X_LANES)), rel0=r)
    to3 = lambda a: a.reshape(bs, ts, d_attn)
    mix_attn_s = _attn_sample(to3(qs), to3(ks), to3(vs), cache_k[0].reshape(bs, r * nh, HEAD_DIM),
                              cache_v[0].reshape(bs, r * nh, HEAD_DIM), to3(zs), bias_s, nbs=1)
    y_sample = _out(x_sample, mix_conv_s, mix_attn_s.reshape(bs * ts, d_attn), w_out_b, gate[bp:], gf3,
                    nb=16, t=ts, tn=1024)

    keep = CONV_WIDTH - 1
    rows_major = lambda a: jnp.swapaxes(a[-1], 0, 1)
    new_k_prompt = rows_major(k_keep).reshape(1, bp, rows_kept, nh, HEAD_DIM)
    new_v_prompt = rows_major(v_keep).reshape(1, bp, rows_kept, nh, HEAD_DIM)
    new_conv_prompt = u_tail_p[-1, V7X_SUBLANES - keep:, :].reshape(1, bp, keep, d_conv)
    new_k_sample = ks.reshape(1, bs, ts, nh, HEAD_DIM)
    new_v_sample = vs.reshape(1, bs, ts, nh, HEAD_DIM)
    new_conv_sample = u_tail_s[:, V7X_SUBLANES - keep:, :].reshape(1, bs, keep, d_conv)
    return (y_prompt, y_sample, new_k_prompt, new_v_prompt, new_conv_prompt,
            new_k_sample, new_v_sample, new_conv_sample)
```

```python
import functools
import math

import jax
import jax.numpy as jnp
from jax import lax
from jax.experimental import pallas as pl
from jax.experimental.pallas import tpu as pltpu

CHUNK = 64
N_PAST_CHUNKS = 8
BAND_PAST = N_PAST_CHUNKS * CHUNK
HEAD_DIM = 128
CONV_WIDTH = 3
MAX_REL = 256
NORM_EPS = 1e-6
ATTN_SCALE = HEAD_DIM ** -0.5
LOG2E = math.log2(math.e)
N_PARTS = 4

V7X_LANES = 128
V7X_SUBLANES = 8
V7X_VMEM_BYTES = 64 * 1024 * 1024

F32 = jnp.float32
BF16 = jnp.bfloat16


def _nbytes(shape, dtype):
    n = 1
    for s in shape:
        n *= s
    return n * jnp.dtype(dtype).itemsize


def _params(semantics, blocks, extra_bytes=0):
    need = 2 * sum(_nbytes(s, d) for s, d in blocks) + extra_bytes + 16 * 1024 * 1024
    limit = int(min(need, V7X_VMEM_BYTES - 4 * 1024 * 1024))
    return pltpu.CompilerParams(dimension_semantics=semantics, vmem_limit_bytes=limit)


def _silu(z):
    return z * jax.nn.sigmoid(z)


def _mm(a, b):
    return jnp.dot(a, b, preferred_element_type=F32)


def _ada_kernel(c_ref, w_ref, b_ref, o_ref):
    o_ref[...] = _mm(c_ref[...].astype(BF16), w_ref[...].astype(BF16)) + b_ref[...]


def _ada(c, w_ada, b_ada, *, tn=512):
    r, d = c.shape
    n = w_ada.shape[1]
    assert n % tn == 0 and r % V7X_SUBLANES == 0
    blocks = [((r, d), F32), ((d, tn), F32), ((1, tn), F32), ((r, tn), F32), ((d, tn), BF16)]
    return pl.pallas_call(
        _ada_kernel,
        grid=(n // tn,),
        in_specs=[pl.BlockSpec((r, d), lambda j: (0, 0)),
                  pl.BlockSpec((d, tn), lambda j: (0, j)),
                  pl.BlockSpec((1, tn), lambda j: (0, j))],
        out_specs=pl.BlockSpec((r, tn), lambda j: (0, j)),
        out_shape=jax.ShapeDtypeStruct((r, n), F32),
        compiler_params=_params(("arbitrary",), blocks),
        name="ada",
    )(c, w_ada, b_ada)


def _modulated_norm(x, g, scale, shift):
    ms = jnp.mean(x * x, axis=-1, keepdims=True)
    xn = x * lax.rsqrt(ms + NORM_EPS)
    return (xn * g * (1.0 + scale) + shift).astype(BF16)


def _prep_kernel(x_ref, g_ref, scale_ref, shift_ref, h_ref):
    h_ref[...] = _modulated_norm(x_ref[...], g_ref[...], scale_ref[...], shift_ref[...])


def _prep(x, g, scale, shift, *, nb, t, s0=0, s_len=None):
    b, s, d = x.shape
    s_len = s if s_len is None else s_len
    assert b % nb == 0 and s_len % t == 0 and s0 % t == 0 and s0 + s_len <= s
    j0 = s0 // t
    blocks = [((nb, t, d), F32), ((nb, t, d), BF16), ((nb, t, d), F32)]
    return pl.pallas_call(
        _prep_kernel,
        grid=(b // nb, s_len // t),
        in_specs=[pl.BlockSpec((nb, t, d), lambda i, j: (i, j + j0, 0)),
                  pl.BlockSpec((1, 1, d), lambda i, j: (0, 0, 0)),
                  pl.BlockSpec((nb, 1, d), lambda i, j: (i, 0, 0)),
                  pl.BlockSpec((nb, 1, d), lambda i, j: (i, 0, 0))],
        out_specs=pl.BlockSpec((nb, t, d), lambda i, j: (i, j, 0)),
        out_shape=jax.ShapeDtypeStruct((b, s_len, d), BF16),
        compiler_params=_params(("arbitrary", "arbitrary"), blocks),
        name="prep",
    )(x, g, scale, shift)


def _weight_block_spec(d, tn, index_map):
    return pl.BlockSpec((None, d, N_PARTS * tn), lambda *ids: (index_map(*ids), 0, 0))


def _conv_epilogue(u, prev1, prev2, bg, z, cwb_ref):
    conv = cwb_ref[CONV_WIDTH:CONV_WIDTH + 1, :] + cwb_ref[0:1, :] * prev2
    conv = conv + cwb_ref[1:2, :] * prev1
    conv = conv + cwb_ref[2:3, :] * u
    return (bg * conv) * _silu(z)


def _inproj_prompt_kernel(x_ref, norm_ref, h0_ref, w_ref, cwb_ref, wo_ref,
                          mix_ref, tail_ref, q_ref, k_ref, v_ref, z_ref, k_new_ref, v_new_ref, wo_cast_ref,
                          h_buf, halo_ref, *, nbc, halves, tn):
    i = pl.program_id(0)
    j = pl.program_id(1)
    slot = lax.rem(i, 2)
    sm = x_ref.shape[0]
    hm = h_buf.shape[1] // halves

    @pl.when((i == 0) & (j == 0))
    def _():
        pltpu.sync_copy(h0_ref, h_buf.at[0])

    def prep_next_slab():
        h = _modulated_norm(x_ref[...], norm_ref[0:1, :], norm_ref[1:2, :], norm_ref[2:3, :])
        h_buf[1 - slot, pl.ds(pl.multiple_of(j * sm, sm), sm), :] = h

    def part(h, p):
        return _mm(h, w_ref[:, p * tn:(p + 1) * tn])

    def sub_tiles():
        wo_cast_ref[...] = wo_ref[...].astype(BF16)
        for s in range(halves):
            if s == halves - 1:
                prep_next_slab()
            yield slice(s * hm, (s + 1) * hm)

    @pl.when(j < nbc)
    def _conv_group():
        @pl.when(i == 0)
        def _():
            halo_ref[j] = jnp.zeros(halo_ref.shape[1:], F32)

        last8 = halo_ref[j]
        for rows in sub_tiles():
            h = h_buf[slot, rows, :]
            xin, bg, cg, z = part(h, 0), part(h, 1), part(h, 2), part(h, 3)
            u = cg * xin
            last1 = last8[V7X_SUBLANES - 1:V7X_SUBLANES, :]
            last2 = last8[V7X_SUBLANES - 2:V7X_SUBLANES - 1, :]
            row = lax.broadcasted_iota(jnp.int32, u.shape, 0)
            prev1 = jnp.where(row == 0, last1, pltpu.roll(u, 1, 0))
            prev2 = jnp.where(row == 0, last2, jnp.where(row == 1, last1, pltpu.roll(u, 2, 0)))
            mix_ref[rows, :] = _conv_epilogue(u, prev1, prev2, bg, z, cwb_ref).astype(BF16)
            last8 = u[hm - V7X_SUBLANES:, :]
        halo_ref[j] = last8
        tail_ref[0] = last8

    @pl.when(j >= nbc)
    def _attn_group():
        first_new = h_buf.shape[1] - k_new_ref.shape[0]
        for rows in sub_tiles():
            h = h_buf[slot, rows, :]
            k, v = part(h, 1), part(h, 2)
            q_ref[rows, :] = part(h, 0).astype(BF16)
            k_ref[rows, :] = k.astype(BF16)
            v_ref[rows, :] = v.astype(BF16)
            z_ref[rows, :] = part(h, 3)
            if rows.start >= first_new:
                new_rows = slice(rows.start - first_new, rows.stop - first_new)
                k_new_ref[new_rows, :] = k
                v_new_ref[new_rows, :] = v


def _inproj_prompt(x, norm, h0, w_blk, conv_wb, w_out, *, d_conv, d_attn, tm, tn, halves, n_new):
    s, d = x.shape
    assert s % tm == 0 and d_conv % tn == 0 and d_attn % tn == 0 and (tm // halves) % V7X_SUBLANES == 0
    nbc, nba = d_conv // tn, d_attn // tn
    nsteps = nbc + nba
    nrt = s // tm
    hm = tm // halves
    sm = tm // nsteps
    assert tm % nsteps == 0 and sm % (2 * V7X_SUBLANES) == 0 and h0.shape == (tm, d)
    assert n_new % hm == 0 and n_new <= tm, "the kept K/V rows are whole sub-tiles of the last row tile"
    cj = lambda j: jnp.minimum(j, nbc - 1)
    aj = lambda j: jnp.maximum(j - nbc, 0)
    attn_spec = pl.BlockSpec((None, tm, tn), lambda i, j: (aj(j), i, 0))
    newest_spec = pl.BlockSpec((None, None, n_new, tn), lambda i, j: (i, aj(j), 0, 0))
    wo_rows, wo_cols = w_out.shape
    ro = wo_rows // (nrt * nsteps)
    assert wo_rows % (nrt * nsteps) == 0 and ro % (2 * V7X_SUBLANES) == 0
    wo_spec = pl.BlockSpec((ro, wo_cols), lambda i, j: (i * nsteps + j, 0))
    blocks = ([((sm, d), F32), ((d, N_PARTS * tn), BF16)] + [((tm, tn), BF16)] * 4 + [((tm, tn), F32)]
              + [((n_new, tn), F32)] * 2 + [((hm, tn), F32)] * 8 + [((ro, wo_cols), F32)] * 2)
    return pl.pallas_call(
        functools.partial(_inproj_prompt_kernel, nbc=nbc, halves=halves, tn=tn),
        grid=(nrt, nsteps),
        in_specs=[pl.BlockSpec((sm, d), lambda i, j: (jnp.minimum(i + 1, nrt - 1) * nsteps + j, 0)),
                  pl.BlockSpec(norm.shape, lambda i, j: (0, 0)),
                  pl.BlockSpec(memory_space=pl.ANY),
                  _weight_block_spec(d, tn, lambda i, j: j),
                  pl.BlockSpec((CONV_WIDTH + 1, tn), lambda i, j: (0, cj(j))),
                  wo_spec],
        out_specs=[pl.BlockSpec((tm, tn), lambda i, j: (i, cj(j))),
                   pl.BlockSpec((1, V7X_SUBLANES, tn), lambda i, j: (i, 0, cj(j))),
                   attn_spec, attn_spec, attn_spec, attn_spec, newest_spec, newest_spec, wo_spec],
        out_shape=[jax.ShapeDtypeStruct((s, d_conv), BF16),
                   jax.ShapeDtypeStruct((nrt, V7X_SUBLANES, d_conv), F32),
                   jax.ShapeDtypeStruct((nba, s, tn), BF16),
                   jax.ShapeDtypeStruct((nba, s, tn), BF16),
                   jax.ShapeDtypeStruct((nba, s, tn), BF16),
                   jax.ShapeDtypeStruct((nba, s, tn), F32),
                   jax.ShapeDtypeStruct((nrt, nba, n_new, tn), F32),
                   jax.ShapeDtypeStruct((nrt, nba, n_new, tn), F32),
                   jax.ShapeDtypeStruct(w_out.shape, BF16)],
        scratch_shapes=[pltpu.VMEM((2, tm, d), BF16), pltpu.VMEM((nbc, V7X_SUBLANES, tn), F32)],
        compiler_params=_params(("arbitrary", "arbitrary"), blocks, extra_bytes=_nbytes((2, tm, d), BF16)),
        name="inproj_prompt",
    )(x, norm, h0, w_blk, conv_wb, w_out)


def _inproj_sample_kernel(h_ref, w0_ref, w1_ref, w2_ref, w3_ref, cwb_ref, cache_ref,
                          mix_ref, tail_ref, q_ref, k_ref, v_ref, z_ref, wblk_ref, *, nb, t, tn, n_conv):
    j = pl.program_id(0)
    tc = w0_ref.shape[1]
    piece = lax.rem(j, tn // tc)
    w = [w_ref[...].astype(BF16) for w_ref in (w0_ref, w1_ref, w2_ref, w3_ref)]
    for p in range(N_PARTS):
        wblk_ref[:, pl.ds(pl.multiple_of(p * tn + piece * tc, tc), tc)] = w[p]
    acc = _mm(h_ref[...], jnp.concatenate(w, axis=1))
    p0, p1, p2, p3 = (acc[:, p * tc:(p + 1) * tc] for p in range(N_PARTS))

    @pl.when(j < n_conv)
    def _conv_group():
        u = p2 * p0
        cache = cache_ref[...]
        c2 = jnp.broadcast_to(cache[:, 0:1, :], (nb, t, tc)).reshape(nb * t, tc)
        c1 = jnp.broadcast_to(cache[:, 1:2, :], (nb, t, tc)).reshape(nb * t, tc)
        pos = lax.broadcasted_iota(jnp.int32, (nb, t, tc), 1).reshape(nb * t, tc)
        prev1 = jnp.where(pos == 0, c1, pltpu.roll(u, 1, 0))
        prev2 = jnp.where(pos == 0, c2, jnp.where(pos == 1, c1, pltpu.roll(u, 2, 0)))
        mix_ref[...] = _conv_epilogue(u, prev1, prev2, p1, p3, cwb_ref).astype(BF16)
        tail_ref[...] = u.reshape(nb, t, tc)[:, t - V7X_SUBLANES:, :]

    @pl.when(j >= n_conv)
    def _attn_group():
        q_ref[...] = p0
        k_ref[...] = p1
        v_ref[...] = p2
        z_ref[...] = p3


def _inproj_sample(h, w_in, conv_wb, cache_conv, *, d_conv, d_attn, nb, t, tn, tc):
    m, d = h.shape
    assert m == nb * t and t % V7X_SUBLANES == 0 and t >= CONV_WIDTH - 1
    assert tn % tc == 0 and d_conv % tn == 0 and d_attn % tn == 0 and tc % V7X_LANES == 0
    n_conv, n_attn = d_conv // tc, d_attn // tc

    def wspec(p):
        src = lambda j: jnp.where(j < n_conv, p * n_conv + j, N_PARTS * n_conv + p * n_attn + j - n_conv)
        return pl.BlockSpec((d, tc), lambda j: (0, src(j)))

    cj = lambda j: jnp.minimum(j, n_conv - 1)
    aj = lambda j: jnp.maximum(j - n_conv, 0)
    attn_spec = pl.BlockSpec((m, tc), lambda j: (0, aj(j)))
    blocks = ([((d, tc), F32)] * N_PARTS + [((d, N_PARTS * tn), BF16)] + [((m, tc), F32)] * 6
              + [((d, N_PARTS * tc), BF16)] * 2)
    n_blocks = (d_conv + d_attn) // tn
    return pl.pallas_call(
        functools.partial(_inproj_sample_kernel, nb=nb, t=t, tn=tn, n_conv=n_conv),
        grid=(n_conv + n_attn,),
        in_specs=[pl.BlockSpec((m, d), lambda j: (0, 0), pipeline_mode=pl.Buffered(1)),
                  wspec(0), wspec(1), wspec(2), wspec(3),
                  pl.BlockSpec((CONV_WIDTH + 1, tc), lambda j: (0, cj(j))),
                  pl.BlockSpec((nb, CONV_WIDTH - 1, tc), lambda j: (0, 0, cj(j)))],
        out_specs=[pl.BlockSpec((m, tc), lambda j: (0, cj(j))),
                   pl.BlockSpec((nb, V7X_SUBLANES, tc), lambda j: (0, 0, cj(j))),
                   attn_spec, attn_spec, attn_spec, attn_spec,
                   pl.BlockSpec((None, d, N_PARTS * tn), lambda j: (j // (tn // tc), 0, 0))],
        out_shape=[jax.ShapeDtypeStruct((m, d_conv), BF16),
                   jax.ShapeDtypeStruct((nb, V7X_SUBLANES, d_conv), F32),
                   jax.ShapeDtypeStruct((m, d_attn), F32),
                   jax.ShapeDtypeStruct((m, d_attn), F32),
                   jax.ShapeDtypeStruct((m, d_attn), F32),
                   jax.ShapeDtypeStruct((m, d_attn), F32),
                   jax.ShapeDtypeStruct((n_blocks, d, N_PARTS * tn), BF16)],
        compiler_params=_params(("arbitrary",), blocks, extra_bytes=_nbytes((m, d), BF16)),
        name="inproj_sample",
    )(h, w_in, w_in, w_in, w_in, conv_wb, cache_conv)


def _bias_seq(rel_bias, offset, n):
    rev = rel_bias[:, ::-1]
    left = offset - MAX_REL
    assert left >= 0
    right = max(n - left - rev.shape[1], 0)
    return jnp.pad(rev, ((0, 0), (left, right)), mode="edge")[:, :n]


def _bias_table_kernel(seq_ref, o_ref, *, rows, cols, lane0, band_cols):
    width = seq_ref.shape[-1]
    seq = jnp.broadcast_to(seq_ref[0], (rows, width))
    table = pltpu.roll(seq, width - lane0, 1, stride=1, stride_axis=0)[:, :cols] * LOG2E
    if band_cols is not None:
        qc = lax.broadcasted_iota(jnp.int32, (rows, cols), 0) // CHUNK
        kc = lax.broadcasted_iota(jnp.int32, (rows, cols), 1) // CHUNK - band_cols // CHUNK
        table = jnp.where((kc <= qc) & (kc >= qc - N_PAST_CHUNKS), table, -jnp.inf)
    o_ref[0] = table


def _bias_table(rel_bias, *, rows, cols, rel0, band_cols=None):
    nh = rel_bias.shape[0]
    lane0 = V7X_LANES * (-(-rows // V7X_LANES))
    width = 1 << (lane0 + cols - 1).bit_length()
    seq = _bias_seq(rel_bias, rel0 + lane0, width)
    blocks = [((1, width), F32), ((rows, cols), F32), ((rows, width), F32), ((rows, width), F32)]
    return pl.pallas_call(
        functools.partial(_bias_table_kernel, rows=rows, cols=cols, lane0=lane0, band_cols=band_cols),
        grid=(nh,),
        in_specs=[pl.BlockSpec((1, 1, width), lambda h: (h, 0, 0))],
        out_specs=pl.BlockSpec((1, rows, cols), lambda h: (h, 0, 0)),
        out_shape=jax.ShapeDtypeStruct((nh, rows, cols), F32),
        compiler_params=_params(("arbitrary",), blocks),
        name="bias_table",
    )(seq.reshape(nh, 1, width))


KEY_BLOCKS = 3


def _attn_prompt_kernel(*refs, heads, tq, nq):
    nkb = nq + KEY_BLOCKS - 1
    q_ref, k_refs, v_refs = refs[0], refs[1:1 + nkb], refs[1 + nkb:1 + 2 * nkb]
    z_ref, bias_ref, o_ref = refs[1 + 2 * nkb:]
    step = pl.program_id(1)
    tn = q_ref.shape[-1]
    dims = (((1,), (1,)), ((), ()))
    seen_by = [(max(kb - KEY_BLOCKS + 1, 0), min(kb, nq - 1) + 1) for kb in range(nkb)]

    def run(mask_keys):
        if mask_keys:
            kc = lax.broadcasted_iota(jnp.int32, (1, KEY_BLOCKS * tq), 1) // CHUNK
        for hh in range(heads):
            blk, off = divmod(hh * HEAD_DIM, tn)
            head = (blk, slice(None), slice(off, off + HEAD_DIM))
            q = q_ref[head]
            scores = [lax.dot_general(q[lo * tq:hi * tq], k_refs[kb][head], dims, preferred_element_type=F32)
                      for kb, (lo, hi) in enumerate(seen_by)]
            probs, sums = [], []
            for qb in range(nq):
                pieces = [scores[kb][(qb - seen_by[kb][0]) * tq:(qb - seen_by[kb][0] + 1) * tq]
                          for kb in range(qb, qb + KEY_BLOCKS)]
                s = jnp.concatenate(pieces, axis=1) * (ATTN_SCALE * LOG2E) + bias_ref[hh]
                if mask_keys and qb < KEY_BLOCKS - 1:
                    s = jnp.where(kc >= (KEY_BLOCKS - 1 - qb) * (tq // CHUNK), s, -jnp.inf)
                p = jnp.exp2(s - jnp.max(s, axis=-1, keepdims=True))
                sums.append(jnp.sum(p, axis=-1, keepdims=True))
                probs.append(p.astype(BF16))
            outs = [None] * nq
            for kb, (lo, hi) in enumerate(seen_by):
                lhs = [probs[qb][:, (kb - qb) * tq:(kb - qb + 1) * tq] for qb in range(lo, hi)]
                t = _mm(lhs[0] if len(lhs) == 1 else jnp.concatenate(lhs, axis=0), v_refs[kb][head])
                for qb in range(lo, hi):
                    part = t[(qb - lo) * tq:(qb - lo + 1) * tq]
                    outs[qb] = part if outs[qb] is None else outs[qb] + part
            o = jnp.concatenate([outs[qb] / sums[qb] for qb in range(nq)], axis=0)
            o_ref[:, hh * HEAD_DIM:(hh + 1) * HEAD_DIM] = (o * _silu(z_ref[head])).astype(BF16)

    pl.when(step == 0)(lambda: run(True))
    pl.when(step > 0)(lambda: run(False))


def _attn_prompt(q, k, v, z, bias, *, tq, heads, nq):
    nblk, s, tn = q.shape
    da = nblk * tn
    nh = da // HEAD_DIM
    wcols = heads * HEAD_DIM
    nkb = nq + KEY_BLOCKS - 1
    assert s % (nq * tq) == 0 and nh % heads == 0 and tq % CHUNK == 0 and (KEY_BLOCKS - 1) * tq >= BAND_PAST
    assert tn % HEAD_DIM == 0 and wcols % tn == 0 and nq >= KEY_BLOCKS - 1
    gb = wcols // tn
    kspec = lambda kb: pl.BlockSpec((gb, tq, tn),
                                    lambda g, p, kb=kb: (g, jnp.maximum(p * nq + kb - (KEY_BLOCKS - 1), 0), 0))
    rows = pl.BlockSpec((gb, nq * tq, tn), lambda g, p: (g, p, 0))
    kv_specs = [kspec(kb) for kb in range(nkb)]
    blocks = ([((tq, wcols), BF16)] * (2 * nkb + 2 * nq) + [((nq * tq, wcols), F32)]
              + [((heads, tq, KEY_BLOCKS * tq), F32)] + [((tq, KEY_BLOCKS * tq), F32)] * (4 * nq + 4))
    return pl.pallas_call(
        functools.partial(_attn_prompt_kernel, heads=heads, tq=tq, nq=nq),
        grid=(nh // heads, s // (nq * tq)),
        in_specs=[rows] + kv_specs + kv_specs + [rows,
                  pl.BlockSpec((heads, tq, KEY_BLOCKS * tq), lambda g, p: (g, 0, 0))],
        out_specs=pl.BlockSpec((nq * tq, wcols), lambda g, p: (p, g)),
        out_shape=jax.ShapeDtypeStruct((s, da), BF16),
        compiler_params=_params(("arbitrary", "arbitrary"), blocks),
        name="attn_prompt",
    )(q, *([k] * nkb), *([v] * nkb), z, bias)


def _attn_sample_kernel(q_ref, kn_ref, vn_ref, ck_ref, cv_ref, z_ref, bias_ref, o_ref, *, nh, r, t):
    dims = (((1,), (1,)), ((), ()))
    for bb in range(q_ref.shape[0]):
        for hh in range(nh):
            sl = slice(hh * HEAD_DIM, (hh + 1) * HEAD_DIM)
            head_rows = pl.ds(hh, r, stride=nh)
            q = q_ref[bb, :, sl].astype(BF16)
            sc = lax.dot_general(q, ck_ref[bb, head_rows, :].astype(BF16), dims, preferred_element_type=F32)
            sn = lax.dot_general(q, kn_ref[bb, :, sl].astype(BF16), dims, preferred_element_type=F32)
            sc = sc * (ATTN_SCALE * LOG2E) + bias_ref[hh, :, 0:r]
            sn = sn * (ATTN_SCALE * LOG2E) + bias_ref[hh, :, r:r + t]
            m = jnp.maximum(jnp.max(sc, axis=-1, keepdims=True), jnp.max(sn, axis=-1, keepdims=True))
            pc = jnp.exp2(sc - m)
            pn = jnp.exp2(sn - m)
            l = jnp.sum(pc, axis=-1, keepdims=True) + jnp.sum(pn, axis=-1, keepdims=True)
            o = _mm(pc.astype(BF16), cv_ref[bb, head_rows, :].astype(BF16))
            o = o + _mm(pn.astype(BF16), vn_ref[bb, :, sl].astype(BF16))
            o_ref[bb, :, sl] = ((o / l) * _silu(z_ref[bb, :, sl])).astype(BF16)


def _attn_sample(q, kn, vn, cache_k, cache_v, z, bias, *, nbs):
    nb, t, da = q.shape
    nh = da // HEAD_DIM
    r = cache_k.shape[1] // nh
    assert r % V7X_LANES == 0 and bias.shape[2] >= r + t and nb % nbs == 0
    new = pl.BlockSpec((nbs, t, da), lambda b: (b, 0, 0))
    old = pl.BlockSpec((nbs, r * nh, HEAD_DIM), lambda b: (b, 0, 0))
    blocks = ([((nbs, t, da), F32)] * 5 + [((nbs, r, da), F32)] * 2 + [(bias.shape, F32)]
              + [((nbs, r, da), BF16)] * 2)
    return pl.pallas_call(
        functools.partial(_attn_sample_kernel, nh=nh, r=r, t=t),
        grid=(nb // nbs,),
        in_specs=[new, new, new, old, old, new, pl.BlockSpec(bias.shape, lambda b: (0, 0, 0))],
        out_specs=new,
        out_shape=jax.ShapeDtypeStruct((nb, t, da), BF16),
        compiler_params=_params(("arbitrary",), blocks),
        name="attn_sample",
    )(q, kn, vn, cache_k, cache_v, z, bias)


def _out_kernel(x_ref, ma_ref, mb_ref, wa_ref, wb_ref, gate_ref, gf_ref, y_ref, ss_ref, *, nj, tn):
    j = pl.program_id(1)
    acc = _mm(ma_ref[...], wa_ref[...]) + _mm(mb_ref[...], wb_ref[...])
    nb, t, _ = x_ref.shape
    res = x_ref[...] + gate_ref[...] * acc.reshape(nb, t, tn)
    y_ref[:, :, pl.ds(pl.multiple_of(j * tn, tn), tn)] = res
    part = jnp.sum(res * res, axis=-1, keepdims=True)

    @pl.when(j == 0)
    def _():
        ss_ref[...] = part

    @pl.when(j > 0)
    def _():
        ss_ref[...] += part

    @pl.when(j == nj - 1)
    def _():
        inv = lax.rsqrt(ss_ref[...] * (1.0 / (nj * tn)) + NORM_EPS)
        for n in range(nj):
            cols = slice(n * tn, (n + 1) * tn)
            y_ref[:, :, cols] = (y_ref[:, :, cols] * inv) * gf_ref[:, :, cols]


def _out(x, mix_a, mix_b, w_out, gate, g_final, *, nb, t, tn):
    b, s, d = x.shape
    half = mix_a.shape[1]
    tm = nb * t
    assert b % nb == 0 and s % t == 0 and d % tn == 0 and w_out.shape[0] == 2 * half
    nj = d // tn
    nt = s // t
    assert nb == 1 or nt == 1, "a row tile must be contiguous in the flattened (B*S) mixed rows"
    blocks = ([((nb, t, tn), F32)] + [((tm, half), BF16)] * 2 + [((half, tn), BF16)] * 2
              + [((nb, t, d), F32)] + [((tm, tn), F32)] * 4)
    return pl.pallas_call(
        functools.partial(_out_kernel, nj=nj, tn=tn),
        grid=((b // nb) * nt, nj),
        in_specs=[pl.BlockSpec((nb, t, tn), lambda i, j: (i // nt, i % nt, j)),
                  pl.BlockSpec((tm, half), lambda i, j: (i, 0)),
                  pl.BlockSpec((tm, half), lambda i, j: (i, 0)),
                  pl.BlockSpec((half, tn), lambda i, j: (0, j)),
                  pl.BlockSpec((half, tn), lambda i, j: (1, j)),
                  pl.BlockSpec((nb, 1, tn), lambda i, j: (i // nt, 0, j)),
                  pl.BlockSpec((1, 1, d), lambda i, j: (0, 0, 0))],
        out_specs=pl.BlockSpec((nb, t, d), lambda i, j: (i // nt, i % nt, 0)),
        out_shape=jax.ShapeDtypeStruct((b, s, d), F32),
        scratch_shapes=[pltpu.VMEM((nb, t, 1), F32)],
        compiler_params=_params(("arbitrary", "arbitrary"), blocks),
        name="out_proj",
    )(x, mix_a, mix_b, w_out, w_out, gate, g_final)


def kernel(x_prompt, x_sample, cache_k, cache_v, cache_conv, c_prompt, c_sample,
           g_norm, w_ada, b_ada, w_in, conv_w, conv_b, rel_bias, w_out, g_final):
    depth = g_norm.shape[0]
    assert depth == 1, "single-layer trunk"
    bp, sp, d = x_prompt.shape
    bs, ts, _ = x_sample.shape
    assert bp == 1
    d_conv = conv_w.shape[-1]
    d_attn = w_out.shape[1] - d_conv
    nh = d_attn // HEAD_DIM
    r = cache_k.shape[2]
    rows_kept = min(BAND_PAST, sp)

    n_c = bp + bs
    pad = (-n_c) % V7X_SUBLANES
    c_all = jnp.concatenate([c_prompt, c_sample, jnp.zeros((pad, d), F32)], axis=0)
    mod = _ada(c_all, w_ada[0], b_ada)
    shift, scale, gate = (mod[:n_c, i * d:(i + 1) * d].reshape(n_c, 1, d) for i in range(3))

    tn = 256
    g3 = g_norm.reshape(1, 1, d)
    gf3 = g_final.reshape(1, 1, d)
    conv_wb = jnp.concatenate([conv_w[0], conv_b], axis=0)

    hs = _prep(x_sample, g3, scale[bp:], shift[bp:], nb=8, t=ts).reshape(bs * ts, d)
    mix_conv_s, u_tail_s, qs, ks, vs, zs, w_blk = _inproj_sample(
        hs, w_in[0], conv_wb, cache_conv[0], d_conv=d_conv, d_attn=d_attn, nb=bs, t=ts, tn=tn, tc=V7X_LANES)

    tm, tq, halves = 1024, 256, 4
    h_first = _prep(x_prompt, g3, scale[:bp], shift[:bp], nb=1, t=512, s0=0, s_len=tm).reshape(tm, d)
    norm_p = jnp.concatenate([g_norm, scale[0], shift[0]], axis=0)
    mix_conv_p, u_tail_p, qp, kp, vp, zp, k_keep, v_keep, w_out_b = _inproj_prompt(
        x_prompt.reshape(sp, d), norm_p, h_first, w_blk, conv_wb, w_out[0],
        d_conv=d_conv, d_attn=d_attn, tm=tm, tn=tn, halves=halves, n_new=rows_kept)
    bias_p = _bias_table(rel_bias[0], rows=tq, cols=3 * tq, rel0=2 * tq, band_cols=2 * tq)
    mix_attn_p = _attn_prompt(qp, kp, vp, zp, bias_p, tq=tq, heads=8, nq=4)
    y_prompt = _out(x_prompt, mix_conv_p, mix_attn_p, w_out_b, gate[:bp], gf3, nb=1, t=512, tn=1024)

    bias_s =_bias_table(rel_bias[0], rows=ts, cols=r + V7X_LANES * (-(-ts // V7X_LANES)), rel0=r)
    to3 = lambda a: a.reshape(bs, ts, d_attn)
    mix_attn_s = _attn_sample(to3(qs), to3(ks), to3(vs), cache_k[0].reshape(bs, r * nh, HEAD_DIM),
                              cache_v[0].reshape(bs, r * nh, HEAD_DIM), to3(zs), bias_s, nbs=1)
    y_sample = _out(x_sample, mix_conv_s, mix_attn_s.reshape(bs * ts, d_attn), w_out_b, gate[bp:], gf3,
                    nb=16, t=ts, tn=1024)

    keep = CONV_WIDTH - 1
    rows_major = lambda a: jnp.swapaxes(a[-1], 0, 1)
    new_k_prompt = rows_major(k_keep).reshape(1, bp, rows_kept, nh, HEAD_DIM)
    new_v_prompt = rows_major(v_keep).reshape(1, bp, rows_kept, nh, HEAD_DIM)
    new_conv_prompt = u_tail_p[-1, V7X_SUBLANES - keep:, :].reshape(1, bp, keep, d_conv)
    new_k_sample = ks.reshape(1, bs, ts, nh, HEAD_DIM)
    new_v_sample = vs.reshape(1, bs, ts, nh, HEAD_DIM)
    new_conv_sample = u_tail_s[:, V7X_SUBLANES - keep:, :].reshape(1, bs, keep, d_conv)
    return (y_prompt, y_sample, new_k_prompt, new_v_prompt, new_conv_prompt,
            new_k_sample, new_v_sample, new_conv_sample)
```

```python
import functools
import math

import jax
import jax.numpy as jnp
from jax import lax
from jax.experimental import pallas as pl
from jax.experimental.pallas import tpu as pltpu

CHUNK = 64
N_PAST_CHUNKS = 8
BAND_PAST = N_PAST_CHUNKS * CHUNK
HEAD_DIM = 128
CONV_WIDTH = 3
MAX_REL = 256
NORM_EPS = 1e-6
ATTN_SCALE = HEAD_DIM ** -0.5
LOG2E = math.log2(math.e)
N_PARTS = 4

V7X_LANES = 128
V7X_SUBLANES = 8
V7X_VMEM_BYTES = 64 * 1024 * 1024

F32 = jnp.float32
BF16 = jnp.bfloat16


def _nbytes(shape, dtype):
    n = 1
    for s in shape:
        n *= s
    return n * jnp.dtype(dtype).itemsize


def _params(semantics, blocks, extra_bytes=0):
    need = 2 * sum(_nbytes(s, d) for s, d in blocks) + extra_bytes + 16 * 1024 * 1024
    limit = int(min(need, V7X_VMEM_BYTES - 4 * 1024 * 1024))
    return pltpu.CompilerParams(dimension_semantics=semantics, vmem_limit_bytes=limit)


def _silu(z):
    return z * jax.nn.sigmoid(z)


def _mm(a, b):
    return jnp.dot(a, b, preferred_element_type=F32)


def _ada_kernel(c_ref, w_ref, b_ref, o_ref):
    o_ref[...] = _mm(c_ref[...].astype(BF16), w_ref[...].astype(BF16)) + b_ref[...]


def _ada(c, w_ada, b_ada, *, tn=512):
    r, d = c.shape
    n = w_ada.shape[1]
    assert n % tn == 0 and r % V7X_SUBLANES == 0
    blocks = [((r, d), F32), ((d, tn), F32), ((1, tn), F32), ((r, tn), F32), ((d, tn), BF16)]
    return pl.pallas_call(
        _ada_kernel,
        grid=(n // tn,),
        in_specs=[pl.BlockSpec((r, d), lambda j: (0, 0)),
                  pl.BlockSpec((d, tn), lambda j: (0, j)),
                  pl.BlockSpec((1, tn), lambda j: (0, j))],
        out_specs=pl.BlockSpec((r, tn), lambda j: (0, j)),
        out_shape=jax.ShapeDtypeStruct((r, n), F32),
        compiler_params=_params(("arbitrary",), blocks),
        name="ada",
    )(c, w_ada, b_ada)


def _modulated_norm(x, g, scale, shift):
    ms = jnp.mean(x * x, axis=-1, keepdims=True)
    xn = x * lax.rsqrt(ms + NORM_EPS)
    return (xn * g * (1.0 + scale) + shift).astype(BF16)


def _prep_kernel(x_ref, g_ref, scale_ref, shift_ref, h_ref):
    h_ref[...] = _modulated_norm(x_ref[...], g_ref[...], scale_ref[...], shift_ref[...])


def _prep(x, g, scale, shift, *, nb, t, s0=0, s_len=None):
    b, s, d = x.shape
    s_len = s if s_len is None else s_len
    assert b % nb == 0 and s_len % t == 0 and s0 % t == 0 and s0 + s_len <= s
    j0 = s0 // t
    blocks = [((nb, t, d), F32), ((nb, t, d), BF16), ((nb, t, d), F32)]
    return pl.pallas_call(
        _prep_kernel,
        grid=(b // nb, s_len // t),
        in_specs=[pl.BlockSpec((nb, t, d), lambda i, j: (i, j + j0, 0)),
                  pl.BlockSpec((1, 1, d), lambda i, j: (0, 0, 0)),
                  pl.BlockSpec((nb, 1, d), lambda i, j: (i, 0, 0)),
                  pl.BlockSpec((nb, 1, d), lambda i, j: (i, 0, 0))],
        out_specs=pl.BlockSpec((nb, t, d), lambda i, j: (i, j, 0)),
        out_shape=jax.ShapeDtypeStruct((b, s_len, d), BF16),
        compiler_params=_params(("arbitrary", "arbitrary"), blocks),
        name="prep",
    )(x, g, scale, shift)


def _weight_block_spec(d, tn, index_map):
    return pl.BlockSpec((None, d, N_PARTS * tn), lambda *ids: (index_map(*ids), 0, 0))


def _conv_epilogue(u, prev1, prev2, bg, z, cwb_ref):
    conv = cwb_ref[CONV_WIDTH:CONV_WIDTH + 1, :] + cwb_ref[0:1, :] * prev2
    conv = conv + cwb_ref[1:2, :] * prev1
    conv = conv + cwb_ref[2:3, :] * u
    return (bg * conv) * _silu(z)


def _inproj_prompt_kernel(x_ref, norm_ref, h0_ref, w_ref, cwb_ref, wo_ref,
                          mix_ref, tail_ref, q_ref, k_ref, v_ref, z_ref, k_new_ref, v_new_ref, wo_cast_ref,
                          h_buf, halo_ref, *, nbc, halves, tn):
    i = pl.program_id(0)
    j = pl.program_id(1)
    slot = lax.rem(i, 2)
    sm = x_ref.shape[0]
    hm = h_buf.shape[1] // halves

    @pl.when((i == 0) & (j == 0))
    def _():
        pltpu.sync_copy(h0_ref, h_buf.at[0])

    def prep_next_slab():
        h = _modulated_norm(x_ref[...], norm_ref[0:1, :], norm_ref[1:2, :], norm_ref[2:3, :])
        h_buf[1 - slot, pl.ds(pl.multiple_of(j * sm, sm), sm), :] = h

    def part(h, p):
        return _mm(h, w_ref[:, p * tn:(p + 1) * tn])

    def sub_tiles():
        wo_cast_ref[...] = wo_ref[...].astype(BF16)
        for s in range(halves):
            if s == halves - 1:
                prep_next_slab()
            yield slice(s * hm, (s + 1) * hm)

    @pl.when(j < nbc)
    def _conv_group():
        @pl.when(i == 0)
        def _():
            halo_ref[j] = jnp.zeros(halo_ref.shape[1:], F32)

        last8 = halo_ref[j]
        for rows in sub_tiles():
            h = h_buf[slot, rows, :]
            xin, bg, cg, z = part(h, 0), part(h, 1), part(h, 2), part(h, 3)
            u = cg * xin
            last1 = last8[V7X_SUBLANES - 1:V7X_SUBLANES, :]
            last2 = last8[V7X_SUBLANES - 2:V7X_SUBLANES - 1, :]
            row = lax.broadcasted_iota(jnp.int32, u.shape, 0)
            prev1 = jnp.where(row == 0, last1, pltpu.roll(u, 1, 0))
            prev2 = jnp.where(row == 0, last2, jnp.where(row == 1, last1, pltpu.roll(u, 2, 0)))
            mix_ref[rows, :] = _conv_epilogue(u, prev1, prev2, bg, z, cwb_ref).astype(BF16)
            last8 = u[hm - V7X_SUBLANES:, :]
        halo_ref[j] = last8
        tail_ref[0] = last8

    @pl.when(j >= nbc)
    def _attn_group():
        first_new = h_buf.shape[1] - k_new_ref.shape[0]
        for rows in sub_tiles():
            h = h_buf[slot, rows, :]
            k, v = part(h, 1), part(h, 2)
            q_ref[rows, :] = part(h, 0).astype(BF16)
            k_ref[rows, :] = k.astype(BF16)
            v_ref[rows, :] = v.astype(BF16)
            z_ref[rows, :] = part(h, 3)
            if rows.start >= first_new:
                new_rows = slice(rows.start - first_new, rows.stop - first_new)
                k_new_ref[new_rows, :] = k
                v_new_ref[new_rows, :] = v


def _inproj_prompt(x, norm, h0, w_blk, conv_wb, w_out, *, d_conv, d_attn, tm, tn, halves, n_new):
    s, d = x.shape
    assert s % tm == 0 and d_conv % tn == 0 and d_attn % tn == 0 and (tm // halves) % V7X_SUBLANES == 0
    nbc, nba = d_conv // tn, d_attn // tn
    nsteps = nbc + nba
    nrt = s // tm
    hm = tm // halves
    sm = tm // nsteps
    assert tm % nsteps == 0 and sm % (2 * V7X_SUBLANES) == 0 and h0.shape == (tm, d)
    assert n_new % hm == 0 and n_new <= tm, "the kept K/V rows are whole sub-tiles of the last row tile"
    cj = lambda j: jnp.minimum(j, nbc - 1)
    aj = lambda j: jnp.maximum(j - nbc, 0)
    attn_spec = pl.BlockSpec((None, tm, tn), lambda i, j: (aj(j), i, 0))
    newest_spec = pl.BlockSpec((None, None, n_new, tn), lambda i, j: (i, aj(j), 0, 0))
    wo_rows, wo_cols = w_out.shape
    ro = wo_rows // (nrt * nsteps)
    assert wo_rows % (nrt * nsteps) == 0 and ro % (2 * V7X_SUBLANES) == 0
    wo_spec = pl.BlockSpec((ro, wo_cols), lambda i, j: (i * nsteps + j, 0))
    blocks = ([((sm, d), F32), ((d, N_PARTS * tn), BF16)] + [((tm, tn), BF16)] * 4 + [((tm, tn), F32)]
              + [((n_new, tn), F32)] * 2 + [((hm, tn), F32)] * 8 + [((ro, wo_cols), F32)] * 2)
    return pl.pallas_call(
        functools.partial(_inproj_prompt_kernel, nbc=nbc, halves=halves, tn=tn),
        grid=(nrt, nsteps),
        in_specs=[pl.BlockSpec((sm, d), lambda i, j: (jnp.minimum(i + 1, nrt - 1) * nsteps + j, 0)),
                  pl.BlockSpec(norm.shape, lambda i, j: (0, 0)),
                  pl.BlockSpec(memory_space=pl.ANY),
                  _weight_block_spec(d, tn, lambda i, j: j),
                  pl.BlockSpec((CONV_WIDTH + 1, tn), lambda i, j: (0, cj(j))),
                  wo_spec],
        out_specs=[pl.BlockSpec((tm, tn), lambda i, j: (i, cj(j))),
                   pl.BlockSpec((1, V7X_SUBLANES, tn), lambda i, j: (i, 0, cj(j))),
                   attn_spec, attn_spec, attn_spec, attn_spec, newest_spec, newest_spec, wo_spec],
        out_shape=[jax.ShapeDtypeStruct((s, d_conv), BF16),
                   jax.ShapeDtypeStruct((nrt, V7X_SUBLANES, d_conv), F32),
                   jax.ShapeDtypeStruct((nba, s, tn), BF16),
                   jax.ShapeDtypeStruct((nba, s, tn), BF16),
                   jax.ShapeDtypeStruct((nba, s, tn), BF16),
                   jax.ShapeDtypeStruct((nba, s, tn), F32),
                   jax.ShapeDtypeStruct((nrt, nba, n_new, tn), F32),
                   jax.ShapeDtypeStruct((nrt, nba, n_new, tn), F32),
                   jax.ShapeDtypeStruct(w_out.shape, BF16)],
        scratch_shapes=[pltpu.VMEM((2, tm, d), BF16), pltpu.VMEM((nbc, V7X_SUBLANES, tn), F32)],
        compiler_params=_params(("arbitrary", "arbitrary"), blocks, extra_bytes=_nbytes((2, tm, d), BF16)),
        name="inproj_prompt",
    )(x, norm, h0, w_blk, conv_wb, w_out)


def _inproj_sample_kernel(h_ref, w0_ref, w1_ref, w2_ref, w3_ref, cwb_ref, cache_ref,
                          mix_ref, tail_ref, q_ref, k_ref, v_ref, z_ref, wblk_ref, *, nb, t, tn, n_conv):
    j = pl.program_id(0)
    tc = w0_ref.shape[1]
    piece = lax.rem(j, tn // tc)
    w = [w_ref[...].astype(BF16) for w_ref in (w0_ref, w1_ref, w2_ref, w3_ref)]
    for p in range(N_PARTS):
        wblk_ref[:, pl.ds(pl.multiple_of(p * tn + piece * tc, tc), tc)] = w[p]
    acc = _mm(h_ref[...], jnp.concatenate(w, axis=1))
    p0, p1, p2, p3 = (acc[:, p * tc:(p + 1) * tc] for p in range(N_PARTS))

    @pl.when(j < n_conv)
    def _conv_group():
        u = p2 * p0
        cache = cache_ref[...]
        c2 = jnp.broadcast_to(cache[:, 0:1, :], (nb, t, tc)).reshape(nb * t, tc)
        c1 = jnp.broadcast_to(cache[:, 1:2, :], (nb, t, tc)).reshape(nb * t, tc)
        pos = lax.broadcasted_iota(jnp.int32, (nb, t, tc), 1).reshape(nb * t, tc)
        prev1 = jnp.where(pos == 0, c1, pltpu.roll(u, 1, 0))
        prev2 = jnp.where(pos == 0, c2, jnp.where(pos == 1, c1, pltpu.roll(u, 2, 0)))
        mix_ref[...] = _conv_epilogue(u, prev1, prev2, p1, p3, cwb_ref).astype(BF16)
        tail_ref[...] = u.reshape(nb, t, tc)[:, t - V7X_SUBLANES:, :]

    @pl.when(j >= n_conv)
    def _attn_group():
        q_ref[...] = p0
        k_ref[...] = p1
        v_ref[...] = p2
        z_ref[...] = p3


def _inproj_sample(h, w_in, conv_wb, cache_conv, *, d_conv, d_attn, nb, t, tn, tc):
    m, d = h.shape
    assert m == nb * t and t % V7X_SUBLANES == 0 and t >= CONV_WIDTH - 1
    assert tn % tc == 0 and d_conv % tn == 0 and d_attn % tn == 0 and tc % V7X_LANES == 0
    n_conv, n_attn = d_conv // tc, d_attn // tc

    def wspec(p):
        src = lambda j: jnp.where(j < n_conv, p * n_conv + j, N_PARTS * n_conv + p * n_attn + j - n_conv)
        return pl.BlockSpec((d, tc), lambda j: (0, src(j)))

    cj = lambda j: jnp.minimum(j, n_conv - 1)
    aj = lambda j: jnp.maximum(j - n_conv, 0)
    attn_spec = pl.BlockSpec((m, tc), lambda j: (0, aj(j)))
    blocks = ([((d, tc), F32)] * N_PARTS + [((d, N_PARTS * tn), BF16)] + [((m, tc), F32)] * 6
              + [((d, N_PARTS * tc), BF16)] * 2)
    n_blocks = (d_conv + d_attn) // tn
    return pl.pallas_call(
        functools.partial(_inproj_sample_kernel, nb=nb, t=t, tn=tn, n_conv=n_conv),
        grid=(n_conv + n_attn,),
        in_specs=[pl.BlockSpec((m, d), lambda j: (0, 0), pipeline_mode=pl.Buffered(1)),
                  wspec(0), wspec(1), wspec(2), wspec(3),
                  pl.BlockSpec((CONV_WIDTH + 1, tc), lambda j: (0, cj(j))),
                  pl.BlockSpec((nb, CONV_WIDTH - 1, tc), lambda j: (0, 0, cj(j)))],
        out_specs=[pl.BlockSpec((m, tc), lambda j: (0, cj(j))),
                   pl.BlockSpec((nb, V7X_SUBLANES, tc), lambda j: (0, 0, cj(j))),
                   attn_spec, attn_spec, attn_spec, attn_spec,
                   pl.BlockSpec((None, d, N_PARTS * tn), lambda j: (j // (tn // tc), 0, 0))],
        out_shape=[jax.ShapeDtypeStruct((m, d_conv), BF16),
                   jax.ShapeDtypeStruct((nb, V7X_SUBLANES, d_conv), F32),
                   jax.ShapeDtypeStruct((m, d_attn), F32),
                   jax.ShapeDtypeStruct((m, d_attn), F32),
                   jax.ShapeDtypeStruct((m, d_attn), F32),
                   jax.ShapeDtypeStruct((m, d_attn), F32),
                   jax.ShapeDtypeStruct((n_blocks, d, N_PARTS * tn), BF16)],
        compiler_params=_params(("arbitrary",), blocks, extra_bytes=_nbytes((m, d), BF16)),
        name="inproj_sample",
    )(h, w_in, w_in, w_in, w_in, conv_wb, cache_conv)


def _bias_seq(rel_bias, offset, n):
    rev = rel_bias[:, ::-1]
    left = offset - MAX_REL
    assert left >= 0
    right = max(n - left - rev.shape[1], 0)
    return jnp.pad(rev, ((0, 0), (left, right)), mode="edge")[:, :n]


def _bias_table_kernel(seq_ref, o_ref, *, rows, cols, lane0, band_cols):
    width = seq_ref.shape[-1]
    seq = jnp.broadcast_to(seq_ref[0], (rows, width))
    table = pltpu.roll(seq, width - lane0, 1, stride=1, stride_axis=0)[:, :cols] * LOG2E
    if band_cols is not None:
        qc = lax.broadcasted_iota(jnp.int32, (rows, cols), 0) // CHUNK
        kc = lax.broadcasted_iota(jnp.int32, (rows, cols), 1) // CHUNK - band_cols // CHUNK
        table = jnp.where((kc <= qc) & (kc >= qc - N_PAST_CHUNKS), table, -jnp.inf)
    o_ref[0] = table


def _bias_table(rel_bias, *, rows, cols, rel0, band_cols=None):
    nh = rel_bias.shape[0]
    lane0 = V7X_LANES * (-(-rows // V7X_LANES))
    width = 1 << (lane0 + cols - 1).bit_length()
    seq = _bias_seq(rel_bias, rel0 + lane0, width)
    blocks = [((1, width), F32), ((rows, cols), F32), ((rows, width), F32), ((rows, width), F32)]
    return pl.pallas_call(
        functools.partial(_bias_table_kernel, rows=rows, cols=cols, lane0=lane0, band_cols=band_cols),
        grid=(nh,),
        in_specs=[pl.BlockSpec((1, 1, width), lambda h: (h, 0, 0))],
        out_specs=pl.BlockSpec((1, rows, cols), lambda h: (h, 0, 0)),
        out_shape=jax.ShapeDtypeStruct((nh, rows, cols), F32),
        compiler_params=_params(("arbitrary",), blocks),
        name="bias_table",
    )(seq.reshape(nh, 1, width))


KEY_BLOCKS = 3


def _attn_prompt_kernel(*refs, heads, tq, nq):
    nkb = nq + KEY_BLOCKS - 1
    q_ref, k_refs, v_refs = refs[0], refs[1:1 + nkb], refs[1 + nkb:1 + 2 * nkb]
    z_ref, bias_ref, o_ref = refs[1 + 2 * nkb:]
    step = pl.program_id(1)
    tn = q_ref.shape[-1]
    dims = (((1,), (1,)), ((), ()))
    seen_by = [(max(kb - KEY_BLOCKS + 1, 0), min(kb, nq - 1) + 1) for kb in range(nkb)]

    def run(mask_keys):
        if mask_keys:
            kc = lax.broadcasted_iota(jnp.int32, (1, KEY_BLOCKS * tq), 1) // CHUNK
        for hh in range(heads):
            blk, off = divmod(hh * HEAD_DIM, tn)
            head = (blk, slice(None), slice(off, off + HEAD_DIM))
            q = q_ref[head]
            scores = [lax.dot_general(q[lo * tq:hi * tq], k_refs[kb][head], dims, preferred_element_type=F32)
                      for kb, (lo, hi) in enumerate(seen_by)]
            probs, sums = [], []
            for qb in range(nq):
                pieces = [scores[kb][(qb - seen_by[kb][0]) * tq:(qb - seen_by[kb][0] + 1) * tq]
                          for kb in range(qb, qb + KEY_BLOCKS)]
                s = jnp.concatenate(pieces, axis=1) * (ATTN_SCALE * LOG2E) + bias_ref[hh]
                if mask_keys and qb < KEY_BLOCKS - 1:
                    s = jnp.where(kc >= (KEY_BLOCKS - 1 - qb) * (tq // CHUNK), s, -jnp.inf)
                p = jnp.exp2(s - jnp.max(s, axis=-1, keepdims=True))
                sums.append(jnp.sum(p, axis=-1, keepdims=True))
                probs.append(p.astype(BF16))
            outs = [None] * nq
            for kb, (lo, hi) in enumerate(seen_by):
                lhs = [probs[qb][:, (kb - qb) * tq:(kb - qb + 1) * tq] for qb in range(lo, hi)]
                t = _mm(lhs[0] if len(lhs) == 1 else jnp.concatenate(lhs, axis=0), v_refs[kb][head])
                for qb in range(lo, hi):
                    part = t[(qb - lo) * tq:(qb - lo + 1) * tq]
                    outs[qb] = part if outs[qb] is None else outs[qb] + part
            o = jnp.concatenate([outs[qb] / sums[qb] for qb in range(nq)], axis=0)
            o_ref[:, hh * HEAD_DIM:(hh + 1) * HEAD_DIM] = (o * _silu(z_ref[head])).astype(BF16)

    pl.when(step == 0)(lambda: run(True))
    pl.when(step > 0)(lambda: run(False))


def _attn_prompt(q, k, v, z, bias, *, tq, heads, nq):
    nblk, s, tn = q.shape
    da = nblk * tn
    nh = da // HEAD_DIM
    wcols = heads * HEAD_DIM
    nkb = nq + KEY_BLOCKS - 1
    assert s % (nq * tq) == 0 and nh % heads == 0 and tq % CHUNK == 0 and (KEY_BLOCKS - 1) * tq >= BAND_PAST
    assert tn % HEAD_DIM == 0 and wcols % tn == 0 and nq >= KEY_BLOCKS - 1
    gb = wcols // tn
    kspec = lambda kb: pl.BlockSpec((gb, tq, tn),
                                    lambda g, p, kb=kb: (g, jnp.maximum(p * nq + kb - (KEY_BLOCKS - 1), 0), 0))
    rows = pl.BlockSpec((gb, nq * tq, tn), lambda g, p: (g, p, 0))
    kv_specs = [kspec(kb) for kb in range(nkb)]
    blocks = ([((tq, wcols), BF16)] * (2 * nkb + 2 * nq) + [((nq * tq, wcols), F32)]
              + [((heads, tq, KEY_BLOCKS * tq), F32)] + [((tq, KEY_BLOCKS * tq), F32)] * (4 * nq + 4))
    return pl.pallas_call(
        functools.partial(_attn_prompt_kernel, heads=heads, tq=tq, nq=nq),
        grid=(nh // heads, s // (nq * tq)),
        in_specs=[rows] + kv_specs + kv_specs + [rows,
                  pl.BlockSpec((heads, tq, KEY_BLOCKS * tq), lambda g, p: (g, 0, 0),
                               pipeline_mode=pl.Buffered(1))],
        out_specs=pl.BlockSpec((nq * tq, wcols), lambda g, p: (p, g)),
        out_shape=jax.ShapeDtypeStruct((s, da), BF16),
        compiler_params=_params(("arbitrary", "arbitrary"), blocks),
        name="attn_prompt",
    )(q, *([k] * nkb), *([v] * nkb), z, bias)


def _attn_sample_kernel(q_ref, kn_ref, vn_ref, ck_ref, cv_ref, z_ref, bias_ref, o_ref, *, nh, r, t):
    dims = (((1,), (1,)), ((), ()))
    for bb in range(q_ref.shape[0]):
        for hh in range(nh):
            sl = slice(hh * HEAD_DIM, (hh + 1) * HEAD_DIM)
            head_rows = pl.ds(hh, r, stride=nh)
            q = q_ref[bb, :, sl].astype(BF16)
            sc = lax.dot_general(q, ck_ref[bb, head_rows, :].astype(BF16), dims, preferred_element_type=F32)
            sn = lax.dot_general(q, kn_ref[bb, :, sl].astype(BF16), dims, preferred_element_type=F32)
            sc = sc * (ATTN_SCALE * LOG2E) + bias_ref[hh, :, 0:r]
            sn = sn * (ATTN_SCALE * LOG2E) + bias_ref[hh, :, r:r + t]
            m = jnp.maximum(jnp.max(sc, axis=-1, keepdims=True), jnp.max(sn, axis=-1, keepdims=True))
            pc = jnp.exp2(sc - m)
            pn = jnp.exp2(sn - m)
            l = jnp.sum(pc, axis=-1, keepdims=True) + jnp.sum(pn, axis=-1, keepdims=True)
            o = _mm(pc.astype(BF16), cv_ref[bb, head_rows, :].astype(BF16))
            o = o + _mm(pn.astype(BF16), vn_ref[bb, :, sl].astype(BF16))
            o_ref[bb, :, sl] = ((o / l) * _silu(z_ref[bb, :, sl])).astype(BF16)


def _attn_sample(q, kn, vn, cache_k, cache_v, z, bias, *, nbs):
    nb, t, da = q.shape
    nh = da // HEAD_DIM
    r = cache_k.shape[1] // nh
    assert r % V7X_LANES == 0 and bias.shape[2] >= r + t and nb % nbs == 0
    new = pl.BlockSpec((nbs, t, da), lambda b: (b, 0, 0))
    old = pl.BlockSpec((nbs, r * nh, HEAD_DIM), lambda b: (b, 0, 0))
    blocks = ([((nbs, t, da), F32)] * 5 + [((nbs, r, da), F32)] * 2 + [(bias.shape, F32)]
              + [((nbs, r, da), BF16)] * 2)
    return pl.pallas_call(
        functools.partial(_attn_sample_kernel, nh=nh, r=r, t=t),
        grid=(nb // nbs,),
        in_specs=[new, new, new, old, old, new, pl.BlockSpec(bias.shape, lambda b: (0, 0, 0))],
        out_specs=new,
        out_shape=jax.ShapeDtypeStruct((nb, t, da), BF16),
        compiler_params=_params(("arbitrary",), blocks),
        name="attn_sample",
    )(q, kn, vn, cache_k, cache_v, z, bias)


def _out_kernel(x_ref, ma_ref, mb_ref, wa_ref, wb_ref, gate_ref, gf_ref, y_ref, ss_ref, *, nj, tn):
    j = pl.program_id(1)
    acc = _mm(ma_ref[...], wa_ref[...]) + _mm(mb_ref[...], wb_ref[...])
    nb, t, _ = x_ref.shape
    res = x_ref[...] + gate_ref[...] * acc.reshape(nb, t, tn)
    y_ref[:, :, pl.ds(pl.multiple_of(j * tn, tn), tn)] = res
    part = jnp.sum(res * res, axis=-1, keepdims=True)

    @pl.when(j == 0)
    def _():
        ss_ref[...] = part

    @pl.when(j > 0)
    def _():
        ss_ref[...] += part

    @pl.when(j == nj - 1)
    def _():
        inv = lax.rsqrt(ss_ref[...] * (1.0 / (nj * tn)) + NORM_EPS)
        for n in range(nj):
            cols = slice(n * tn, (n + 1) * tn)
            y_ref[:, :, cols] = (y_ref[:, :, cols] * inv) * gf_ref[:, :, cols]


def _out(x, mix_a, mix_b, w_out, gate, g_final, *, nb, t, tn):
    b, s, d = x.shape
    half = mix_a.shape[1]
    tm = nb * t
    assert b % nb == 0 and s % t == 0 and d % tn == 0 and w_out.shape[0] == 2 * half
    nj = d // tn
    nt = s // t
    assert nb == 1 or nt == 1, "a row tile must be contiguous in the flattened (B*S) mixed rows"
    blocks = ([((nb, t, tn), F32)] + [((tm, half), BF16)] * 2 + [((half, tn), BF16)] * 2
              + [((nb, t, d), F32)] + [((tm, tn), F32)] * 4)
    return pl.pallas_call(
        functools.partial(_out_kernel, nj=nj, tn=tn),
        grid=((b // nb) * nt, nj),
        in_specs=[pl.BlockSpec((nb, t, tn), lambda i, j: (i // nt, i % nt, j)),
                  pl.BlockSpec((tm, half), lambda i, j: (i, 0)),
                  pl.BlockSpec((tm, half), lambda i, j: (i, 0)),
                  pl.BlockSpec((half, tn), lambda i, j: (0, j)),
                  pl.BlockSpec((half, tn), lambda i, j: (1, j)),
                  pl.BlockSpec((nb, 1, tn), lambda i, j: (i // nt, 0, j)),
                  pl.BlockSpec((1, 1, d), lambda i, j: (0, 0, 0))],
        out_specs=pl.BlockSpec((nb, t, d), lambda i, j: (i // nt, i % nt, 0)),
        out_shape=jax.ShapeDtypeStruct((b, s, d), F32),
        scratch_shapes=[pltpu.VMEM((nb, t, 1), F32)],
        compiler_params=_params(("arbitrary", "arbitrary"), blocks),
        name="out_proj",
    )(x, mix_a, mix_b, w_out, w_out, gate, g_final)


def kernel(x_prompt, x_sample, cache_k, cache_v, cache_conv, c_prompt, c_sample,
           g_norm, w_ada, b_ada, w_in, conv_w, conv_b, rel_bias, w_out, g_final):
    depth = g_norm.shape[0]
    assert depth == 1, "single-layer trunk"
    bp, sp, d = x_prompt.shape
    bs, ts, _ = x_sample.shape
    assert bp == 1
    d_conv = conv_w.shape[-1]
    d_attn = w_out.shape[1] - d_conv
    nh = d_attn // HEAD_DIM
    r = cache_k.shape[2]
    rows_kept = min(BAND_PAST, sp)

    n_c = bp + bs
    pad = (-n_c) % V7X_SUBLANES
    c_all = jnp.concatenate([c_prompt, c_sample, jnp.zeros((pad, d), F32)], axis=0)
    mod = _ada(c_all, w_ada[0], b_ada)
    shift, scale, gate = (mod[:n_c, i * d:(i + 1) * d].reshape(n_c, 1, d) for i in range(3))

    tn = 256
    g3 = g_norm.reshape(1, 1, d)
    gf3 = g_final.reshape(1, 1, d)
    conv_wb = jnp.concatenate([conv_w[0], conv_b], axis=0)

    hs = _prep(x_sample, g3, scale[bp:], shift[bp:], nb=8, t=ts).reshape(bs * ts, d)
    mix_conv_s, u_tail_s, qs, ks, vs, zs, w_blk = _inproj_sample(
        hs, w_in[0], conv_wb, cache_conv[0], d_conv=d_conv, d_attn=d_attn, nb=bs, t=ts, tn=tn, tc=V7X_LANES)

    tm, tq, halves = 1024, 256, 2
    h_first = _prep(x_prompt, g3, scale[:bp], shift[:bp], nb=1, t=512, s0=0, s_len=tm).reshape(tm, d)
    norm_p = jnp.concatenate([g_norm, scale[0], shift[0]], axis=0)
    mix_conv_p, u_tail_p, qp, kp, vp, zp, k_keep, v_keep, w_out_b = _inproj_prompt(
        x_prompt.reshape(sp, d), norm_p, h_first, w_blk, conv_wb, w_out[0],
        d_conv=d_conv, d_attn=d_attn, tm=tm, tn=tn, halves=halves, n_new=rows_kept)
    bias_p = _bias_table(rel_bias[0], rows=tq, cols=3 * tq, rel0=2 * tq, band_cols=2 * tq)
    mix_attn_p = _attn_prompt(qp, kp, vp, zp, bias_p, tq=tq, heads=8, nq=4)
    y_prompt = _out(x_prompt, mix_conv_p, mix_attn_p, w_out_b, gate[:bp], gf3, nb=1, t=512, tn=1024)

    bias_s =_bias_table(rel_bias[0], rows=ts, cols=r + V7X_LANES * (-(-ts // V7X_LANES)), rel0=r)
    to3 = lambda a: a.reshape(bs, ts, d_attn)
    mix_attn_s = _attn_sample(to3(qs), to3(ks), to3(vs), cache_k[0].reshape(bs, r * nh, HEAD_DIM),
                              cache_v[0].reshape(bs, r * nh, HEAD_DIM), to3(zs), bias_s, nbs=1)
    y_sample = _out(x_sample, mix_conv_s, mix_attn_s.reshape(bs * ts, d_attn), w_out_b, gate[bp:], gf3,
                    nb=16, t=ts, tn=1024)

    keep = CONV_WIDTH - 1
    rows_major = lambda a: jnp.swapaxes(a[-1], 0, 1)
    new_k_prompt = rows_major(k_keep).reshape(1, bp, rows_kept, nh, HEAD_DIM)
    new_v_prompt = rows_major(v_keep).reshape(1, bp, rows_kept, nh, HEAD_DIM)
    new_conv_prompt = u_tail_p[-1, V7X_SUBLANES - keep:, :].reshape(1, bp, keep, d_conv)
    new_k_sample = ks.reshape(1, bs, ts, nh, HEAD_DIM)
    new_v_sample = vs.reshape(1, bs, ts, nh, HEAD_DIM)
    new_conv_sample = u_tail_s[:, V7X_SUBLANES - keep:, :].reshape(1, bs, keep, d_conv)
    return (y_prompt, y_sample, new_k_prompt, new_v_prompt, new_conv_prompt,
            new_k_sample, new_v_sample, new_conv_sample)
```
